```python
import math
import jax
import jax.numpy as jnp
from jax import lax
import numpy as np

D_MODEL = 1024
BATCH = 4
SEQ = 8192
DEPTH = 2

N_MIXERS = 2
N_SWA_LAYERS = (DEPTH + 1) // 2
N_DIFF_LAYERS = DEPTH // 2

HEAD_DIM = 64
QBLK = 128

SWA_HEADS = D_MODEL // HEAD_DIM
SWA_KV_HEADS = SWA_HEADS // 4
SWA_GROUP = SWA_HEADS // SWA_KV_HEADS
WINDOW = 128
KBLK = QBLK + 2 * WINDOW
SWA_QKV_WIDTH = (SWA_HEADS + 2 * SWA_KV_HEADS) * HEAD_DIM

DIFF_HEADS = D_MODEL // (2 * HEAD_DIM)
DIFF_VDIM = 2 * HEAD_DIM
DIFF_QKV_WIDTH = 2 * DIFF_HEADS * HEAD_DIM * 2 + DIFF_HEADS * DIFF_VDIM

N_GROUPS = 4
EXPERTS_PER_GROUP = 8
N_EXPERTS = N_GROUPS * EXPERTS_PER_GROUP
TOP_K = 2
D_EXPERT = D_MODEL // 2
MOE_BLK = 128

NORM_EPS = 1e-6
SUBLN_EPS = 1e-5

kernel_name = "hybrid_swa_diffattn_hmoe_encoder"


def rmsnorm(x, g, eps=NORM_EPS):
    xf = x.astype(jnp.float32)
    y = xf * lax.rsqrt(jnp.mean(xf * xf, axis=-1, keepdims=True) + eps)
    return y.astype(x.dtype) * g


def alibi_slopes(n_heads):
    h = jnp.arange(1, n_heads + 1, dtype=jnp.float32)
    return jnp.exp2(-8.0 * h / n_heads)


def lambda_init_fn(layer_idx):
    return 0.8 - 0.6 * math.exp(-0.3 * layer_idx)


def windowed_gqa_attention(h, w_qkv, w_o, sink):
    b, s, _ = h.shape
    nb = s // QBLK
    qkv = h @ w_qkv
    q = qkv[..., :SWA_HEADS * HEAD_DIM]
    k = qkv[..., SWA_HEADS * HEAD_DIM:(SWA_HEADS + SWA_KV_HEADS) * HEAD_DIM]
    v = qkv[..., (SWA_HEADS + SWA_KV_HEADS) * HEAD_DIM:]
    q = q.reshape(b, s, SWA_KV_HEADS, SWA_GROUP, HEAD_DIM) * (HEAD_DIM ** -0.5)
    k = k.reshape(b, s, SWA_KV_HEADS, HEAD_DIM)
    v = v.reshape(b, s, SWA_KV_HEADS, HEAD_DIM)
    pad = ((0, 0), (WINDOW, WINDOW), (0, 0), (0, 0))
    kp = jnp.pad(k, pad)
    vp = jnp.pad(v, pad)
    qb = q.reshape(b, nb, QBLK, SWA_KV_HEADS, SWA_GROUP, HEAD_DIM).transpose(1, 0, 2, 3, 4, 5)
    slopes = alibi_slopes(SWA_HEADS).reshape(SWA_KV_HEADS, SWA_GROUP)[:, :, None, None]
    sink_l = sink.astype(jnp.float32).reshape(SWA_KV_HEADS, SWA_GROUP)[:, :, None, None]

    def block(args):
        qi, i = args
        start = i * QBLK
        kb = lax.dynamic_slice_in_dim(kp, start, KBLK, axis=1)
        vb = lax.dynamic_slice_in_dim(vp, start, KBLK, axis=1)
        qpos = start + jnp.arange(QBLK)
        kpos = start - WINDOW + jnp.arange(KBLK)
        dist = jnp.abs(qpos[:, None] - kpos[None, :])
        valid = (kpos[None, :] >= 0) & (kpos[None, :] < s) & (dist <= WINDOW)
        logits = jnp.einsum('bqkgd,bskd->bkgqs', qi, kb).astype(jnp.float32)
        logits = logits - slopes * dist.astype(jnp.float32)
        logits = jnp.where(valid, logits, -jnp.inf)
        m = jnp.maximum(jnp.max(logits, axis=-1, keepdims=True), sink_l)
        p = jnp.exp(logits - m)
        denom = jnp.sum(p, axis=-1, keepdims=True) + jnp.exp(sink_l - m)
        probs = (p / denom).astype(vb.dtype)
        return jnp.einsum('bkgqs,bskd->bqkgd', probs, vb)

    out = lax.map(block, (qb, jnp.arange(nb)))
    out = out.transpose(1, 0, 2, 3, 4, 5).reshape(b, s, SWA_HEADS * HEAD_DIM)
    return out @ w_o


def differential_attention(h, w_qkv, w_o, lam, subln_g, lambda_init):
    b, s, _ = h.shape
    nb = s // QBLK
    qk_w = 2 * DIFF_HEADS * HEAD_DIM
    qkv = h @ w_qkv
    q = qkv[..., :qk_w].reshape(b, s, DIFF_HEADS, 2, HEAD_DIM) * (HEAD_DIM ** -0.5)
    k = qkv[..., qk_w:2 * qk_w].reshape(b, s, DIFF_HEADS, 2, HEAD_DIM)
    v = qkv[..., 2 * qk_w:].reshape(b, s, DIFF_HEADS, DIFF_VDIM)
    lamf = lam.astype(jnp.float32)
    lam_full = (jnp.exp(jnp.sum(lamf[0] * lamf[1])) - jnp.exp(jnp.sum(lamf[2] * lamf[3]))
                + lambda_init)
    slopes = alibi_slopes(DIFF_HEADS)[None, :, None, None, None]
    kpos = jnp.arange(s)
    qb = q.reshape(b, nb, QBLK, DIFF_HEADS, 2, HEAD_DIM).transpose(1, 0, 2, 3, 4, 5)

    def block(args):
        qi, i = args
        qpos = i * QBLK + jnp.arange(QBLK)
        dist = jnp.abs(qpos[:, None] - kpos[None, :]).astype(jnp.float32)
        logits = jnp.einsum('bqhcd,bshcd->bhcqs', qi, k).astype(jnp.float32) - slopes * dist
        a = jax.nn.softmax(logits, axis=-1)
        wmap = a[:, :, 0] - lam_full * a[:, :, 1]
        return jnp.einsum('bhqs,bshe->bqhe', wmap.astype(v.dtype), v)

    o = lax.map(block, (qb, jnp.arange(nb)))
    o = o.transpose(1, 0, 2, 3, 4).reshape(b, s, DIFF_HEADS, DIFF_VDIM)
    o = rmsnorm(o, subln_g, eps=SUBLN_EPS) * (1.0 - lambda_init)
    return o.reshape(b, s, DIFF_HEADS * DIFF_VDIM) @ w_o


def hierarchical_moe(h, w_group, b_group, w_router, b_router, w_gate, w_up, w_down):
    b, s, d = h.shape
    x2 = h.reshape(b * s, d)
    n = x2.shape[0]
    lg = (x2 @ w_group + b_group).astype(jnp.float32)
    pg_all = jax.nn.softmax(lg, axis=-1)
    g = jnp.argmax(lg, axis=-1)
    pg = jnp.take_along_axis(pg_all, g[:, None], axis=1)[:, 0]
    le = (x2 @ w_router + b_router).astype(jnp.float32).reshape(n, N_GROUPS, EXPERTS_PER_GROUP)
    le = jnp.take_along_axis(le, g[:, None, None], axis=1)[:, 0]
    pe = jax.nn.softmax(le, axis=-1)
    top_p, top_i = lax.top_k(pe, TOP_K)
    top_p = top_p / jnp.sum(top_p, axis=-1, keepdims=True)
    weights = pg[:, None] * top_p
    eid = g[:, None] * EXPERTS_PER_GROUP + top_i

    flat_e = eid.reshape(-1)
    flat_w = weights.reshape(-1)
    flat_t = jnp.arange(n * TOP_K) // TOP_K
    order = jnp.argsort(flat_e)
    se, st, sw = flat_e[order], flat_t[order], flat_w[order]
    counts = jnp.zeros((N_EXPERTS,), jnp.int32).at[flat_e].add(1)
    padded = ((counts + MOE_BLK - 1) // MOE_BLK) * MOE_BLK
    start = jnp.cumsum(counts) - counts
    pend = jnp.cumsum(padded)
    pstart = pend - padded
    dest = pstart[se] + (jnp.arange(n * TOP_K) - start[se])
    cap = n * TOP_K + N_EXPERTS * MOE_BLK
    n_blocks = cap // MOE_BLK
    slot_tok = jnp.zeros((cap,), jnp.int32).at[dest].set(st)
    slot_w = jnp.zeros((cap,), x2.dtype).at[dest].set(sw.astype(x2.dtype))
    block_e = jnp.minimum(jnp.searchsorted(pend, jnp.arange(n_blocks) * MOE_BLK, side='right'),
                          N_EXPERTS - 1)
    xs = x2[slot_tok].reshape(n_blocks, MOE_BLK, d)

    def expert_block(args):
        xb, e = args
        hid = jax.nn.silu(xb @ w_gate[e]) * (xb @ w_up[e])
        return hid @ w_down[e]

    ys = lax.map(expert_block, (xs, block_e)).reshape(cap, d)
    out = jnp.zeros_like(x2).at[slot_tok].add(ys * slot_w[:, None])
    return out.reshape(b, s, d)


def setup_inputs(seed: int = 0) -> dict:
    key = jax.random.key(seed)
    ks = jax.random.split(key, 20)
    f32 = jnp.float32
    nrm = lambda k, shape, scale: jax.random.normal(k, shape, f32) * scale
    return {
        "x": jax.random.normal(ks[0], (BATCH, SEQ, D_MODEL), f32),
        "norm_mix": 1.0 + nrm(ks[1], (DEPTH, D_MODEL), 0.02),
        "norm_ffn": 1.0 + nrm(ks[2], (DEPTH, D_MODEL), 0.02),
        "swa_w_qkv": nrm(ks[3], (N_SWA_LAYERS, D_MODEL, SWA_QKV_WIDTH), D_MODEL ** -0.5),
        "swa_w_o": nrm(ks[4], (N_SWA_LAYERS, SWA_HEADS * HEAD_DIM, D_MODEL), (SWA_HEADS * HEAD_DIM) ** -0.5),
        "swa_sink": nrm(ks[5], (N_SWA_LAYERS, SWA_HEADS), 0.5),
        "diff_w_qkv": nrm(ks[6], (N_DIFF_LAYERS, D_MODEL, DIFF_QKV_WIDTH), D_MODEL ** -0.5),
        "diff_w_o": nrm(ks[7], (N_DIFF_LAYERS, DIFF_HEADS * DIFF_VDIM, D_MODEL), (DIFF_HEADS * DIFF_VDIM) ** -0.5),
        "diff_lambda": nrm(ks[8], (N_DIFF_LAYERS, 4, HEAD_DIM), 0.1),
        "diff_subln": 1.0 + nrm(ks[9], (N_DIFF_LAYERS, DIFF_VDIM), 0.02),
        "moe_w_group": nrm(ks[10], (DEPTH, D_MODEL, N_GROUPS), D_MODEL ** -0.5),
        "moe_b_group": nrm(ks[11], (DEPTH, N_GROUPS), 0.01),
        "moe_w_router": nrm(ks[12], (DEPTH, D_MODEL, N_EXPERTS), D_MODEL ** -0.5),
        "moe_b_router": nrm(ks[13], (DEPTH, N_EXPERTS), 0.01),
        "moe_w_gate": nrm(ks[14], (DEPTH, N_EXPERTS, D_MODEL, D_EXPERT), D_MODEL ** -0.5),
        "moe_w_up": nrm(ks[15], (DEPTH, N_EXPERTS, D_MODEL, D_EXPERT), D_MODEL ** -0.5),
        "moe_w_down": nrm(ks[16], (DEPTH, N_EXPERTS, D_EXPERT, D_MODEL), D_EXPERT ** -0.5),
        "final_norm": 1.0 + nrm(ks[17], (D_MODEL,), 0.02),
    }


def reference(x, norm_mix, norm_ffn, swa_w_qkv, swa_w_o, swa_sink, diff_w_qkv, diff_w_o,
              diff_lambda, diff_subln, moe_w_group, moe_b_group, moe_w_router, moe_b_router,
              moe_w_gate, moe_w_up, moe_w_down, final_norm):
    for i in range(DEPTH):
        hn = rmsnorm(x, norm_mix[i])
        j = i // N_MIXERS
        if i % N_MIXERS == 0:
            x = x + windowed_gqa_attention(hn, swa_w_qkv[j], swa_w_o[j], swa_sink[j])
        else:
            x = x + differential_attention(hn, diff_w_qkv[j], diff_w_o[j], diff_lambda[j],
                                           diff_subln[j], lambda_init_fn(i))
        x = x + hierarchical_moe(rmsnorm(x, norm_ffn[i]), moe_w_group[i], moe_b_group[i],
                                 moe_w_router[i], moe_b_router[i], moe_w_gate[i],
                                 moe_w_up[i], moe_w_down[i])
    return rmsnorm(x, final_norm)
```

```python
import functools
import math

import jax
import jax.numpy as jnp
from jax import lax
from jax.experimental import pallas as pl
from jax.experimental.pallas import tpu as pltpu

F32 = jnp.float32
BF16 = jnp.bfloat16

HEAD_DIM = 64
WINDOW = 128
SWA_KV_HEADS = 4
SWA_GROUP = 4
N_GROUPS = 4
EXPERTS_PER_GROUP = 8
N_EXPERTS = N_GROUPS * EXPERTS_PER_GROUP
NORM_EPS = 1e-6
SUBLN_EPS = 1e-5
NEG = -1e30
LANES = 128

PROJ_ROWS = 512
SWA_Q = 128
ROUTER_ROWS = 512
MOE_ROWS = 256
MOVE_ROWS = 256
DIFF_Q = 512
DIFF_K = 256
VMEM_LIMIT = 56 * 1024 * 1024


def _params(*sem):
    return pltpu.CompilerParams(dimension_semantics=sem, vmem_limit_bytes=VMEM_LIMIT)


def _rms(x, g, eps):
    ms = jnp.mean(x * x, axis=-1, keepdims=True)
    return x * lax.rsqrt(ms + eps) * g


def _swa_proj_kernel(x_ref, g_ref, w_ref, o_ref):
    xn = _rms(x_ref[...], g_ref[...], NORM_EPS).astype(BF16)
    o_ref[...] = jnp.dot(xn, w_ref[...], preferred_element_type=F32).astype(BF16)


def _swa_proj(x2, g, w):
    n, d = x2.shape
    width = w.shape[1]
    return pl.pallas_call(
        _swa_proj_kernel,
        grid=(n // PROJ_ROWS,),
        in_specs=[pl.BlockSpec((PROJ_ROWS, d), lambda i: (i, 0)),
                  pl.BlockSpec((1, d), lambda i: (0, 0)),
                  pl.BlockSpec((d, width), lambda i: (0, 0))],
        out_specs=pl.BlockSpec((PROJ_ROWS, width), lambda i: (i, 0)),
        out_shape=jax.ShapeDtypeStruct((n, width), BF16),
        compiler_params=_params("parallel"),
        name="swa_proj",
    )(x2, g, w)


def _out_proj_kernel(a_ref, w_ref, x_ref, o_ref):
    o_ref[...] = x_ref[...] + jnp.dot(a_ref[...], w_ref[...], preferred_element_type=F32)


def _out_proj(a, w, x2):
    n, d = x2.shape
    return pl.pallas_call(
        _out_proj_kernel,
        grid=(n // PROJ_ROWS,),
        in_specs=[pl.BlockSpec((PROJ_ROWS, a.shape[1]), lambda i: (i, 0)),
                  pl.BlockSpec(w.shape, lambda i: (0, 0)),
                  pl.BlockSpec((PROJ_ROWS, d), lambda i: (i, 0))],
        out_specs=pl.BlockSpec((PROJ_ROWS, d), lambda i: (i, 0)),
        out_shape=jax.ShapeDtypeStruct((n, d), F32),
        compiler_params=_params("parallel"),
        name="out_proj",
    )(a, w, x2)


def _swa_attn_kernel(slope_ref, sink_ref, q_ref, kp_ref, kc_ref, kn_ref, vp_ref, vc_ref, vn_ref, o_ref):
    i = pl.program_id(1)
    last = pl.num_programs(1) - 1
    tq = SWA_Q
    row = lax.broadcasted_iota(jnp.int32, (tq, 3 * tq), 0)
    col = lax.broadcasted_iota(jnp.int32, (tq, 3 * tq), 1)
    dist = jnp.abs(col - tq - row)
    valid = dist <= WINDOW
    valid = valid & jnp.logical_not((col < tq) & (i == 0))
    valid = valid & jnp.logical_not((col >= 2 * tq) & (i == last))
    distf = dist.astype(F32)
    for j in range(SWA_KV_HEADS):
        ks = slice(j * HEAD_DIM, (j + 1) * HEAD_DIM)
        kj = jnp.concatenate([kp_ref[:, ks], kc_ref[:, ks], kn_ref[:, ks]], axis=0)
        vj = jnp.concatenate([vp_ref[:, ks], vc_ref[:, ks], vn_ref[:, ks]], axis=0)
        for g in range(SWA_GROUP):
            h = j * SWA_GROUP + g
            hs = slice(h * HEAD_DIM, (h + 1) * HEAD_DIM)
            s = lax.dot_general(q_ref[:, hs], kj, (((1,), (1,)), ((), ())),
                                preferred_element_type=F32)
            s = jnp.where(valid, s - slope_ref[h] * distf, NEG)
            sink = sink_ref[h]
            m = jnp.maximum(jnp.max(s, axis=-1, keepdims=True), sink)
            p = jnp.exp(s - m)
            denom = jnp.sum(p, axis=-1, keepdims=True) + jnp.exp(sink - m)
            probs = (p / denom).astype(BF16)
            o_ref[:, hs] = jnp.dot(probs, vj, preferred_element_type=F32).astype(BF16)


def _swa_attn(qkv, slopes, sink, batch, seq):
    n_heads = SWA_KV_HEADS * SWA_GROUP
    qw = n_heads * HEAD_DIM
    kvw = SWA_KV_HEADS * HEAD_DIM
    nb = seq // SWA_Q
    qkv3 = qkv.reshape(batch, seq, qw + 2 * kvw)
    kcol = qw // kvw
    vcol = kcol + 1
    smem = pl.BlockSpec(memory_space=pltpu.SMEM)

    def kv_spec(colblk, shift):
        def index(b, i):
            return (b, jnp.clip(i + shift, 0, nb - 1), colblk)
        return pl.BlockSpec((None, SWA_Q, kvw), index)

    out = pl.pallas_call(
        _swa_attn_kernel,
        grid=(batch, nb),
        in_specs=[smem, smem,
                  pl.BlockSpec((None, SWA_Q, qw), lambda b, i: (b, i, 0)),
                  kv_spec(kcol, -1), kv_spec(kcol, 0), kv_spec(kcol, 1),
                  kv_spec(vcol, -1), kv_spec(vcol, 0), kv_spec(vcol, 1)],
        out_specs=pl.BlockSpec((None, SWA_Q, qw), lambda b, i: (b, i, 0)),
        out_shape=jax.ShapeDtypeStruct((batch, seq, qw), BF16),
        compiler_params=_params("parallel", "parallel"),
        name="swa_attn",
    )(slopes, sink, qkv3, qkv3, qkv3, qkv3, qkv3, qkv3, qkv3)
    return out.reshape(batch * seq, qw)


def _router_kernel(x_ref, g_ref, w_ref, b_ref, meta_ref, metat_ref, cnt_ref, carry_ref):
    t = pl.program_id(0)
    rows = x_ref.shape[0]

    @pl.when(t == 0)
    def _():
        carry_ref[...] = jnp.zeros_like(carry_ref)

    xn = _rms(x_ref[...], g_ref[...], NORM_EPS).astype(BF16)
    logits = jnp.dot(xn, w_ref[...], preferred_element_type=F32) + b_ref[...]
    lane = lax.broadcasted_iota(jnp.int32, (rows, LANES), 1)

    is_group = lane < N_GROUPS
    gl = jnp.where(is_group, logits, NEG)
    gmax = jnp.max(gl, axis=-1, keepdims=True)
    gidx = jnp.min(jnp.where(gl == gmax, lane, LANES), axis=-1, keepdims=True)
    zg = jnp.sum(jnp.where(is_group, jnp.exp(gl - gmax), 0.0), axis=-1, keepdims=True)
    pg = 1.0 / zg

    lo = N_GROUPS + EXPERTS_PER_GROUP * gidx
    in_group = (lane >= lo) & (lane < lo + EXPERTS_PER_GROUP)
    el = jnp.where(in_group, logits, NEG)
    m0 = jnp.max(el, axis=-1, keepdims=True)
    i0 = jnp.min(jnp.where(el == m0, lane, LANES), axis=-1, keepdims=True)
    el1 = jnp.where(lane == i0, NEG, el)
    m1 = jnp.max(el1, axis=-1, keepdims=True)
    i1 = jnp.min(jnp.where(el1 == m1, lane, LANES), axis=-1, keepdims=True)
    r = jnp.exp(m1 - m0)
    w0 = pg / (1.0 + r)
    w1 = pg * r / (1.0 + r)
    e0 = i0 - N_GROUPS
    e1 = i1 - N_GROUPS

    sel0 = lane == e0
    sel1 = lane == e1
    onehot = (sel0 | sel1).astype(BF16)
    rr = lax.broadcasted_iota(jnp.int32, (rows, rows), 0)
    cc = lax.broadcasted_iota(jnp.int32, (rows, rows), 1)
    tri = (cc < rr).astype(BF16)
    before = jnp.dot(tri, onehot, preferred_element_type=F32) + carry_ref[...]
    r0 = jnp.sum(jnp.where(sel0, before, 0.0), axis=-1, keepdims=True)
    r1 = jnp.sum(jnp.where(sel1, before, 0.0), axis=-1, keepdims=True)
    carry_ref[...] = carry_ref[...] + jnp.sum(onehot.astype(F32), axis=0, keepdims=True)
    cnt_ref[...] = carry_ref[...]

    meta = jnp.where(lane == 0, e0.astype(F32), 0.0)
    meta = jnp.where(lane == 1, e1.astype(F32), meta)
    meta = jnp.where(lane == 2, w0, meta)
    meta = jnp.where(lane == 3, w1, meta)
    meta = jnp.where(lane == 4, r0, meta)
    meta = jnp.where(lane == 5, r1, meta)
    meta_ref[...] = meta
    metat_ref[...] = meta.T[:8, :]


def _router(x2, g, w_pad, b_pad):
    n, d = x2.shape
    rows = ROUTER_ROWS
    return pl.pallas_call(
        _router_kernel,
        grid=(n // rows,),
        in_specs=[pl.BlockSpec((rows, d), lambda t: (t, 0)),
                  pl.BlockSpec((1, d), lambda t: (0, 0)),
                  pl.BlockSpec((d, LANES), lambda t: (0, 0)),
                  pl.BlockSpec((1, LANES), lambda t: (0, 0))],
        out_specs=[pl.BlockSpec((rows, LANES), lambda t: (t, 0)),
                   pl.BlockSpec((8, rows), lambda t: (0, t)),
                   pl.BlockSpec((1, LANES), lambda t: (0, 0))],
        out_shape=[jax.ShapeDtypeStruct((n, LANES), F32),
                   jax.ShapeDtypeStruct((8, n), F32),
                   jax.ShapeDtypeStruct((1, LANES), F32)],
        scratch_shapes=[pltpu.VMEM((1, LANES), F32)],
        compiler_params=_params("arbitrary"),
        name="router",
    )(x2, g, w_pad, b_pad)


def _row_copy(src_hbm, src_row, dst, dst_row, sem):
    return pltpu.make_async_copy(src_hbm.at[pl.ds(src_row, 1)], dst.at[pl.ds(dst_row, 1)], sem)


def _dispatch_kernel(dest_ref, x_hbm, xs_in_hbm, xs_hbm, sem):
    del xs_in_hbm
    rows = dest_ref.shape[-1] // 2
    base = pl.program_id(0) * rows

    def start(r, c):
        _row_copy(x_hbm, base + r, xs_hbm, dest_ref[0, 0, 2 * r], sem).start()
        _row_copy(x_hbm, base + r, xs_hbm, dest_ref[0, 0, 2 * r + 1], sem).start()
        return c

    def wait(r, c):
        _row_copy(x_hbm, 0, xs_hbm, 0, sem).wait()
        _row_copy(x_hbm, 0, xs_hbm, 0, sem).wait()
        return c

    lax.fori_loop(0, rows, start, 0)
    lax.fori_loop(0, rows, wait, 0)


def _dispatch(x2, dest, cap):
    n, d = x2.shape
    steps = n // MOVE_ROWS
    dest3 = dest.reshape(steps, 1, 2 * MOVE_ROWS)
    any_spec = pl.BlockSpec(memory_space=pl.ANY)
    return pl.pallas_call(
        _dispatch_kernel,
        grid=(steps,),
        in_specs=[pl.BlockSpec((1, 1, 2 * MOVE_ROWS), lambda t: (t, 0, 0), memory_space=pltpu.SMEM),
                  any_spec, any_spec],
        out_specs=any_spec,
        out_shape=jax.ShapeDtypeStruct((cap, d), F32),
        scratch_shapes=[pltpu.SemaphoreType.DMA(())],
        input_output_aliases={2: 0},
        compiler_params=pltpu.CompilerParams(dimension_semantics=("arbitrary",),
                                             has_side_effects=True),
        name="moe_dispatch",
    )(dest3, x2, jnp.zeros((cap, d), F32))


def _expert_kernel(be_ref, nused_ref, xs_ref, g_ref, wg_ref, wu_ref, wd_ref, ys_ref, wgb, wub, wdb):
    b = pl.program_id(0)
    e = be_ref[b]
    prev = be_ref[jnp.maximum(b - 1, 0)]

    @pl.when((b == 0) | (e != prev))
    def _():
        wgb[...] = wg_ref[...].astype(BF16)
        wub[...] = wu_ref[...].astype(BF16)
        wdb[...] = wd_ref[...].astype(BF16)

    @pl.when(b < nused_ref[0])
    def _():
        xn = _rms(xs_ref[...], g_ref[...], NORM_EPS).astype(BF16)
        hg = jnp.dot(xn, wgb[...], preferred_element_type=F32)
        hu = jnp.dot(xn, wub[...], preferred_element_type=F32)
        hid = (hg / (1.0 + jnp.exp(-hg))) * hu
        ys_ref[...] = jnp.dot(hid.astype(BF16), wdb[...], preferred_element_type=F32)

    @pl.when(b >= nused_ref[0])
    def _():
        ys_ref[...] = jnp.zeros_like(ys_ref)


def _experts(xs, g, w_gate, w_up, w_down, block_e, n_used):
    cap, d = xs.shape
    de = w_gate.shape[-1]
    rows = MOE_ROWS
    grid_spec = pltpu.PrefetchScalarGridSpec(
        num_scalar_prefetch=2,
        grid=(cap // rows,),
        in_specs=[pl.BlockSpec((rows, d), lambda b, be, nu: (b, 0)),
                  pl.BlockSpec((1, d), lambda b, be, nu: (0, 0)),
                  pl.BlockSpec((None, d, de), lambda b, be, nu: (be[b], 0, 0)),
                  pl.BlockSpec((None, d, de), lambda b, be, nu: (be[b], 0, 0)),
                  pl.BlockSpec((None, de, d), lambda b, be, nu: (be[b], 0, 0))],
        out_specs=pl.BlockSpec((rows, d), lambda b, be, nu: (b, 0)),
        scratch_shapes=[pltpu.VMEM((d, de), BF16), pltpu.VMEM((d, de), BF16), pltpu.VMEM((de, d), BF16)],
    )
    return pl.pallas_call(
        _expert_kernel,
        grid_spec=grid_spec,
        out_shape=jax.ShapeDtypeStruct((cap, d), F32),
        compiler_params=_params("arbitrary"),
        name="moe_experts",
    )(block_e, n_used, xs, g, w_gate, w_up, w_down)


def _combine_kernel(dest_ref, x_ref, meta_ref, g_ref, ys_hbm, o_ref, ybuf, sem, *, final_norm):
    rows = x_ref.shape[0]

    def start(r, c):
        _row_copy(ys_hbm, dest_ref[0, 0, 2 * r], ybuf.at[0], r, sem).start()
        _row_copy(ys_hbm, dest_ref[0, 0, 2 * r + 1], ybuf.at[1], r, sem).start()
        return c

    def wait(r, c):
        _row_copy(ys_hbm, 0, ybuf.at[0], 0, sem).wait()
        _row_copy(ys_hbm, 0, ybuf.at[1], 0, sem).wait()
        return c

    lax.fori_loop(0, rows, start, 0)
    lax.fori_loop(0, rows, wait, 0)
    meta = meta_ref[...]
    out = x_ref[...] + (meta[:, 2:3] * ybuf[0] + meta[:, 3:4] * ybuf[1])
    if final_norm:
        out = _rms(out, g_ref[...], NORM_EPS)
    o_ref[...] = out


def _combine(x2, meta, dest, ys, g, final_norm):
    n, d = x2.shape
    rows = MOVE_ROWS
    steps = n // rows
    dest3 = dest.reshape(steps, 1, 2 * rows)
    return pl.pallas_call(
        functools.partial(_combine_kernel, final_norm=final_norm),
        grid=(steps,),
        in_specs=[pl.BlockSpec((1, 1, 2 * rows), lambda t: (t, 0, 0), memory_space=pltpu.SMEM),
                  pl.BlockSpec((rows, d), lambda t: (t, 0)),
                  pl.BlockSpec((rows, LANES), lambda t: (t, 0)),
                  pl.BlockSpec((1, d), lambda t: (0, 0)),
                  pl.BlockSpec(memory_space=pl.ANY)],
        out_specs=pl.BlockSpec((rows, d), lambda t: (t, 0)),
        out_shape=jax.ShapeDtypeStruct((n, d), F32),
        scratch_shapes=[pltpu.VMEM((2, rows, d), F32), pltpu.SemaphoreType.DMA(())],
        compiler_params=_params("arbitrary"),
        name="moe_combine",
    )(dest3, x2, meta, g, ys)


def _moe(x2, g, w_group, b_group, w_router, b_router, w_gate, w_up, w_down, g_final, final_norm):
    n, d = x2.shape
    pad = LANES - N_GROUPS - N_EXPERTS
    w_pad = jnp.concatenate([w_group, w_router, jnp.zeros((d, pad), F32)], axis=1).astype(BF16)
    b_pad = jnp.concatenate([b_group, b_router, jnp.zeros((pad,), F32)])[None, :]
    meta, metat, cnt = _router(x2, g, w_pad, b_pad)

    counts = cnt[0, :N_EXPERTS].astype(jnp.int32)
    padded = ((counts + MOE_ROWS - 1) // MOE_ROWS) * MOE_ROWS
    pend = jnp.cumsum(padded)
    pstart = pend - padded
    e01 = metat[0:2].astype(jnp.int32)
    r01 = metat[4:6].astype(jnp.int32)
    dest = (pstart[e01] + r01).T.reshape(-1)
    cap = 2 * n + N_EXPERTS * MOE_ROWS
    n_blocks = cap // MOE_ROWS
    block_e = jnp.minimum(jnp.searchsorted(pend, jnp.arange(n_blocks) * MOE_ROWS, side="right"),
                          N_EXPERTS - 1).astype(jnp.int32)
    n_used = (pend[-1:] // MOE_ROWS).astype(jnp.int32)

    xs = _dispatch(x2, dest, cap)
    ys = _experts(xs, g, w_gate, w_up, w_down, block_e, n_used)
    return _combine(x2, meta, dest, ys, g_final, final_norm)


def _diff_proj_kernel(x_ref, g_ref, wqt_ref, wk_ref, wvt_ref, qt_ref, k_ref, vt_ref):
    xn = _rms(x_ref[...], g_ref[...], NORM_EPS).astype(BF16)
    nt = (((1,), (1,)), ((), ()))
    qt_ref[...] = lax.dot_general(wqt_ref[...], xn, nt, preferred_element_type=F32).astype(BF16)
    k_ref[...] = jnp.dot(xn, wk_ref[...], preferred_element_type=F32).astype(BF16)
    vt_ref[...] = lax.dot_general(wvt_ref[...], xn, nt, preferred_element_type=F32).astype(BF16)


def _diff_proj(x3, g, wqt, wk, wvt):
    batch, seq, d = x3.shape
    rows = PROJ_ROWS
    wspec = pl.BlockSpec((d, d), lambda b, i: (0, 0))
    tspec = pl.BlockSpec((None, d, rows), lambda b, i: (b, 0, i))
    return pl.pallas_call(
        _diff_proj_kernel,
        grid=(batch, seq // rows),
        in_specs=[pl.BlockSpec((None, rows, d), lambda b, i: (b, i, 0)),
                  pl.BlockSpec((1, d), lambda b, i: (0, 0)),
                  wspec, wspec, wspec],
        out_specs=[tspec, pl.BlockSpec((None, rows, d), lambda b, i: (b, i, 0)), tspec],
        out_shape=[jax.ShapeDtypeStruct((batch, d, seq), BF16),
                   jax.ShapeDtypeStruct((batch, seq, d), BF16),
                   jax.ShapeDtypeStruct((batch, d, seq), BF16)],
        compiler_params=_params("parallel", "parallel"),
        name="diff_proj",
    )(x3, g, wqt, wk, wvt)


def _diff_attn_kernel(slope_ref, lam_ref, subg_ref, qt_ref, k_ref, vt_ref, o_ref,
                      m_ref, l_ref, acc_ref, *, lambda_init):
    h = pl.program_id(1)
    i = pl.program_id(2)
    tq = qt_ref.shape[-1]
    tk = DIFF_K
    n_k = k_ref.shape[0] // tk
    slope = slope_ref[h]
    zeros = jnp.zeros((HEAD_DIM, tq), BF16)
    q_maps = (jnp.concatenate([qt_ref[0], zeros], axis=0),
              jnp.concatenate([zeros, qt_ref[1]], axis=0))
    qpos = i * tq + lax.broadcasted_iota(jnp.int32, (1, tq), 1)
    m_ref[...] = jnp.full_like(m_ref, NEG)
    l_ref[...] = jnp.zeros_like(l_ref)
    acc_ref[...] = jnp.zeros_like(acc_ref)

    def tile(j, carry):
        start = pl.multiple_of(j * tk, tk)
        kb = k_ref[pl.ds(start, tk), :]
        vb = vt_ref[:, pl.ds(start, tk)]
        kpos = start + lax.broadcasted_iota(jnp.int32, (tk, 1), 0)
        bias = slope * jnp.abs(kpos - qpos).astype(F32)
        for c in range(2):
            s = jnp.dot(kb, q_maps[c], preferred_element_type=F32) - bias
            m_old = m_ref[c]
            m_new = jnp.maximum(m_old, jnp.max(s, axis=0, keepdims=True))
            alpha = jnp.exp(m_old - m_new)
            p = jnp.exp(s - m_new)
            l_ref[c] = alpha * l_ref[c] + jnp.sum(p, axis=0, keepdims=True)
            acc_ref[c] = alpha * acc_ref[c] + jnp.dot(vb, p.astype(BF16), preferred_element_type=F32)
            m_ref[c] = m_new
        return carry

    lax.fori_loop(0, n_k, tile, 0)

    lam = lam_ref[...]
    lam_full = (jnp.exp(jnp.sum(lam[0:1] * lam[1:2], axis=-1, keepdims=True))
                - jnp.exp(jnp.sum(lam[2:3] * lam[3:4], axis=-1, keepdims=True)) + lambda_init)
    o = acc_ref[0] / l_ref[0] - lam_full * (acc_ref[1] / l_ref[1])
    ms = jnp.mean(o * o, axis=0, keepdims=True)
    on = o * lax.rsqrt(ms + SUBLN_EPS) * subg_ref[...] * (1.0 - lambda_init)
    o_ref[...] = on.T.astype(BF16)


def _diff_attn(qt, k, vt, slopes, lam, subg, lambda_init):
    batch, heads, _, _, seq = qt.shape
    vdim = vt.shape[2]
    tq = DIFF_Q
    smem = pl.BlockSpec(memory_space=pltpu.SMEM)
    return pl.pallas_call(
        functools.partial(_diff_attn_kernel, lambda_init=lambda_init),
        grid=(batch, heads, seq // tq),
        in_specs=[smem,
                  pl.BlockSpec(lam.shape, lambda b, h, i: (0, 0)),
                  pl.BlockSpec(subg.shape, lambda b, h, i: (0, 0)),
                  pl.BlockSpec((None, None, 2, HEAD_DIM, tq), lambda b, h, i: (b, h, 0, 0, i)),
                  pl.BlockSpec((None, seq, 2 * HEAD_DIM), lambda b, h, i: (b, 0, h)),
                  pl.BlockSpec((None, None, vdim, seq), lambda b, h, i: (b, h, 0, 0))],
        out_specs=pl.BlockSpec((None, tq, vdim), lambda b, h, i: (b, i, h)),
        out_shape=jax.ShapeDtypeStruct((batch, seq, heads * vdim), BF16),
        scratch_shapes=[pltpu.VMEM((2, 1, tq), F32), pltpu.VMEM((2, 1, tq), F32),
                        pltpu.VMEM((2, vdim, tq), F32)],
        compiler_params=_params("parallel", "parallel", "arbitrary"),
        name="diff_attn",
    )(slopes, lam, subg, qt, k, vt)


def _alibi_slopes(n_heads):
    h = jnp.arange(1, n_heads + 1, dtype=F32)
    return jnp.exp2(-8.0 * h / n_heads)


def _lambda_init(layer_idx):
    return 0.8 - 0.6 * math.exp(-0.3 * layer_idx)


def kernel(x, norm_mix, norm_ffn, swa_w_qkv, swa_w_o, swa_sink, diff_w_qkv, diff_w_o, diff_lambda,
           diff_subln, moe_w_group, moe_b_group, moe_w_router, moe_b_router, moe_w_gate, moe_w_up,
           moe_w_down, final_norm):
    batch, seq, d = x.shape
    n = batch * seq
    depth = norm_mix.shape[0]
    x2 = x.reshape(n, d)
    q_scale = HEAD_DIM ** -0.5
    for layer in range(depth):
        j = layer // 2
        g_mix = norm_mix[layer][None, :]
        if layer % 2 == 0:
            n_heads = SWA_KV_HEADS * SWA_GROUP
            qw = n_heads * HEAD_DIM
            w = swa_w_qkv[j]
            w = jnp.concatenate([w[:, :qw] * q_scale, w[:, qw:]], axis=1).astype(BF16)
            qkv = _swa_proj(x2, g_mix, w)
            attn = _swa_attn(qkv, _alibi_slopes(n_heads), swa_sink[j].astype(F32), batch, seq)
            x2 = _out_proj(attn, swa_w_o[j].astype(BF16), x2)
        else:
            heads = d // (2 * HEAD_DIM)
            w = diff_w_qkv[j]
            wqt = (w[:, :d] * q_scale).T.astype(BF16)
            wk = w[:, d:2 * d].astype(BF16)
            wvt = w[:, 2 * d:].T.astype(BF16)
            qt, k, vt = _diff_proj(x2.reshape(batch, seq, d), g_mix, wqt, wk, wvt)
            qt = qt.reshape(batch, heads, 2, HEAD_DIM, seq)
            vt = vt.reshape(batch, heads, 2 * HEAD_DIM, seq)
            attn = _diff_attn(qt, k, vt, _alibi_slopes(heads), diff_lambda[j].astype(F32),
                              diff_subln[j].astype(F32)[:, None], _lambda_init(layer))
            x2 = _out_proj(attn.reshape(n, d), diff_w_o[j].astype(BF16), x2)
        last = layer == depth - 1
        x2 = _moe(x2, norm_ffn[layer][None, :], moe_w_group[layer], moe_b_group[layer],
                  moe_w_router[layer], moe_b_router[layer], moe_w_gate[layer], moe_w_up[layer],
                  moe_w_down[layer], final_norm[None, :], last)
    if depth == 0:
        raise ValueError("depth must be positive")
    return x2.reshape(batch, seq, d)
```

```python
import functools
import math

import jax
import jax.numpy as jnp
from jax import lax
from jax.experimental import pallas as pl
from jax.experimental.pallas import tpu as pltpu

F32 = jnp.float32
BF16 = jnp.bfloat16

HEAD_DIM = 64
WINDOW = 128
SWA_KV_HEADS = 4
SWA_GROUP = 4
N_GROUPS = 4
EXPERTS_PER_GROUP = 8
N_EXPERTS = N_GROUPS * EXPERTS_PER_GROUP
NORM_EPS = 1e-6
SUBLN_EPS = 1e-5
NEG = -1e30
LANES = 128
LOG2E = math.log2(math.e)

PROJ_ROWS = 512
SWA_Q = 128
MOE_ROWS = 256
MOVE_ROWS = 512
DIFF_Q = 512
DIFF_K = 256
VMEM_LIMIT = 56 * 1024 * 1024


def _params(*sem):
    return pltpu.CompilerParams(dimension_semantics=sem, vmem_limit_bytes=VMEM_LIMIT)


def _rms(x, g, eps):
    ms = jnp.mean(x * x, axis=-1, keepdims=True)
    return x * lax.rsqrt(ms + eps) * g


def _swa_proj_kernel(x_ref, g_ref, w_ref, o_ref):
    xn = _rms(x_ref[...], g_ref[...], NORM_EPS).astype(BF16)
    o_ref[...] = jnp.dot(xn, w_ref[...], preferred_element_type=F32).astype(BF16)


def _swa_proj(x2, g, w):
    n, d = x2.shape
    width = w.shape[1]
    return pl.pallas_call(
        _swa_proj_kernel,
        grid=(n // PROJ_ROWS,),
        in_specs=[pl.BlockSpec((PROJ_ROWS, d), lambda i: (i, 0)),
                  pl.BlockSpec((1, d), lambda i: (0, 0)),
                  pl.BlockSpec((d, width), lambda i: (0, 0))],
        out_specs=pl.BlockSpec((PROJ_ROWS, width), lambda i: (i, 0)),
        out_shape=jax.ShapeDtypeStruct((n, width), BF16),
        compiler_params=_params("parallel"),
        name="swa_proj",
    )(x2, g, w)


def _out_proj_kernel(a_ref, w_ref, x_ref, o_ref):
    o_ref[...] = x_ref[...] + jnp.dot(a_ref[...], w_ref[...], preferred_element_type=F32)


def _out_proj(a, w, x2):
    n, d = x2.shape
    return pl.pallas_call(
        _out_proj_kernel,
        grid=(n // PROJ_ROWS,),
        in_specs=[pl.BlockSpec((PROJ_ROWS, a.shape[1]), lambda i: (i, 0)),
                  pl.BlockSpec(w.shape, lambda i: (0, 0)),
                  pl.BlockSpec((PROJ_ROWS, d), lambda i: (i, 0))],
        out_specs=pl.BlockSpec((PROJ_ROWS, d), lambda i: (i, 0)),
        out_shape=jax.ShapeDtypeStruct((n, d), F32),
        compiler_params=_params("parallel"),
        name="out_proj",
    )(a, w, x2)


def _swa_attn_kernel(slope_ref, sink_ref, q_ref, kp_ref, kc_ref, kn_ref, vp_ref, vc_ref, vn_ref, o_ref):
    i = pl.program_id(1)
    last = pl.num_programs(1) - 1
    tq = SWA_Q
    row = lax.broadcasted_iota(jnp.int32, (tq, 3 * tq), 0)
    col = lax.broadcasted_iota(jnp.int32, (tq, 3 * tq), 1)
    dist = jnp.abs(col - tq - row)
    valid = dist <= WINDOW
    valid = valid & jnp.logical_not((col < tq) & (i == 0))
    valid = valid & jnp.logical_not((col >= 2 * tq) & (i == last))
    distf = dist.astype(F32)
    for j in range(SWA_KV_HEADS):
        ks = slice(j * HEAD_DIM, (j + 1) * HEAD_DIM)
        kj = jnp.concatenate([kp_ref[:, ks], kc_ref[:, ks], kn_ref[:, ks]], axis=0)
        vj = jnp.concatenate([vp_ref[:, ks], vc_ref[:, ks], vn_ref[:, ks]], axis=0)
        for g in range(SWA_GROUP):
            h = j * SWA_GROUP + g
            hs = slice(h * HEAD_DIM, (h + 1) * HEAD_DIM)
            s = lax.dot_general(q_ref[:, hs], kj, (((1,), (1,)), ((), ())),
                                preferred_element_type=F32)
            s = jnp.where(valid, s - slope_ref[h] * distf, NEG)
            sink = sink_ref[h]
            m = jnp.maximum(jnp.max(s, axis=-1, keepdims=True), sink)
            p = jnp.exp(s - m)
            denom = jnp.sum(p, axis=-1, keepdims=True) + jnp.exp(sink - m)
            probs = (p / denom).astype(BF16)
            o_ref[:, hs] = jnp.dot(probs, vj, preferred_element_type=F32).astype(BF16)


def _swa_attn(qkv, slopes, sink, batch, seq):
    n_heads = SWA_KV_HEADS * SWA_GROUP
    qw = n_heads * HEAD_DIM
    kvw = SWA_KV_HEADS * HEAD_DIM
    nb = seq // SWA_Q
    qkv3 = qkv.reshape(batch, seq, qw + 2 * kvw)
    kcol = qw // kvw
    vcol = kcol + 1
    smem = pl.BlockSpec(memory_space=pltpu.SMEM)

    def kv_spec(colblk, shift):
        def index(b, i):
            return (b, jnp.clip(i + shift, 0, nb - 1), colblk)
        return pl.BlockSpec((None, SWA_Q, kvw), index)

    out = pl.pallas_call(
        _swa_attn_kernel,
        grid=(batch, nb),
        in_specs=[smem, smem,
                  pl.BlockSpec((None, SWA_Q, qw), lambda b, i: (b, i, 0)),
                  kv_spec(kcol, -1), kv_spec(kcol, 0), kv_spec(kcol, 1),
                  kv_spec(vcol, -1), kv_spec(vcol, 0), kv_spec(vcol, 1)],
        out_specs=pl.BlockSpec((None, SWA_Q, qw), lambda b, i: (b, i, 0)),
        out_shape=jax.ShapeDtypeStruct((batch, seq, qw), BF16),
        compiler_params=_params("parallel", "parallel"),
        name="swa_attn",
    )(slopes, sink, qkv3, qkv3, qkv3, qkv3, qkv3, qkv3, qkv3)
    return out.reshape(batch * seq, qw)


def _router_kernel(x_ref, g_ref, w_ref, b_ref, meta_ref, er_ref, plan_ref, carry_ref):
    t = pl.program_id(0)
    rows = x_ref.shape[0]

    @pl.when(t == 0)
    def _():
        carry_ref[...] = jnp.zeros_like(carry_ref)

    xn = _rms(x_ref[...], g_ref[...], NORM_EPS).astype(BF16)
    logits = jnp.dot(xn, w_ref[...], preferred_element_type=F32) + b_ref[...]
    lane = lax.broadcasted_iota(jnp.int32, (rows, LANES), 1)

    is_group = lane < N_GROUPS
    gl = jnp.where(is_group, logits, NEG)
    gmax = jnp.max(gl, axis=-1, keepdims=True)
    gidx = jnp.min(jnp.where(gl == gmax, lane, LANES), axis=-1, keepdims=True)
    zg = jnp.sum(jnp.where(is_group, jnp.exp(gl - gmax), 0.0), axis=-1, keepdims=True)
    pg = 1.0 / zg

    lo = N_GROUPS + EXPERTS_PER_GROUP * gidx
    in_group = (lane >= lo) & (lane < lo + EXPERTS_PER_GROUP)
    el = jnp.where(in_group, logits, NEG)
    m0 = jnp.max(el, axis=-1, keepdims=True)
    i0 = jnp.min(jnp.where(el == m0, lane, LANES), axis=-1, keepdims=True)
    el1 = jnp.where(lane == i0, NEG, el)
    m1 = jnp.max(el1, axis=-1, keepdims=True)
    i1 = jnp.min(jnp.where(el1 == m1, lane, LANES), axis=-1, keepdims=True)
    r = jnp.exp(m1 - m0)
    w0 = pg / (1.0 + r)
    w1 = pg * r / (1.0 + r)
    e0 = i0 - N_GROUPS
    e1 = i1 - N_GROUPS

    sel0 = lane == e0
    sel1 = lane == e1
    onehot = (sel0 | sel1).astype(BF16)
    rr = lax.broadcasted_iota(jnp.int32, (rows, rows), 0)
    cc = lax.broadcasted_iota(jnp.int32, (rows, rows), 1)
    tri = (cc < rr).astype(BF16)
    before = jnp.dot(tri, onehot, preferred_element_type=F32) + carry_ref[...]
    r0 = jnp.sum(jnp.where(sel0, before, 0.0), axis=-1, keepdims=True)
    r1 = jnp.sum(jnp.where(sel1, before, 0.0), axis=-1, keepdims=True)
    carry_ref[...] = carry_ref[...] + jnp.sum(onehot.astype(F32), axis=0, keepdims=True)

    meta = jnp.where(lane == 0, e0.astype(F32), 0.0)
    meta = jnp.where(lane == 1, e1.astype(F32), meta)
    meta = jnp.where(lane == 2, r0, meta)
    meta = jnp.where(lane == 3, r1, meta)
    meta = jnp.where(lane == 4, w0, meta)
    meta = jnp.where(lane == 5, w1, meta)
    meta_ref[...] = meta
    metat = meta.T
    for f in range(4):
        er_ref[:, f * rows:(f + 1) * rows] = metat[f:f + 1, :].astype(jnp.int32)

    @pl.when(t == pl.num_programs(0) - 1)
    def _plan():
        counts = jnp.broadcast_to(carry_ref[...], (8, LANES))
        padded = jnp.floor((counts + (MOE_ROWS - 1)) * (1.0 / MOE_ROWS)) * MOE_ROWS
        lane8 = lax.broadcasted_iota(jnp.int32, (8, LANES), 1)
        pend = padded
        shift = 1
        while shift < LANES:
            pend = pend + jnp.where(lane8 >= shift, pltpu.roll(pend, shift, 1), 0.0)
            shift *= 2
        plan_ref[:, 0:LANES] = (pend - padded)[0:1].astype(jnp.int32)
        sq_r = lax.broadcasted_iota(jnp.int32, (LANES, LANES), 0)
        sq_c = lax.broadcasted_iota(jnp.int32, (LANES, LANES), 1)
        pend_col = jnp.sum(jnp.where(sq_r == sq_c, jnp.broadcast_to(pend[0:1], (LANES, LANES)), 0.0),
                           axis=1, keepdims=True)
        nbp = plan_ref.shape[1] - LANES
        blk_row = lax.broadcasted_iota(jnp.int32, (LANES, nbp), 0)
        blk_start = (lax.broadcasted_iota(jnp.int32, (LANES, nbp), 1) * MOE_ROWS).astype(F32)
        ended = (pend_col <= blk_start) & (blk_row < N_EXPERTS)
        block_e = jnp.minimum(jnp.sum(ended.astype(F32), axis=0, keepdims=True), N_EXPERTS - 1.0)
        n_used = jnp.sum(jnp.where(lane8[0:1] == N_EXPERTS - 1, pend[0:1], 0.0), axis=1,
                         keepdims=True) * (1.0 / MOE_ROWS)
        blk_lane = lax.broadcasted_iota(jnp.int32, (1, nbp), 1)
        plan_ref[:, LANES:] = jnp.where(blk_lane == nbp - 1, n_used, block_e).astype(jnp.int32)


def _plan_width(n):
    n_blocks = (2 * n + N_EXPERTS * MOE_ROWS) // MOE_ROWS
    return LANES + (n_blocks // LANES + 1) * LANES


def _router(x2, g, w_pad, b_pad):
    n, d = x2.shape
    rows = MOVE_ROWS
    tiles = n // rows
    pw = _plan_width(n)
    return pl.pallas_call(
        _router_kernel,
        grid=(tiles,),
        in_specs=[pl.BlockSpec((rows, d), lambda t: (t, 0)),
                  pl.BlockSpec((1, d), lambda t: (0, 0)),
                  pl.BlockSpec((d, LANES), lambda t: (0, 0)),
                  pl.BlockSpec((1, LANES), lambda t: (0, 0))],
        out_specs=[pl.BlockSpec((rows, LANES), lambda t: (t, 0)),
                   pl.BlockSpec((None, 1, 4 * rows), lambda t: (t, 0, 0)),
                   pl.BlockSpec((1, pw), lambda t: (0, 0))],
        out_shape=[jax.ShapeDtypeStruct((n, LANES), F32),
                   jax.ShapeDtypeStruct((tiles, 1, 4 * rows), jnp.int32),
                   jax.ShapeDtypeStruct((1, pw), jnp.int32)],
        scratch_shapes=[pltpu.VMEM((1, LANES), F32)],
        compiler_params=_params("arbitrary"),
        name="router",
    )(x2, g, w_pad, b_pad)


def _row_copy(src, src_row, dst, dst_row, sem):
    return pltpu.make_async_copy(src.at[pl.ds(src_row, 1)], dst.at[pl.ds(dst_row, 1)], sem)


def _dest_rows(er_ref, plan_ref, r, rows):
    d0 = plan_ref[er_ref[0, 0, r]] + er_ref[0, 0, 2 * rows + r]
    d1 = plan_ref[er_ref[0, 0, rows + r]] + er_ref[0, 0, 3 * rows + r]
    return d0, d1


def _dispatch_kernel(er_ref, plan_ref, x_ref, xs_in_hbm, xs_hbm, sem):
    del xs_in_hbm
    rows = x_ref.shape[0]

    def start(r, c):
        d0, d1 = _dest_rows(er_ref, plan_ref, r, rows)
        _row_copy(x_ref, r, xs_hbm, d0, sem).start()
        _row_copy(x_ref, r, xs_hbm, d1, sem).start()
        return c

    def wait(r, c):
        _row_copy(x_ref, 0, xs_hbm, 0, sem).wait()
        _row_copy(x_ref, 0, xs_hbm, 0, sem).wait()
        return c

    lax.fori_loop(0, rows, start, 0)
    lax.fori_loop(0, rows, wait, 0)


def _dispatch(x2, er, plan, cap):
    n, d = x2.shape
    rows = MOVE_ROWS
    any_spec = pl.BlockSpec(memory_space=pl.ANY)
    smem = pl.BlockSpec(memory_space=pltpu.SMEM)
    return pl.pallas_call(
        _dispatch_kernel,
        grid=(n // rows,),
        in_specs=[pl.BlockSpec((1, 1, 4 * rows), lambda t: (t, 0, 0), memory_space=pltpu.SMEM),
                  smem,
                  pl.BlockSpec((rows, d), lambda t: (t, 0)),
                  any_spec],
        out_specs=any_spec,
        out_shape=jax.ShapeDtypeStruct((cap, d), F32),
        scratch_shapes=[pltpu.SemaphoreType.DMA(())],
        input_output_aliases={3: 0},
        compiler_params=pltpu.CompilerParams(dimension_semantics=("arbitrary",),
                                             has_side_effects=True, vmem_limit_bytes=VMEM_LIMIT),
        name="moe_dispatch",
    )(er, plan, x2, jnp.zeros((cap, d), F32))


def _expert_kernel(plan_ref, xs_ref, g_ref, wg_ref, wu_ref, wd_ref, ys_ref, wgb, wub, wdb):
    b = pl.program_id(0)
    e = plan_ref[LANES + b]
    prev = plan_ref[LANES + jnp.maximum(b - 1, 0)]
    n_used = plan_ref[plan_ref.shape[0] - 1]

    @pl.when((b == 0) | (e != prev))
    def _():
        wgb[...] = wg_ref[...].astype(BF16)
        wub[...] = wu_ref[...].astype(BF16)
        wdb[...] = wd_ref[...].astype(BF16)

    @pl.when(b < n_used)
    def _():
        xn = _rms(xs_ref[...], g_ref[...], NORM_EPS).astype(BF16)
        hg = jnp.dot(xn, wgb[...], preferred_element_type=F32)
        hu = jnp.dot(xn, wub[...], preferred_element_type=F32)
        hid = (hg / (1.0 + jnp.exp(-hg))) * hu
        ys_ref[...] = jnp.dot(hid.astype(BF16), wdb[...], preferred_element_type=F32)

    @pl.when(b >= n_used)
    def _():
        ys_ref[...] = jnp.zeros_like(ys_ref)


def _experts(xs, g, w_gate, w_up, w_down, plan):
    cap, d = xs.shape
    de = w_gate.shape[-1]
    rows = MOE_ROWS
    grid_spec = pltpu.PrefetchScalarGridSpec(
        num_scalar_prefetch=1,
        grid=(cap // rows,),
        in_specs=[pl.BlockSpec((rows, d), lambda b, plan: (b, 0)),
                  pl.BlockSpec((1, d), lambda b, plan: (0, 0)),
                  pl.BlockSpec((None, d, de), lambda b, plan: (plan[LANES + b], 0, 0)),
                  pl.BlockSpec((None, d, de), lambda b, plan: (plan[LANES + b], 0, 0)),
                  pl.BlockSpec((None, de, d), lambda b, plan: (plan[LANES + b], 0, 0))],
        out_specs=pl.BlockSpec((rows, d), lambda b, plan: (b, 0)),
        scratch_shapes=[pltpu.VMEM((d, de), BF16), pltpu.VMEM((d, de), BF16), pltpu.VMEM((de, d), BF16)],
    )
    return pl.pallas_call(
        _expert_kernel,
        grid_spec=grid_spec,
        out_shape=jax.ShapeDtypeStruct((cap, d), F32),
        compiler_params=_params("arbitrary"),
        name="moe_experts",
    )(plan, xs, g, w_gate, w_up, w_down)


def _combine_kernel(er_ref, plan_ref, x_ref, meta_ref, g_ref, ys_hbm, o_ref, ybuf, sem, *, final_norm):
    rows = x_ref.shape[0]

    def start(r, c):
        d0, d1 = _dest_rows(er_ref, plan_ref, r, rows)
        _row_copy(ys_hbm, d0, ybuf.at[0], r, sem).start()
        _row_copy(ys_hbm, d1, ybuf.at[1], r, sem).start()
        return c

    def wait(r, c):
        _row_copy(ys_hbm, 0, ybuf.at[0], 0, sem).wait()
        _row_copy(ys_hbm, 0, ybuf.at[1], 0, sem).wait()
        return c

    lax.fori_loop(0, rows, start, 0)
    lax.fori_loop(0, rows, wait, 0)
    meta = meta_ref[...]
    out = x_ref[...] + (meta[:, 4:5] * ybuf[0] + meta[:, 5:6] * ybuf[1])
    if final_norm:
        out = _rms(out, g_ref[...], NORM_EPS)
    o_ref[...] = out


def _combine(x2, meta, er, plan, ys, g, final_norm):
    n, d = x2.shape
    rows = MOVE_ROWS
    return pl.pallas_call(
        functools.partial(_combine_kernel, final_norm=final_norm),
        grid=(n // rows,),
        in_specs=[pl.BlockSpec((1, 1, 4 * rows), lambda t: (t, 0, 0), memory_space=pltpu.SMEM),
                  pl.BlockSpec(memory_space=pltpu.SMEM),
                  pl.BlockSpec((rows, d), lambda t: (t, 0)),
                  pl.BlockSpec((rows, LANES), lambda t: (t, 0)),
                  pl.BlockSpec((1, d), lambda t: (0, 0)),
                  pl.BlockSpec(memory_space=pl.ANY)],
        out_specs=pl.BlockSpec((rows, d), lambda t: (t, 0)),
        out_shape=jax.ShapeDtypeStruct((n, d), F32),
        scratch_shapes=[pltpu.VMEM((2, rows, d), F32), pltpu.SemaphoreType.DMA(())],
        compiler_params=_params("arbitrary"),
        name="moe_combine",
    )(er, plan, x2, meta, g, ys)


def _moe(x2, g, w_group, b_group, w_router, b_router, w_gate, w_up, w_down, g_final, final_norm):
    n, d = x2.shape
    pad = LANES - N_GROUPS - N_EXPERTS
    w_pad = jnp.concatenate([w_group, w_router, jnp.zeros((d, pad), F32)], axis=1).astype(BF16)
    b_pad = jnp.concatenate([b_group, b_router, jnp.zeros((pad,), F32)])[None, :]
    meta, er, plan = _router(x2, g, w_pad, b_pad)
    plan = plan.reshape(-1)
    cap = 2 * n + N_EXPERTS * MOE_ROWS
    xs = _dispatch(x2, er, plan, cap)
    ys = _experts(xs, g, w_gate, w_up, w_down, plan)
    return _combine(x2, meta, er, plan, ys, g_final, final_norm)


def _diff_proj_kernel(x_ref, g_ref, wqt_ref, wk_ref, wvt_ref, qt_ref, k_ref, vt_ref):
    xn = _rms(x_ref[...], g_ref[...], NORM_EPS).astype(BF16)
    nt = (((1,), (1,)), ((), ()))
    qt_ref[...] = lax.dot_general(wqt_ref[...], xn, nt, preferred_element_type=F32).astype(BF16)
    k_ref[...] = jnp.dot(xn, wk_ref[...], preferred_element_type=F32).astype(BF16)
    vt_ref[...] = lax.dot_general(wvt_ref[...], xn, nt, preferred_element_type=F32).astype(BF16)


def _diff_proj(x3, g, wqt, wk, wvt):
    batch, seq, d = x3.shape
    rows = PROJ_ROWS
    wspec = pl.BlockSpec((d, d), lambda b, i: (0, 0))
    tspec = pl.BlockSpec((None, d, rows), lambda b, i: (b, 0, i))
    return pl.pallas_call(
        _diff_proj_kernel,
        grid=(batch, seq // rows),
        in_specs=[pl.BlockSpec((None, rows, d), lambda b, i: (b, i, 0)),
                  pl.BlockSpec((1, d), lambda b, i: (0, 0)),
                  wspec, wspec, wspec],
        out_specs=[tspec, pl.BlockSpec((None, rows, d), lambda b, i: (b, i, 0)), tspec],
        out_shape=[jax.ShapeDtypeStruct((batch, d, seq), BF16),
                   jax.ShapeDtypeStruct((batch, seq, d), BF16),
                   jax.ShapeDtypeStruct((batch, d, seq), BF16)],
        compiler_params=_params("parallel", "parallel"),
        name="diff_proj",
    )(x3, g, wqt, wk, wvt)


def _diff_attn_kernel(cp_ref, lam_ref, subg_ref, qt_ref, k_ref, vt_ref, o_ref,
                      m_ref, l_ref, acc_ref, qx_ref, s_ref, fix_ref, *, lambda_init):
    h = pl.program_id(1)
    i = pl.program_id(2)
    tq = qt_ref.shape[-1]
    tk = DIFF_K
    per_q = tq // tk
    n_groups = k_ref.shape[0] // tq
    cp = [cp_ref[3 * h + p] for p in range(3)]
    slope2 = cp[0] + cp[1] + cp[2]

    kl = lax.broadcasted_iota(jnp.int32, (tk, LANES), 1)
    kr = lax.broadcasted_iota(jnp.int32, (tk, LANES), 0).astype(F32)
    qr = lax.broadcasted_iota(jnp.int32, (LANES, tq), 0)
    ql = lax.broadcasted_iota(jnp.int32, (LANES, tq), 1)
    di_lo = (ql & 255).astype(F32)
    di_hi = (ql >> 8).astype(F32)
    kaug = jnp.zeros((tk, LANES), F32)
    qaug = jnp.zeros((LANES, tq), F32)
    for p in range(3):
        kaug = jnp.where(kl == p, -cp[p], kaug)
        kaug = jnp.where(kl == 3 + p, -256.0 * cp[p], kaug)
        kaug = jnp.where(kl == 6 + p, kr, kaug)
        qaug = jnp.where(qr == p, di_lo, qaug)
        qaug = jnp.where(qr == 3 + p, di_hi, qaug)
        qaug = jnp.where(qr == 6 + p, cp[p], qaug)
    kaug = kaug.astype(BF16)
    qaug = qaug.astype(BF16)

    zeros = jnp.zeros((HEAD_DIM, tq), BF16)
    q_maps = (jnp.concatenate([qt_ref[0], zeros], axis=0),
              jnp.concatenate([zeros, qt_ref[1]], axis=0))
    for c in range(2):
        qx_ref[0, c] = jnp.concatenate([q_maps[c], qaug], axis=0)
        qx_ref[1, c] = jnp.concatenate([q_maps[c], -qaug], axis=0)
    m_ref[...] = jnp.full_like(m_ref, NEG)
    l_ref[...] = jnp.zeros_like(l_ref)
    acc_ref[...] = jnp.zeros_like(acc_ref)

    for u in range(per_q):
        rel = (lax.broadcasted_iota(jnp.int32, (tk, tq), 0)
               - lax.broadcasted_iota(jnp.int32, (tk, tq), 1)) + u * tk
        fix_ref[u] = (2.0 * slope2) * jnp.maximum(rel, 0).astype(F32)

    n_k = n_groups * per_q

    def qk_stage(j, slot):
        jc = jnp.minimum(j, n_k - 1)
        start = pl.multiple_of(jc * tk, tk)
        kb = jnp.concatenate([k_ref[pl.ds(start, tk), :], kaug], axis=1)
        after = (jc >= (i + 1) * per_q).astype(jnp.int32)
        for c in range(2):
            s_ref[slot, c] = jnp.dot(kb, qx_ref[after, c], preferred_element_type=F32)

    def softmax_stage(j, slot, after, fix):
        start = pl.multiple_of(j * tk, tk)
        vb = vt_ref[:, pl.ds(start, tk)]
        delta = (i * tq - start).astype(F32)
        off = slope2 * (-delta if after else delta)
        for c in range(2):
            s = s_ref[slot, c]
            if fix is not None:
                s = s - fix_ref[fix]
            m_old = m_ref[c]
            m_new = jnp.maximum(m_old, jnp.max(s, axis=0, keepdims=True) - off)
            alpha = jnp.exp2(m_old - m_new)
            p = jnp.exp2(s - (m_new + off))
            l_ref[c] = alpha * l_ref[c] + jnp.sum(p, axis=0, keepdims=True)
            acc_ref[c] = alpha * acc_ref[c] + jnp.dot(vb, p.astype(BF16), preferred_element_type=F32)
            m_ref[c] = m_new

    def pair(j0, after, diagonal):
        qk_stage(j0 + 1, 1)
        softmax_stage(j0, 0, after, 0 if diagonal else None)
        qk_stage(j0 + 2, 0)
        softmax_stage(j0 + 1, 1, after, 1 if diagonal else None)

    def loop_body(after):
        def body(jj, carry):
            pair(jj * per_q, after, False)
            return carry
        return body

    qk_stage(0, 0)
    lax.fori_loop(0, i, loop_body(False), 0)
    pair(i * per_q, False, True)
    lax.fori_loop(i + 1, n_groups, loop_body(True), 0)

    lam = lam_ref[...]
    lam_full = (jnp.exp(jnp.sum(lam[0:1] * lam[1:2], axis=-1, keepdims=True))
                - jnp.exp(jnp.sum(lam[2:3] * lam[3:4], axis=-1, keepdims=True)) + lambda_init)
    o = acc_ref[0] / l_ref[0] - lam_full * (acc_ref[1] / l_ref[1])
    ms = jnp.mean(o * o, axis=0, keepdims=True)
    on = o * lax.rsqrt(ms + SUBLN_EPS) * subg_ref[...] * (1.0 - lambda_init)
    o_ref[...] = on.T.astype(BF16)


def _slope_pieces(slopes):
    s2 = slopes * LOG2E
    c1 = s2.astype(BF16).astype(F32)
    c2 = (s2 - c1).astype(BF16).astype(F32)
    c3 = (s2 - c1 - c2).astype(BF16).astype(F32)
    return jnp.stack([c1, c2, c3], axis=1).reshape(-1)


def _diff_attn(qt, k, vt, slopes, lam, subg, lambda_init):
    batch, heads, _, _, seq = qt.shape
    vdim = vt.shape[2]
    tq = DIFF_Q
    assert tq == 2 * 256 and DIFF_K == 256 and seq % tq == 0
    slopes = _slope_pieces(slopes)
    smem = pl.BlockSpec(memory_space=pltpu.SMEM)
    return pl.pallas_call(
        functools.partial(_diff_attn_kernel, lambda_init=lambda_init),
        grid=(batch, heads, seq // tq),
        in_specs=[smem,
                  pl.BlockSpec(lam.shape, lambda b, h, i: (0, 0)),
                  pl.BlockSpec(subg.shape, lambda b, h, i: (0, 0)),
                  pl.BlockSpec((None, None, 2, HEAD_DIM, tq), lambda b, h, i: (b, h, 0, 0, i)),
                  pl.BlockSpec((None, seq, 2 * HEAD_DIM), lambda b, h, i: (b, 0, h)),
                  pl.BlockSpec((None, None, vdim, seq), lambda b, h, i: (b, h, 0, 0))],
        out_specs=pl.BlockSpec((None, tq, vdim), lambda b, h, i: (b, i, h)),
        out_shape=jax.ShapeDtypeStruct((batch, seq, heads * vdim), BF16),
        scratch_shapes=[pltpu.VMEM((2, 1, tq), F32), pltpu.VMEM((2, 1, tq), F32),
                        pltpu.VMEM((2, vdim, tq), F32),
                        pltpu.VMEM((2, 2, 2 * HEAD_DIM + LANES, tq), BF16),
                        pltpu.VMEM((2, 2, DIFF_K, tq), F32),
                        pltpu.VMEM((tq // DIFF_K, DIFF_K, tq), F32)],
        compiler_params=_params("parallel", "parallel", "arbitrary"),
        name="diff_attn",
    )(slopes, lam, subg, qt, k, vt)


def _alibi_slopes(n_heads):
    h = jnp.arange(1, n_heads + 1, dtype=F32)
    return jnp.exp2(-8.0 * h / n_heads)


def _lambda_init(layer_idx):
    return 0.8 - 0.6 * math.exp(-0.3 * layer_idx)


def kernel(x, norm_mix, norm_ffn, swa_w_qkv, swa_w_o, swa_sink, diff_w_qkv, diff_w_o, diff_lambda,
           diff_subln, moe_w_group, moe_b_group, moe_w_router, moe_b_router, moe_w_gate, moe_w_up,
           moe_w_down, final_norm):
    batch, seq, d = x.shape
    n = batch * seq
    depth = norm_mix.shape[0]
    x2 = x.reshape(n, d)
    q_scale = HEAD_DIM ** -0.5
    for layer in range(depth):
        j = layer // 2
        g_mix = norm_mix[layer][None, :]
        if layer % 2 == 0:
            n_heads = SWA_KV_HEADS * SWA_GROUP
            qw = n_heads * HEAD_DIM
            w = swa_w_qkv[j]
            w = jnp.concatenate([w[:, :qw] * q_scale, w[:, qw:]], axis=1).astype(BF16)
            qkv = _swa_proj(x2, g_mix, w)
            attn = _swa_attn(qkv, _alibi_slopes(n_heads), swa_sink[j].astype(F32), batch, seq)
            x2 = _out_proj(attn, swa_w_o[j].astype(BF16), x2)
        else:
            heads = d // (2 * HEAD_DIM)
            w = diff_w_qkv[j]
            wqt = (w[:, :d] * (q_scale * LOG2E)).T.astype(BF16)
            wk = w[:, d:2 * d].astype(BF16)
            wvt = w[:, 2 * d:].T.astype(BF16)
            qt, k, vt = _diff_proj(x2.reshape(batch, seq, d), g_mix, wqt, wk, wvt)
            qt = qt.reshape(batch, heads, 2, HEAD_DIM, seq)
            vt = vt.reshape(batch, heads, 2 * HEAD_DIM, seq)
            attn = _diff_attn(qt, k, vt, _alibi_slopes(heads), diff_lambda[j].astype(F32),
                              diff_subln[j].astype(F32)[:, None], _lambda_init(layer))
            x2 = _out_proj(attn.reshape(n, d), diff_w_o[j].astype(BF16), x2)
        last = layer == depth - 1
        x2 = _moe(x2, norm_ffn[layer][None, :], moe_w_group[layer], moe_b_group[layer],
                  moe_w_router[layer], moe_b_router[layer], moe_w_gate[layer], moe_w_up[layer],
                  moe_w_down[layer], final_norm[None, :], last)
    return x2.reshape(batch, seq, d)
```

```python
import functools
import math

import jax
import jax.numpy as jnp
from jax import lax
from jax.experimental import pallas as pl
from jax.experimental.pallas import tpu as pltpu

F32 = jnp.float32
BF16 = jnp.bfloat16

HEAD_DIM = 64
WINDOW = 128
SWA_KV_HEADS = 4
SWA_GROUP = 4
N_GROUPS = 4
EXPERTS_PER_GROUP = 8
N_EXPERTS = N_GROUPS * EXPERTS_PER_GROUP
NORM_EPS = 1e-6
SUBLN_EPS = 1e-5
NEG = -1e30
LANES = 128
LOG2E = math.log2(math.e)

PROJ_ROWS = 512
SWA_Q = 128
MOE_ROWS = 256
MOVE_ROWS = 512
ISSUE_UNROLL = 8
DIFF_Q = 512
DIFF_K = 256
DIFF_UNROLL = 4
SUM_ROWS = 16
VMEM_LIMIT = 56 * 1024 * 1024


def _params(*sem):
    return pltpu.CompilerParams(dimension_semantics=sem, vmem_limit_bytes=VMEM_LIMIT)


def _rms(x, g, eps):
    ms = jnp.mean(x * x, axis=-1, keepdims=True)
    return x * lax.rsqrt(ms + eps) * g


def _qkv_proj_kernel(x_ref, g_ref, wqt_ref, wk_ref, wvt_ref, qt_ref, k_ref, vt_ref):
    xn = _rms(x_ref[...], g_ref[...], NORM_EPS).astype(BF16)
    nt = (((1,), (1,)), ((), ()))
    qt_ref[...] = lax.dot_general(wqt_ref[...], xn, nt, preferred_element_type=F32).astype(BF16)
    k_ref[...] = jnp.dot(xn, wk_ref[...], preferred_element_type=F32).astype(BF16)
    vt_ref[...] = lax.dot_general(wvt_ref[...], xn, nt, preferred_element_type=F32).astype(BF16)


def _qkv_proj(x3, g, wqt, wk, wvt, name):
    batch, seq, d = x3.shape
    rows = PROJ_ROWS
    qw, kw, vw = wqt.shape[0], wk.shape[1], wvt.shape[0]

    def full(w):
        return pl.BlockSpec(w.shape, lambda b, i: (0, 0))

    def transposed(width):
        return pl.BlockSpec((None, width, rows), lambda b, i: (b, 0, i))

    return pl.pallas_call(
        _qkv_proj_kernel,
        grid=(batch, seq // rows),
        in_specs=[pl.BlockSpec((None, rows, d), lambda b, i: (b, i, 0)),
                  pl.BlockSpec((1, d), lambda b, i: (0, 0)),
                  full(wqt), full(wk), full(wvt)],
        out_specs=[transposed(qw), pl.BlockSpec((None, rows, kw), lambda b, i: (b, i, 0)), transposed(vw)],
        out_shape=[jax.ShapeDtypeStruct((batch, qw, seq), BF16),
                   jax.ShapeDtypeStruct((batch, seq, kw), BF16),
                   jax.ShapeDtypeStruct((batch, vw, seq), BF16)],
        compiler_params=_params("parallel", "parallel"),
        name=name,
    )(x3, g, wqt, wk, wvt)


def _out_proj_kernel(a_ref, w_ref, x_ref, o_ref):
    o_ref[...] = x_ref[...] + jnp.dot(a_ref[...], w_ref[...], preferred_element_type=F32)


def _out_proj(a, w, x2):
    n, d = x2.shape
    return pl.pallas_call(
        _out_proj_kernel,
        grid=(n // PROJ_ROWS,),
        in_specs=[pl.BlockSpec((PROJ_ROWS, a.shape[1]), lambda i: (i, 0)),
                  pl.BlockSpec(w.shape, lambda i: (0, 0)),
                  pl.BlockSpec((PROJ_ROWS, d), lambda i: (i, 0))],
        out_specs=pl.BlockSpec((PROJ_ROWS, d), lambda i: (i, 0)),
        out_shape=jax.ShapeDtypeStruct((n, d), F32),
        compiler_params=_params("parallel"),
        name="out_proj",
    )(a, w, x2)


def _swa_attn_kernel(slope_ref, sink_ref, qt_ref, kp_ref, kc_ref, kn_ref, vp_ref, vc_ref, vn_ref, o_ref,
                     bias_ref, sinkrow_ref):
    first_step = (pl.program_id(0) == 0) & (pl.program_id(1) == 0)
    i = pl.program_id(1)
    last = pl.num_programs(1) - 1
    tq = SWA_Q
    nk = 3 * tq
    nl = SWA_GROUP * tq

    @pl.when(first_step)
    def _():
        row = lax.broadcasted_iota(jnp.int32, (nk, nl), 0)
        lane = lax.broadcasted_iota(jnp.int32, (nk, nl), 1)
        dist = jnp.abs(row - tq - (lane & (tq - 1)))
        distf = dist.astype(F32)
        lane1 = lax.broadcasted_iota(jnp.int32, (1, nl), 1)
        tq_bits = tq.bit_length() - 1
        for j in range(SWA_KV_HEADS):
            slope = jnp.zeros((nk, nl), F32)
            sink = jnp.zeros((1, nl), F32)
            for g in range(SWA_GROUP):
                h = j * SWA_GROUP + g
                slope = jnp.where((lane >> tq_bits) == g, slope_ref[h], slope)
                sink = jnp.where((lane1 >> tq_bits) == g, sink_ref[h], sink)
            sinkrow_ref[j] = sink
            bias = jnp.where(dist <= WINDOW, -slope * distf, NEG)
            bias_ref[0, j] = bias
            bias_ref[1, j] = jnp.where(row < tq, NEG, bias)
            bias_ref[2, j] = jnp.where(row >= 2 * tq, NEG, bias)

    edge = jnp.where(i == 0, 1, jnp.where(i == last, 2, 0))
    kwin = jnp.concatenate([kp_ref[...], kc_ref[...], kn_ref[...]], axis=0)
    kvw = kwin.shape[1]
    for j in range(SWA_KV_HEADS):
        rows = slice(j * HEAD_DIM, (j + 1) * HEAD_DIM)
        qj = jnp.concatenate([qt_ref[(j * SWA_GROUP + g) * HEAD_DIM:(j * SWA_GROUP + g + 1) * HEAD_DIM, :]
                              for g in range(SWA_GROUP)], axis=1)
        parts = []
        if j > 0:
            parts.append(jnp.zeros((j * HEAD_DIM, nl), BF16))
        parts.append(qj)
        if (j + 1) * HEAD_DIM < kvw:
            parts.append(jnp.zeros((kvw - (j + 1) * HEAD_DIM, nl), BF16))
        qpad = jnp.concatenate(parts, axis=0)
        s = jnp.dot(kwin, qpad, preferred_element_type=F32) + bias_ref[edge, j]
        sink = sinkrow_ref[j]
        m = jnp.maximum(jnp.max(s, axis=0, keepdims=True), sink)
        p = jnp.exp2(s - m)
        denom = jnp.sum(p, axis=0, keepdims=True) + jnp.exp2(sink - m)
        vwin = jnp.concatenate([vp_ref[rows, :], vc_ref[rows, :], vn_ref[rows, :]], axis=1)
        ot = jnp.dot(vwin, p.astype(BF16), preferred_element_type=F32) / denom
        for g in range(SWA_GROUP):
            h = j * SWA_GROUP + g
            o_ref[:, h * HEAD_DIM:(h + 1) * HEAD_DIM] = ot[:, g * tq:(g + 1) * tq].T.astype(BF16)


def _swa_attn(qt, k, vt, slopes, sink):
    batch, qw, seq = qt.shape
    kvw = k.shape[2]
    nb = seq // SWA_Q
    smem = pl.BlockSpec(memory_space=pltpu.SMEM)

    def k_spec(shift):
        return pl.BlockSpec((None, SWA_Q, kvw), lambda b, i: (b, jnp.clip(i + shift, 0, nb - 1), 0))

    def v_spec(shift):
        return pl.BlockSpec((None, kvw, SWA_Q), lambda b, i: (b, 0, jnp.clip(i + shift, 0, nb - 1)))

    out = pl.pallas_call(
        _swa_attn_kernel,
        grid=(batch, nb),
        in_specs=[smem, smem,
                  pl.BlockSpec((None, qw, SWA_Q), lambda b, i: (b, 0, i)),
                  k_spec(-1), k_spec(0), k_spec(1), v_spec(-1), v_spec(0), v_spec(1)],
        out_specs=pl.BlockSpec((None, SWA_Q, qw), lambda b, i: (b, i, 0)),
        out_shape=jax.ShapeDtypeStruct((batch, seq, qw), BF16),
        scratch_shapes=[pltpu.VMEM((3, SWA_KV_HEADS, 3 * SWA_Q, SWA_GROUP * SWA_Q), F32),
                        pltpu.VMEM((SWA_KV_HEADS, 1, SWA_GROUP * SWA_Q), F32)],
        compiler_params=_params("arbitrary", "arbitrary"),
        name="swa_attn",
    )(slopes * LOG2E, sink * LOG2E, qt, k, k, k, vt, vt, vt)
    return out.reshape(batch * seq, qw)


def _router_kernel(x_ref, g_ref, w_ref, b_ref, meta_ref, er_ref, plan_ref, carry_ref):
    t = pl.program_id(0)
    rows = x_ref.shape[0]

    @pl.when(t == 0)
    def _():
        carry_ref[...] = jnp.zeros_like(carry_ref)

    xn = _rms(x_ref[...], g_ref[...], NORM_EPS).astype(BF16)
    logits = jnp.dot(xn, w_ref[...], preferred_element_type=F32) + b_ref[...]
    lane = lax.broadcasted_iota(jnp.int32, (rows, LANES), 1)

    is_group = lane < N_GROUPS
    gl = jnp.where(is_group, logits, NEG)
    gmax = jnp.max(gl, axis=-1, keepdims=True)
    gidx = jnp.min(jnp.where(gl == gmax, lane, LANES), axis=-1, keepdims=True)
    zg = jnp.sum(jnp.where(is_group, jnp.exp(gl - gmax), 0.0), axis=-1, keepdims=True)
    pg = 1.0 / zg

    lo = N_GROUPS + EXPERTS_PER_GROUP * gidx
    in_group = (lane >= lo) & (lane < lo + EXPERTS_PER_GROUP)
    el = jnp.where(in_group, logits, NEG)
    m0 = jnp.max(el, axis=-1, keepdims=True)
    i0 = jnp.min(jnp.where(el == m0, lane, LANES), axis=-1, keepdims=True)
    el1 = jnp.where(lane == i0, NEG, el)
    m1 = jnp.max(el1, axis=-1, keepdims=True)
    i1 = jnp.min(jnp.where(el1 == m1, lane, LANES), axis=-1, keepdims=True)
    r = jnp.exp(m1 - m0)
    w0 = pg / (1.0 + r)
    w1 = pg * r / (1.0 + r)
    e0 = i0 - N_GROUPS
    e1 = i1 - N_GROUPS

    sel0 = lane == e0
    sel1 = lane == e1
    onehot = (sel0 | sel1).astype(BF16)
    rr = lax.broadcasted_iota(jnp.int32, (rows, rows), 0)
    cc = lax.broadcasted_iota(jnp.int32, (rows, rows), 1)
    tri = (cc < rr).astype(BF16)
    before = jnp.dot(tri, onehot, preferred_element_type=F32) + carry_ref[...]
    r0 = jnp.sum(jnp.where(sel0, before, 0.0), axis=-1, keepdims=True)
    r1 = jnp.sum(jnp.where(sel1, before, 0.0), axis=-1, keepdims=True)
    carry_ref[...] = carry_ref[...] + jnp.sum(onehot.astype(F32), axis=0, keepdims=True)

    meta = jnp.where(lane == 0, e0.astype(F32), 0.0)
    meta = jnp.where(lane == 1, e1.astype(F32), meta)
    meta = jnp.where(lane == 2, r0, meta)
    meta = jnp.where(lane == 3, r1, meta)
    meta = jnp.where(lane == 4, w0, meta)
    meta = jnp.where(lane == 5, w1, meta)
    meta_ref[...] = meta
    metat = meta.T
    for f in range(4):
        er_ref[:, f * rows:(f + 1) * rows] = metat[f:f + 1, :].astype(jnp.int32)

    @pl.when(t == pl.num_programs(0) - 1)
    def _plan():
        counts = jnp.broadcast_to(carry_ref[...], (8, LANES))
        padded = jnp.floor((counts + (MOE_ROWS - 1)) * (1.0 / MOE_ROWS)) * MOE_ROWS
        lane8 = lax.broadcasted_iota(jnp.int32, (8, LANES), 1)
        pend = padded
        shift = 1
        while shift < LANES:
            pend = pend + jnp.where(lane8 >= shift, pltpu.roll(pend, shift, 1), 0.0)
            shift *= 2
        plan_ref[:, 0:LANES] = (pend - padded)[0:1].astype(jnp.int32)
        sq_r = lax.broadcasted_iota(jnp.int32, (LANES, LANES), 0)
        sq_c = lax.broadcasted_iota(jnp.int32, (LANES, LANES), 1)
        pend_col = jnp.sum(jnp.where(sq_r == sq_c, jnp.broadcast_to(pend[0:1], (LANES, LANES)), 0.0),
                           axis=1, keepdims=True)
        nbp = plan_ref.shape[1] - LANES
        blk_row = lax.broadcasted_iota(jnp.int32, (LANES, nbp), 0)
        blk_start = (lax.broadcasted_iota(jnp.int32, (LANES, nbp), 1) * MOE_ROWS).astype(F32)
        ended = (pend_col <= blk_start) & (blk_row < N_EXPERTS)
        block_e = jnp.minimum(jnp.sum(ended.astype(F32), axis=0, keepdims=True), N_EXPERTS - 1.0)
        n_used = jnp.sum(jnp.where(lane8[0:1] == N_EXPERTS - 1, pend[0:1], 0.0), axis=1,
                         keepdims=True) * (1.0 / MOE_ROWS)
        blk_lane = lax.broadcasted_iota(jnp.int32, (1, nbp), 1)
        plan_ref[:, LANES:] = jnp.where(blk_lane == nbp - 1, n_used, block_e).astype(jnp.int32)


def _plan_width(n):
    n_blocks = (2 * n + N_EXPERTS * MOE_ROWS) // MOE_ROWS
    return LANES + (n_blocks // LANES + 1) * LANES


def _router(x2, g, w_pad, b_pad):
    n, d = x2.shape
    rows = MOVE_ROWS
    tiles = n // rows
    pw = _plan_width(n)
    return pl.pallas_call(
        _router_kernel,
        grid=(tiles,),
        in_specs=[pl.BlockSpec((rows, d), lambda t: (t, 0)),
                  pl.BlockSpec((1, d), lambda t: (0, 0)),
                  pl.BlockSpec((d, LANES), lambda t: (0, 0)),
                  pl.BlockSpec((1, LANES), lambda t: (0, 0))],
        out_specs=[pl.BlockSpec((rows, LANES), lambda t: (t, 0)),
                   pl.BlockSpec((None, 1, 4 * rows), lambda t: (t, 0, 0)),
                   pl.BlockSpec((1, pw), lambda t: (0, 0))],
        out_shape=[jax.ShapeDtypeStruct((n, LANES), F32),
                   jax.ShapeDtypeStruct((tiles, 1, 4 * rows), jnp.int32),
                   jax.ShapeDtypeStruct((1, pw), jnp.int32)],
        scratch_shapes=[pltpu.VMEM((1, LANES), F32)],
        compiler_params=_params("arbitrary"),
        name="router",
    )(x2, g, w_pad, b_pad)


def _row_copy(src, src_row, dst, dst_row, sem):
    return pltpu.make_async_copy(src.at[pl.ds(src_row, 1)], dst.at[pl.ds(dst_row, 1)], sem)


def _dest_rows(er_ref, plan_ref, r, rows):
    d0 = plan_ref[er_ref[0, 0, r]] + er_ref[0, 0, 2 * rows + r]
    d1 = plan_ref[er_ref[0, 0, rows + r]] + er_ref[0, 0, 3 * rows + r]
    return d0, d1


def _dispatch_kernel(er_ref, plan_ref, x_ref, xs_in_hbm, xs_hbm, sem):
    del xs_in_hbm
    rows = x_ref.shape[0]

    def start(r, c):
        d0, d1 = _dest_rows(er_ref, plan_ref, r, rows)
        _row_copy(x_ref, r, xs_hbm, d0, sem).start()
        _row_copy(x_ref, r, xs_hbm, d1, sem).start()
        return c

    lax.fori_loop(0, rows, start, 0, unroll=ISSUE_UNROLL)
    for _ in range(2):
        pltpu.make_async_copy(x_ref, xs_hbm.at[pl.ds(0, rows)], sem).wait()


def _dispatch(x2, er, plan, cap):
    n, d = x2.shape
    rows = MOVE_ROWS
    any_spec = pl.BlockSpec(memory_space=pl.ANY)
    smem = pl.BlockSpec(memory_space=pltpu.SMEM)
    return pl.pallas_call(
        _dispatch_kernel,
        grid=(n // rows,),
        in_specs=[pl.BlockSpec((1, 1, 4 * rows), lambda t: (t, 0, 0), memory_space=pltpu.SMEM),
                  smem,
                  pl.BlockSpec((rows, d), lambda t: (t, 0)),
                  any_spec],
        out_specs=any_spec,
        out_shape=jax.ShapeDtypeStruct((cap, d), F32),
        scratch_shapes=[pltpu.SemaphoreType.DMA(())],
        input_output_aliases={3: 0},
        compiler_params=pltpu.CompilerParams(dimension_semantics=("arbitrary",),
                                             has_side_effects=True, vmem_limit_bytes=VMEM_LIMIT),
        name="moe_dispatch",
    )(er, plan, x2, jnp.zeros((cap, d), F32))


def _expert_kernel(plan_ref, xs_ref, g_ref, wg_ref, wu_ref, wd_ref, ys_ref, wgb, wub, wdb):
    b = pl.program_id(0)
    e = plan_ref[LANES + b]
    prev = plan_ref[LANES + jnp.maximum(b - 1, 0)]
    n_used = plan_ref[plan_ref.shape[0] - 1]

    @pl.when((b == 0) | (e != prev))
    def _():
        wgb[...] = wg_ref[...].astype(BF16)
        wub[...] = wu_ref[...].astype(BF16)
        wdb[...] = wd_ref[...].astype(BF16)

    @pl.when(b < n_used)
    def _():
        xn = _rms(xs_ref[...], g_ref[...], NORM_EPS).astype(BF16)
        hg = jnp.dot(xn, wgb[...], preferred_element_type=F32)
        hu = jnp.dot(xn, wub[...], preferred_element_type=F32)
        hid = (hg / (1.0 + jnp.exp(-hg))) * hu
        ys_ref[...] = jnp.dot(hid.astype(BF16), wdb[...], preferred_element_type=F32)

    @pl.when(b >= n_used)
    def _():
        ys_ref[...] = jnp.zeros_like(ys_ref)


def _experts(xs, g, w_gate, w_up, w_down, plan):
    cap, d = xs.shape
    de = w_gate.shape[-1]
    rows = MOE_ROWS
    grid_spec = pltpu.PrefetchScalarGridSpec(
        num_scalar_prefetch=1,
        grid=(cap // rows,),
        in_specs=[pl.BlockSpec((rows, d), lambda b, plan: (b, 0)),
                  pl.BlockSpec((1, d), lambda b, plan: (0, 0)),
                  pl.BlockSpec((None, d, de), lambda b, plan: (plan[LANES + b], 0, 0)),
                  pl.BlockSpec((None, d, de), lambda b, plan: (plan[LANES + b], 0, 0)),
                  pl.BlockSpec((None, de, d), lambda b, plan: (plan[LANES + b], 0, 0))],
        out_specs=pl.BlockSpec((rows, d), lambda b, plan: (b, 0)),
        scratch_shapes=[pltpu.VMEM((d, de), BF16), pltpu.VMEM((d, de), BF16), pltpu.VMEM((de, d), BF16)],
    )
    return pl.pallas_call(
        _expert_kernel,
        grid_spec=grid_spec,
        out_shape=jax.ShapeDtypeStruct((cap, d), F32),
        compiler_params=_params("arbitrary"),
        name="moe_experts",
    )(plan, xs, g, w_gate, w_up, w_down)


def _combine_kernel(er_ref, plan_ref, x_ref, meta_ref, g_ref, ys_hbm, o_ref, ybuf, sem, *, final_norm):
    rows = x_ref.shape[0]

    def start(r, c):
        d0, d1 = _dest_rows(er_ref, plan_ref, r, rows)
        _row_copy(ys_hbm, d0, ybuf.at[0], r, sem).start()
        _row_copy(ys_hbm, d1, ybuf.at[1], r, sem).start()
        return c

    lax.fori_loop(0, rows, start, 0, unroll=ISSUE_UNROLL)
    for k in range(2):
        pltpu.make_async_copy(ys_hbm.at[pl.ds(0, rows)], ybuf.at[k], sem).wait()
    meta = meta_ref[...]
    out = x_ref[...] + (meta[:, 4:5] * ybuf[0] + meta[:, 5:6] * ybuf[1])
    if final_norm:
        out = _rms(out, g_ref[...], NORM_EPS)
    o_ref[...] = out


def _combine(x2, meta, er, plan, ys, g, final_norm):
    n, d = x2.shape
    rows = MOVE_ROWS
    return pl.pallas_call(
        functools.partial(_combine_kernel, final_norm=final_norm),
        grid=(n // rows,),
        in_specs=[pl.BlockSpec((1, 1, 4 * rows), lambda t: (t, 0, 0), memory_space=pltpu.SMEM),
                  pl.BlockSpec(memory_space=pltpu.SMEM),
                  pl.BlockSpec((rows, d), lambda t: (t, 0)),
                  pl.BlockSpec((rows, LANES), lambda t: (t, 0)),
                  pl.BlockSpec((1, d), lambda t: (0, 0)),
                  pl.BlockSpec(memory_space=pl.ANY)],
        out_specs=pl.BlockSpec((rows, d), lambda t: (t, 0)),
        out_shape=jax.ShapeDtypeStruct((n, d), F32),
        scratch_shapes=[pltpu.VMEM((2, rows, d), F32), pltpu.SemaphoreType.DMA(())],
        compiler_params=_params("arbitrary"),
        name="moe_combine",
    )(er, plan, x2, meta, g, ys)


def _moe(x2, g, w_group, b_group, w_router, b_router, w_gate, w_up, w_down, g_final, final_norm):
    n, d = x2.shape
    pad = LANES - N_GROUPS - N_EXPERTS
    w_pad = jnp.concatenate([w_group, w_router, jnp.zeros((d, pad), F32)], axis=1).astype(BF16)
    b_pad = jnp.concatenate([b_group, b_router, jnp.zeros((pad,), F32)])[None, :]
    meta, er, plan = _router(x2, g, w_pad, b_pad)
    plan = plan.reshape(-1)
    cap = 2 * n + N_EXPERTS * MOE_ROWS
    xs = _dispatch(x2, er, plan, cap)
    ys = _experts(xs, g, w_gate, w_up, w_down, plan)
    return _combine(x2, meta, er, plan, ys, g_final, final_norm)


def _diff_attn_kernel(cp_ref, lam_ref, subg_ref, qt_ref, k_ref, vt_ref, o_ref,
                      m_ref, acc_ref, qx_ref, s_ref, fix_ref, kaug_ref, *, lambda_init):
    h = pl.program_id(1)
    i = pl.program_id(2)
    tq = qt_ref.shape[-1]
    tk = DIFF_K
    per_q = tq // tk
    n_groups = k_ref.shape[0] // tq
    cp = [cp_ref[3 * h + p] for p in range(3)]
    slope2 = cp[0] + cp[1] + cp[2]

    @pl.when(i == 0)
    def _():
        kl = lax.broadcasted_iota(jnp.int32, (tk, LANES), 1)
        kr = lax.broadcasted_iota(jnp.int32, (tk, LANES), 0).astype(F32)
        qr = lax.broadcasted_iota(jnp.int32, (LANES, tq), 0)
        ql = lax.broadcasted_iota(jnp.int32, (LANES, tq), 1)
        di_lo = (ql & 255).astype(F32)
        di_hi = (ql >> 8).astype(F32)
        kaug = jnp.zeros((tk, LANES), F32)
        qaug = jnp.zeros((LANES, tq), F32)
        for p in range(3):
            kaug = jnp.where(kl == p, -cp[p], kaug)
            kaug = jnp.where(kl == 3 + p, -256.0 * cp[p], kaug)
            kaug = jnp.where(kl == 6 + p, kr, kaug)
            qaug = jnp.where(qr == p, di_lo, qaug)
            qaug = jnp.where(qr == 3 + p, di_hi, qaug)
            qaug = jnp.where(qr == 6 + p, cp[p], qaug)
        kaug_ref[...] = kaug.astype(BF16)
        for c in range(2):
            qx_ref[0, c, 2 * HEAD_DIM:, :] = qaug.astype(BF16)
            qx_ref[1, c, 2 * HEAD_DIM:, :] = (-qaug).astype(BF16)
        for u in range(per_q):
            rel = (lax.broadcasted_iota(jnp.int32, (tk, tq), 0)
                   - lax.broadcasted_iota(jnp.int32, (tk, tq), 1)) + u * tk
            fix_ref[u] = (2.0 * slope2) * jnp.maximum(rel, 0).astype(F32)

    zeros = jnp.zeros((HEAD_DIM, tq), BF16)
    q_maps = (jnp.concatenate([qt_ref[0], zeros], axis=0),
              jnp.concatenate([zeros, qt_ref[1]], axis=0))
    for c in range(2):
        qx_ref[0, c, :2 * HEAD_DIM, :] = q_maps[c]
        qx_ref[1, c, :2 * HEAD_DIM, :] = q_maps[c]
    m_ref[...] = jnp.full_like(m_ref, NEG)
    acc_ref[...] = jnp.zeros_like(acc_ref)
    kaug = kaug_ref[...]

    n_k = n_groups * per_q

    def qk_stage(j, slot):
        jc = jnp.minimum(j, n_k - 1)
        start = pl.multiple_of(jc * tk, tk)
        kb = jnp.concatenate([k_ref[pl.ds(start, tk), :], kaug], axis=1)
        after = (jc >= (i + 1) * per_q).astype(jnp.int32)
        for c in range(2):
            s_ref[slot, c] = jnp.dot(kb, qx_ref[after, c], preferred_element_type=F32)

    ones = jnp.ones((SUM_ROWS, tk), BF16)

    def softmax_stage(j, slot, after, fix):
        start = pl.multiple_of(j * tk, tk)
        vb = jnp.concatenate([vt_ref[:, pl.ds(start, tk)], ones], axis=0)
        delta = (i * tq - start).astype(F32)
        off = slope2 * (-delta if after else delta)
        for c in range(2):
            s = s_ref[slot, c]
            if fix is not None:
                s = s - fix_ref[fix]
            m_old = m_ref[c]
            m_new = jnp.maximum(m_old, jnp.max(s, axis=0, keepdims=True) - off)
            alpha = jnp.exp2(m_old - m_new)
            p = jnp.exp2(s - (m_new + off))
            acc_ref[c] = alpha * acc_ref[c] + jnp.dot(vb, p.astype(BF16), preferred_element_type=F32)
            m_ref[c] = m_new

    def pair(j0, after, diagonal):
        qk_stage(j0 + 1, 1)
        softmax_stage(j0, 0, after, 0 if diagonal else None)
        qk_stage(j0 + 2, 0)
        softmax_stage(j0 + 1, 1, after, 1 if diagonal else None)

    def run_groups(first, count, after):
        def body(unroll, base):
            def step(t, carry):
                for u in range(unroll):
                    pair((base + t * unroll + u) * per_q, after, False)
                return carry
            return step
        n_main = count // DIFF_UNROLL
        lax.fori_loop(0, n_main, body(DIFF_UNROLL, first), 0)
        lax.fori_loop(0, count - n_main * DIFF_UNROLL, body(1, first + n_main * DIFF_UNROLL), 0)

    qk_stage(0, 0)
    run_groups(0, i, False)
    pair(i * per_q, False, True)
    run_groups(i + 1, n_groups - i - 1, True)

    lam = lam_ref[...]
    lam_full = (jnp.exp(jnp.sum(lam[0:1] * lam[1:2], axis=-1, keepdims=True))
                - jnp.exp(jnp.sum(lam[2:3] * lam[3:4], axis=-1, keepdims=True)) + lambda_init)
    vdim = vt_ref.shape[0]
    o = (acc_ref[0, :vdim] / acc_ref[0, vdim:vdim + 1]
         - lam_full * (acc_ref[1, :vdim] / acc_ref[1, vdim:vdim + 1]))
    ms = jnp.mean(o * o, axis=0, keepdims=True)
    on = o * lax.rsqrt(ms + SUBLN_EPS) * subg_ref[...] * (1.0 - lambda_init)
    o_ref[...] = on.T.astype(BF16)


def _slope_pieces(slopes):
    s2 = slopes * LOG2E
    c1 = s2.astype(BF16).astype(F32)
    c2 = (s2 - c1).astype(BF16).astype(F32)
    c3 = (s2 - c1 - c2).astype(BF16).astype(F32)
    return jnp.stack([c1, c2, c3], axis=1).reshape(-1)


def _diff_attn(qt, k, vt, slopes, lam, subg, lambda_init):
    batch, heads, _, _, seq = qt.shape
    vdim = vt.shape[2]
    tq = DIFF_Q
    assert tq == 2 * 256 and DIFF_K == 256 and seq % tq == 0
    slopes = _slope_pieces(slopes)
    smem = pl.BlockSpec(memory_space=pltpu.SMEM)
    return pl.pallas_call(
        functools.partial(_diff_attn_kernel, lambda_init=lambda_init),
        grid=(batch, heads, seq // tq),
        in_specs=[smem,
                  pl.BlockSpec(lam.shape, lambda b, h, i: (0, 0)),
                  pl.BlockSpec(subg.shape, lambda b, h, i: (0, 0)),
                  pl.BlockSpec((None, None, 2, HEAD_DIM, tq), lambda b, h, i: (b, h, 0, 0, i)),
                  pl.BlockSpec((None, seq, 2 * HEAD_DIM), lambda b, h, i: (b, 0, h)),
                  pl.BlockSpec((None, None, vdim, seq), lambda b, h, i: (b, h, 0, 0))],
        out_specs=pl.BlockSpec((None, tq, vdim), lambda b, h, i: (b, i, h)),
        out_shape=jax.ShapeDtypeStruct((batch, seq, heads * vdim), BF16),
        scratch_shapes=[pltpu.VMEM((2, 1, tq), F32),
                        pltpu.VMEM((2, vdim + SUM_ROWS, tq), F32),
                        pltpu.VMEM((2, 2, 2 * HEAD_DIM + LANES, tq), BF16),
                        pltpu.VMEM((2, 2, DIFF_K, tq), F32),
                        pltpu.VMEM((tq // DIFF_K, DIFF_K, tq), F32),
                        pltpu.VMEM((DIFF_K, LANES), BF16)],
        compiler_params=_params("arbitrary", "arbitrary", "arbitrary"),
        name="diff_attn",
    )(slopes, lam, subg, qt, k, vt)


def _alibi_slopes(n_heads):
    h = jnp.arange(1, n_heads + 1, dtype=F32)
    return jnp.exp2(-8.0 * h / n_heads)


def _lambda_init(layer_idx):
    return 0.8 - 0.6 * math.exp(-0.3 * layer_idx)


def kernel(x, norm_mix, norm_ffn, swa_w_qkv, swa_w_o, swa_sink, diff_w_qkv, diff_w_o, diff_lambda,
           diff_subln, moe_w_group, moe_b_group, moe_w_router, moe_b_router, moe_w_gate, moe_w_up,
           moe_w_down, final_norm):
    batch, seq, d = x.shape
    n = batch * seq
    depth = norm_mix.shape[0]
    x2 = x.reshape(n, d)
    q_scale = HEAD_DIM ** -0.5 * LOG2E
    for layer in range(depth):
        j = layer // 2
        g_mix = norm_mix[layer][None, :]
        x3 = x2.reshape(batch, seq, d)
        if layer % 2 == 0:
            n_heads = SWA_KV_HEADS * SWA_GROUP
            qw = n_heads * HEAD_DIM
            kw = SWA_KV_HEADS * HEAD_DIM
            w = swa_w_qkv[j]
            wqt = (w[:, :qw] * q_scale).T.astype(BF16)
            wk = w[:, qw:qw + kw].astype(BF16)
            wvt = w[:, qw + kw:].T.astype(BF16)
            qt, k, vt = _qkv_proj(x3, g_mix, wqt, wk, wvt, "swa_proj")
            attn = _swa_attn(qt, k, vt, _alibi_slopes(n_heads), swa_sink[j].astype(F32))
            x2 = _out_proj(attn, swa_w_o[j].astype(BF16), x2)
        else:
            heads = d // (2 * HEAD_DIM)
            w = diff_w_qkv[j]
            wqt = (w[:, :d] * q_scale).T.astype(BF16)
            wk = w[:, d:2 * d].astype(BF16)
            wvt = w[:, 2 * d:].T.astype(BF16)
            qt, k, vt = _qkv_proj(x3, g_mix, wqt, wk, wvt, "diff_proj")
            qt = qt.reshape(batch, heads, 2, HEAD_DIM, seq)
            vt = vt.reshape(batch, heads, 2 * HEAD_DIM, seq)
            attn = _diff_attn(qt, k, vt, _alibi_slopes(heads), diff_lambda[j].astype(F32),
                              diff_subln[j].astype(F32)[:, None], _lambda_init(layer))
            x2 = _out_proj(attn.reshape(n, d), diff_w_o[j].astype(BF16), x2)
        last = layer == depth - 1
        x2 = _moe(x2, norm_ffn[layer][None, :], moe_w_group[layer], moe_b_group[layer],
                  moe_w_router[layer], moe_b_router[layer], moe_w_gate[layer], moe_w_up[layer],
                  moe_w_down[layer], final_norm[None, :], last)
    return x2.reshape(batch, seq, d)
```

```python
import functools
import math

import jax
import jax.numpy as jnp
from jax import lax
from jax.experimental import pallas as pl
from jax.experimental.pallas import tpu as pltpu

F32 = jnp.float32
BF16 = jnp.bfloat16

HEAD_DIM = 64
WINDOW = 128
SWA_KV_HEADS = 4
SWA_GROUP = 4
N_GROUPS = 4
EXPERTS_PER_GROUP = 8
N_EXPERTS = N_GROUPS * EXPERTS_PER_GROUP
NORM_EPS = 1e-6
SUBLN_EPS = 1e-5
NEG = -1e30
LANES = 128
LOG2E = math.log2(math.e)

PROJ_ROWS = 512
SWA_Q = 128
MOE_ROWS = 256
MOVE_ROWS = 512
ISSUE_UNROLL = 8
DIFF_Q = 512
DIFF_K = 256
DIFF_UNROLL = 4
SKIP_MARGIN = 160.0
SUM_ROWS = 16
VMEM_LIMIT = 56 * 1024 * 1024


def _params(*sem):
    return pltpu.CompilerParams(dimension_semantics=sem, vmem_limit_bytes=VMEM_LIMIT)


def _rms(x, g, eps):
    ms = jnp.mean(x * x, axis=-1, keepdims=True)
    return x * lax.rsqrt(ms + eps) * g


def _qkv_proj_kernel(x_ref, g_ref, wqt_ref, wk_ref, wvt_ref, qt_ref, k_ref, vt_ref):
    xn = _rms(x_ref[...], g_ref[...], NORM_EPS).astype(BF16)
    nt = (((1,), (1,)), ((), ()))
    qt_ref[...] = lax.dot_general(wqt_ref[...], xn, nt, preferred_element_type=F32).astype(BF16)
    k_ref[...] = jnp.dot(xn, wk_ref[...], preferred_element_type=F32).astype(BF16)
    vt_ref[...] = lax.dot_general(wvt_ref[...], xn, nt, preferred_element_type=F32).astype(BF16)


def _qkv_proj(x3, g, wqt, wk, wvt, name):
    batch, seq, d = x3.shape
    rows = PROJ_ROWS
    qw, kw, vw = wqt.shape[0], wk.shape[1], wvt.shape[0]

    def full(w):
        return pl.BlockSpec(w.shape, lambda b, i: (0, 0))

    def transposed(width):
        return pl.BlockSpec((None, width, rows), lambda b, i: (b, 0, i))

    return pl.pallas_call(
        _qkv_proj_kernel,
        grid=(batch, seq // rows),
        in_specs=[pl.BlockSpec((None, rows, d), lambda b, i: (b, i, 0)),
                  pl.BlockSpec((1, d), lambda b, i: (0, 0)),
                  full(wqt), full(wk), full(wvt)],
        out_specs=[transposed(qw), pl.BlockSpec((None, rows, kw), lambda b, i: (b, i, 0)), transposed(vw)],
        out_shape=[jax.ShapeDtypeStruct((batch, qw, seq), BF16),
                   jax.ShapeDtypeStruct((batch, seq, kw), BF16),
                   jax.ShapeDtypeStruct((batch, vw, seq), BF16)],
        compiler_params=_params("parallel", "parallel"),
        name=name,
    )(x3, g, wqt, wk, wvt)


def _out_proj_kernel(a_ref, w_ref, x_ref, o_ref):
    o_ref[...] = x_ref[...] + jnp.dot(a_ref[...], w_ref[...], preferred_element_type=F32)


def _out_proj(a, w, x2):
    n, d = x2.shape
    return pl.pallas_call(
        _out_proj_kernel,
        grid=(n // PROJ_ROWS,),
        in_specs=[pl.BlockSpec((PROJ_ROWS, a.shape[1]), lambda i: (i, 0)),
                  pl.BlockSpec(w.shape, lambda i: (0, 0)),
                  pl.BlockSpec((PROJ_ROWS, d), lambda i: (i, 0))],
        out_specs=pl.BlockSpec((PROJ_ROWS, d), lambda i: (i, 0)),
        out_shape=jax.ShapeDtypeStruct((n, d), F32),
        compiler_params=_params("parallel"),
        name="out_proj",
    )(a, w, x2)


def _swa_attn_kernel(slope_ref, sink_ref, qt_ref, kp_ref, kc_ref, kn_ref, vp_ref, vc_ref, vn_ref, o_ref,
                     bias_ref, sinkrow_ref):
    first_step = (pl.program_id(0) == 0) & (pl.program_id(1) == 0)
    i = pl.program_id(1)
    last = pl.num_programs(1) - 1
    tq = SWA_Q
    nk = 3 * tq
    nl = SWA_GROUP * tq

    @pl.when(first_step)
    def _():
        row = lax.broadcasted_iota(jnp.int32, (nk, nl), 0)
        lane = lax.broadcasted_iota(jnp.int32, (nk, nl), 1)
        dist = jnp.abs(row - tq - (lane & (tq - 1)))
        distf = dist.astype(F32)
        lane1 = lax.broadcasted_iota(jnp.int32, (1, nl), 1)
        tq_bits = tq.bit_length() - 1
        for j in range(SWA_KV_HEADS):
            slope = jnp.zeros((nk, nl), F32)
            sink = jnp.zeros((1, nl), F32)
            for g in range(SWA_GROUP):
                h = j * SWA_GROUP + g
                slope = jnp.where((lane >> tq_bits) == g, slope_ref[h], slope)
                sink = jnp.where((lane1 >> tq_bits) == g, sink_ref[h], sink)
            sinkrow_ref[j] = sink
            bias = jnp.where(dist <= WINDOW, -slope * distf, NEG)
            bias_ref[0, j] = bias
            bias_ref[1, j] = jnp.where(row < tq, NEG, bias)
            bias_ref[2, j] = jnp.where(row >= 2 * tq, NEG, bias)

    edge = jnp.where(i == 0, 1, jnp.where(i == last, 2, 0))
    kwin = jnp.concatenate([kp_ref[...], kc_ref[...], kn_ref[...]], axis=0)
    kvw = kwin.shape[1]
    for j in range(SWA_KV_HEADS):
        rows = slice(j * HEAD_DIM, (j + 1) * HEAD_DIM)
        qj = jnp.concatenate([qt_ref[(j * SWA_GROUP + g) * HEAD_DIM:(j * SWA_GROUP + g + 1) * HEAD_DIM, :]
                              for g in range(SWA_GROUP)], axis=1)
        parts = []
        if j > 0:
            parts.append(jnp.zeros((j * HEAD_DIM, nl), BF16))
        parts.append(qj)
        if (j + 1) * HEAD_DIM < kvw:
            parts.append(jnp.zeros((kvw - (j + 1) * HEAD_DIM, nl), BF16))
        qpad = jnp.concatenate(parts, axis=0)
        s = jnp.dot(kwin, qpad, preferred_element_type=F32) + bias_ref[edge, j]
        sink = sinkrow_ref[j]
        m = jnp.maximum(jnp.max(s, axis=0, keepdims=True), sink)
        p = jnp.exp2(s - m)
        denom = jnp.sum(p, axis=0, keepdims=True) + jnp.exp2(sink - m)
        vwin = jnp.concatenate([vp_ref[rows, :], vc_ref[rows, :], vn_ref[rows, :]], axis=1)
        ot = jnp.dot(vwin, p.astype(BF16), preferred_element_type=F32) / denom
        for g in range(SWA_GROUP):
            h = j * SWA_GROUP + g
            o_ref[:, h * HEAD_DIM:(h + 1) * HEAD_DIM] = ot[:, g * tq:(g + 1) * tq].T.astype(BF16)


def _swa_attn(qt, k, vt, slopes, sink):
    batch, qw, seq = qt.shape
    kvw = k.shape[2]
    nb = seq // SWA_Q
    smem = pl.BlockSpec(memory_space=pltpu.SMEM)

    def k_spec(shift):
        return pl.BlockSpec((None, SWA_Q, kvw), lambda b, i: (b, jnp.clip(i + shift, 0, nb - 1), 0))

    def v_spec(shift):
        return pl.BlockSpec((None, kvw, SWA_Q), lambda b, i: (b, 0, jnp.clip(i + shift, 0, nb - 1)))

    out = pl.pallas_call(
        _swa_attn_kernel,
        grid=(batch, nb),
        in_specs=[smem, smem,
                  pl.BlockSpec((None, qw, SWA_Q), lambda b, i: (b, 0, i)),
                  k_spec(-1), k_spec(0), k_spec(1), v_spec(-1), v_spec(0), v_spec(1)],
        out_specs=pl.BlockSpec((None, SWA_Q, qw), lambda b, i: (b, i, 0)),
        out_shape=jax.ShapeDtypeStruct((batch, seq, qw), BF16),
        scratch_shapes=[pltpu.VMEM((3, SWA_KV_HEADS, 3 * SWA_Q, SWA_GROUP * SWA_Q), F32),
                        pltpu.VMEM((SWA_KV_HEADS, 1, SWA_GROUP * SWA_Q), F32)],
        compiler_params=_params("arbitrary", "arbitrary"),
        name="swa_attn",
    )(slopes * LOG2E, sink * LOG2E, qt, k, k, k, vt, vt, vt)
    return out.reshape(batch * seq, qw)


def _router_kernel(x_ref, g_ref, w_ref, b_ref, meta_ref, er_ref, plan_ref, carry_ref):
    t = pl.program_id(0)
    rows = x_ref.shape[0]

    @pl.when(t == 0)
    def _():
        carry_ref[...] = jnp.zeros_like(carry_ref)

    xn = _rms(x_ref[...], g_ref[...], NORM_EPS).astype(BF16)
    logits = jnp.dot(xn, w_ref[...], preferred_element_type=F32) + b_ref[...]
    lane = lax.broadcasted_iota(jnp.int32, (rows, LANES), 1)

    is_group = lane < N_GROUPS
    gl = jnp.where(is_group, logits, NEG)
    gmax = jnp.max(gl, axis=-1, keepdims=True)
    gidx = jnp.min(jnp.where(gl == gmax, lane, LANES), axis=-1, keepdims=True)
    zg = jnp.sum(jnp.where(is_group, jnp.exp(gl - gmax), 0.0), axis=-1, keepdims=True)
    pg = 1.0 / zg

    lo = N_GROUPS + EXPERTS_PER_GROUP * gidx
    in_group = (lane >= lo) & (lane < lo + EXPERTS_PER_GROUP)
    el = jnp.where(in_group, logits, NEG)
    m0 = jnp.max(el, axis=-1, keepdims=True)
    i0 = jnp.min(jnp.where(el == m0, lane, LANES), axis=-1, keepdims=True)
    el1 = jnp.where(lane == i0, NEG, el)
    m1 = jnp.max(el1, axis=-1, keepdims=True)
    i1 = jnp.min(jnp.where(el1 == m1, lane, LANES), axis=-1, keepdims=True)
    r = jnp.exp(m1 - m0)
    w0 = pg / (1.0 + r)
    w1 = pg * r / (1.0 + r)
    e0 = i0 - N_GROUPS
    e1 = i1 - N_GROUPS

    sel0 = lane == e0
    sel1 = lane == e1
    onehot = (sel0 | sel1).astype(BF16)
    rr = lax.broadcasted_iota(jnp.int32, (rows, rows), 0)
    cc = lax.broadcasted_iota(jnp.int32, (rows, rows), 1)
    tri = (cc < rr).astype(BF16)
    before = jnp.dot(tri, onehot, preferred_element_type=F32) + carry_ref[...]
    r0 = jnp.sum(jnp.where(sel0, before, 0.0), axis=-1, keepdims=True)
    r1 = jnp.sum(jnp.where(sel1, before, 0.0), axis=-1, keepdims=True)
    carry_ref[...] = carry_ref[...] + jnp.sum(onehot.astype(F32), axis=0, keepdims=True)

    meta = jnp.where(lane == 0, e0.astype(F32), 0.0)
    meta = jnp.where(lane == 1, e1.astype(F32), meta)
    meta = jnp.where(lane == 2, r0, meta)
    meta = jnp.where(lane == 3, r1, meta)
    meta = jnp.where(lane == 4, w0, meta)
    meta = jnp.where(lane == 5, w1, meta)
    meta_ref[...] = meta
    metat = meta.T
    for f in range(4):
        er_ref[:, f * rows:(f + 1) * rows] = metat[f:f + 1, :].astype(jnp.int32)

    @pl.when(t == pl.num_programs(0) - 1)
    def _plan():
        counts = jnp.broadcast_to(carry_ref[...], (8, LANES))
        padded = jnp.floor((counts + (MOE_ROWS - 1)) * (1.0 / MOE_ROWS)) * MOE_ROWS
        lane8 = lax.broadcasted_iota(jnp.int32, (8, LANES), 1)
        pend = padded
        shift = 1
        while shift < LANES:
            pend = pend + jnp.where(lane8 >= shift, pltpu.roll(pend, shift, 1), 0.0)
            shift *= 2
        plan_ref[:, 0:LANES] = (pend - padded)[0:1].astype(jnp.int32)
        sq_r = lax.broadcasted_iota(jnp.int32, (LANES, LANES), 0)
        sq_c = lax.broadcasted_iota(jnp.int32, (LANES, LANES), 1)
        pend_col = jnp.sum(jnp.where(sq_r == sq_c, jnp.broadcast_to(pend[0:1], (LANES, LANES)), 0.0),
                           axis=1, keepdims=True)
        nbp = plan_ref.shape[1] - LANES
        blk_row = lax.broadcasted_iota(jnp.int32, (LANES, nbp), 0)
        blk_start = (lax.broadcasted_iota(jnp.int32, (LANES, nbp), 1) * MOE_ROWS).astype(F32)
        ended = (pend_col <= blk_start) & (blk_row < N_EXPERTS)
        block_e = jnp.minimum(jnp.sum(ended.astype(F32), axis=0, keepdims=True), N_EXPERTS - 1.0)
        n_used = jnp.sum(jnp.where(lane8[0:1] == N_EXPERTS - 1, pend[0:1], 0.0), axis=1,
                         keepdims=True) * (1.0 / MOE_ROWS)
        blk_lane = lax.broadcasted_iota(jnp.int32, (1, nbp), 1)
        plan_ref[:, LANES:] = jnp.where(blk_lane == nbp - 1, n_used, block_e).astype(jnp.int32)


def _plan_width(n):
    n_blocks = (2 * n + N_EXPERTS * MOE_ROWS) // MOE_ROWS
    return LANES + (n_blocks // LANES + 1) * LANES


def _router(x2, g, w_pad, b_pad):
    n, d = x2.shape
    rows = MOVE_ROWS
    tiles = n // rows
    pw = _plan_width(n)
    return pl.pallas_call(
        _router_kernel,
        grid=(tiles,),
        in_specs=[pl.BlockSpec((rows, d), lambda t: (t, 0)),
                  pl.BlockSpec((1, d), lambda t: (0, 0)),
                  pl.BlockSpec((d, LANES), lambda t: (0, 0)),
                  pl.BlockSpec((1, LANES), lambda t: (0, 0))],
        out_specs=[pl.BlockSpec((rows, LANES), lambda t: (t, 0)),
                   pl.BlockSpec((None, 1, 4 * rows), lambda t: (t, 0, 0)),
                   pl.BlockSpec((1, pw), lambda t: (0, 0))],
        out_shape=[jax.ShapeDtypeStruct((n, LANES), F32),
                   jax.ShapeDtypeStruct((tiles, 1, 4 * rows), jnp.int32),
                   jax.ShapeDtypeStruct((1, pw), jnp.int32)],
        scratch_shapes=[pltpu.VMEM((1, LANES), F32)],
        compiler_params=_params("arbitrary"),
        name="router",
    )(x2, g, w_pad, b_pad)


def _row_copy(src, src_row, dst, dst_row, sem):
    return pltpu.make_async_copy(src.at[pl.ds(src_row, 1)], dst.at[pl.ds(dst_row, 1)], sem)


def _dest_rows(er_ref, plan_ref, r, rows):
    d0 = plan_ref[er_ref[0, 0, r]] + er_ref[0, 0, 2 * rows + r]
    d1 = plan_ref[er_ref[0, 0, rows + r]] + er_ref[0, 0, 3 * rows + r]
    return d0, d1


def _dispatch_kernel(er_ref, plan_ref, x_ref, xs_in_hbm, xs_hbm, sem):
    del xs_in_hbm
    rows = x_ref.shape[0]

    def start(r, c):
        d0, d1 = _dest_rows(er_ref, plan_ref, r, rows)
        _row_copy(x_ref, r, xs_hbm, d0, sem).start()
        _row_copy(x_ref, r, xs_hbm, d1, sem).start()
        return c

    lax.fori_loop(0, rows, start, 0, unroll=ISSUE_UNROLL)
    for _ in range(2):
        pltpu.make_async_copy(x_ref, xs_hbm.at[pl.ds(0, rows)], sem).wait()


def _dispatch(x2, er, plan, cap):
    n, d = x2.shape
    rows = MOVE_ROWS
    any_spec = pl.BlockSpec(memory_space=pl.ANY)
    smem = pl.BlockSpec(memory_space=pltpu.SMEM)
    return pl.pallas_call(
        _dispatch_kernel,
        grid=(n // rows,),
        in_specs=[pl.BlockSpec((1, 1, 4 * rows), lambda t: (t, 0, 0), memory_space=pltpu.SMEM),
                  smem,
                  pl.BlockSpec((rows, d), lambda t: (t, 0)),
                  any_spec],
        out_specs=any_spec,
        out_shape=jax.ShapeDtypeStruct((cap, d), F32),
        scratch_shapes=[pltpu.SemaphoreType.DMA(())],
        input_output_aliases={3: 0},
        compiler_params=pltpu.CompilerParams(dimension_semantics=("arbitrary",),
                                             has_side_effects=True, vmem_limit_bytes=VMEM_LIMIT),
        name="moe_dispatch",
    )(er, plan, x2, jnp.zeros((cap, d), F32))


def _expert_kernel(plan_ref, xs_ref, g_ref, wg_ref, wu_ref, wd_ref, ys_ref, wgb, wub, wdb):
    b = pl.program_id(0)
    e = plan_ref[LANES + b]
    prev = plan_ref[LANES + jnp.maximum(b - 1, 0)]
    n_used = plan_ref[plan_ref.shape[0] - 1]

    @pl.when((b == 0) | (e != prev))
    def _():
        wgb[...] = wg_ref[...].astype(BF16)
        wub[...] = wu_ref[...].astype(BF16)
        wdb[...] = wd_ref[...].astype(BF16)

    @pl.when(b < n_used)
    def _():
        xn = _rms(xs_ref[...], g_ref[...], NORM_EPS).astype(BF16)
        hg = jnp.dot(xn, wgb[...], preferred_element_type=F32)
        hu = jnp.dot(xn, wub[...], preferred_element_type=F32)
        hid = (hg / (1.0 + jnp.exp(-hg))) * hu
        ys_ref[...] = jnp.dot(hid.astype(BF16), wdb[...], preferred_element_type=F32)

    @pl.when(b >= n_used)
    def _():
        ys_ref[...] = jnp.zeros_like(ys_ref)


def _experts(xs, g, w_gate, w_up, w_down, plan):
    cap, d = xs.shape
    de = w_gate.shape[-1]
    rows = MOE_ROWS
    grid_spec = pltpu.PrefetchScalarGridSpec(
        num_scalar_prefetch=1,
        grid=(cap // rows,),
        in_specs=[pl.BlockSpec((rows, d), lambda b, plan: (b, 0)),
                  pl.BlockSpec((1, d), lambda b, plan: (0, 0)),
                  pl.BlockSpec((None, d, de), lambda b, plan: (plan[LANES + b], 0, 0)),
                  pl.BlockSpec((None, d, de), lambda b, plan: (plan[LANES + b], 0, 0)),
                  pl.BlockSpec((None, de, d), lambda b, plan: (plan[LANES + b], 0, 0))],
        out_specs=pl.BlockSpec((rows, d), lambda b, plan: (b, 0)),
        scratch_shapes=[pltpu.VMEM((d, de), BF16), pltpu.VMEM((d, de), BF16), pltpu.VMEM((de, d), BF16)],
    )
    return pl.pallas_call(
        _expert_kernel,
        grid_spec=grid_spec,
        out_shape=jax.ShapeDtypeStruct((cap, d), F32),
        compiler_params=_params("arbitrary"),
        name="moe_experts",
    )(plan, xs, g, w_gate, w_up, w_down)


def _combine_kernel(er_ref, plan_ref, x_ref, meta_ref, g_ref, ys_hbm, o_ref, ybuf, sem, *, final_norm):
    rows = x_ref.shape[0]

    def start(r, c):
        d0, d1 = _dest_rows(er_ref, plan_ref, r, rows)
        _row_copy(ys_hbm, d0, ybuf.at[0], r, sem).start()
        _row_copy(ys_hbm, d1, ybuf.at[1], r, sem).start()
        return c

    lax.fori_loop(0, rows, start, 0, unroll=ISSUE_UNROLL)
    for k in range(2):
        pltpu.make_async_copy(ys_hbm.at[pl.ds(0, rows)], ybuf.at[k], sem).wait()
    meta = meta_ref[...]
    out = x_ref[...] + (meta[:, 4:5] * ybuf[0] + meta[:, 5:6] * ybuf[1])
    if final_norm:
        out = _rms(out, g_ref[...], NORM_EPS)
    o_ref[...] = out


def _combine(x2, meta, er, plan, ys, g, final_norm):
    n, d = x2.shape
    rows = MOVE_ROWS
    return pl.pallas_call(
        functools.partial(_combine_kernel, final_norm=final_norm),
        grid=(n // rows,),
        in_specs=[pl.BlockSpec((1, 1, 4 * rows), lambda t: (t, 0, 0), memory_space=pltpu.SMEM),
                  pl.BlockSpec(memory_space=pltpu.SMEM),
                  pl.BlockSpec((rows, d), lambda t: (t, 0)),
                  pl.BlockSpec((rows, LANES), lambda t: (t, 0)),
                  pl.BlockSpec((1, d), lambda t: (0, 0)),
                  pl.BlockSpec(memory_space=pl.ANY)],
        out_specs=pl.BlockSpec((rows, d), lambda t: (t, 0)),
        out_shape=jax.ShapeDtypeStruct((n, d), F32),
        scratch_shapes=[pltpu.VMEM((2, rows, d), F32), pltpu.SemaphoreType.DMA(())],
        compiler_params=_params("arbitrary"),
        name="moe_combine",
    )(er, plan, x2, meta, g, ys)


def _moe(x2, g, w_group, b_group, w_router, b_router, w_gate, w_up, w_down, g_final, final_norm):
    n, d = x2.shape
    pad = LANES - N_GROUPS - N_EXPERTS
    w_pad = jnp.concatenate([w_group, w_router, jnp.zeros((d, pad), F32)], axis=1).astype(BF16)
    b_pad = jnp.concatenate([b_group, b_router, jnp.zeros((pad,), F32)])[None, :]
    meta, er, plan = _router(x2, g, w_pad, b_pad)
    plan = plan.reshape(-1)
    cap = 2 * n + N_EXPERTS * MOE_ROWS
    xs = _dispatch(x2, er, plan, cap)
    ys = _experts(xs, g, w_gate, w_up, w_down, plan)
    return _combine(x2, meta, er, plan, ys, g_final, final_norm)


def _diff_attn_kernel(cp_ref, lam_ref, subg_ref, qt_ref, k_ref, vt_ref, o_ref,
                      m_ref, acc_ref, qx_ref, s_ref, fix_ref, kaug_ref, km_ref, *, lambda_init):
    h = pl.program_id(1)
    i = pl.program_id(2)
    tq = qt_ref.shape[-1]
    tk = DIFF_K
    per_q = tq // tk
    n_groups = k_ref.shape[0] // tq
    cp = [cp_ref[3 * h + p] for p in range(3)]
    slope2 = cp[0] + cp[1] + cp[2]

    @pl.when(i == 0)
    def _():
        kl = lax.broadcasted_iota(jnp.int32, (tk, LANES), 1)
        kr = lax.broadcasted_iota(jnp.int32, (tk, LANES), 0).astype(F32)
        qr = lax.broadcasted_iota(jnp.int32, (LANES, tq), 0)
        ql = lax.broadcasted_iota(jnp.int32, (LANES, tq), 1)
        di_lo = (ql & 255).astype(F32)
        di_hi = (ql >> 8).astype(F32)
        kaug = jnp.zeros((tk, LANES), F32)
        qaug = jnp.zeros((LANES, tq), F32)
        for p in range(3):
            kaug = jnp.where(kl == p, -cp[p], kaug)
            kaug = jnp.where(kl == 3 + p, -256.0 * cp[p], kaug)
            kaug = jnp.where(kl == 6 + p, kr, kaug)
            qaug = jnp.where(qr == p, di_lo, qaug)
            qaug = jnp.where(qr == 3 + p, di_hi, qaug)
            qaug = jnp.where(qr == 6 + p, cp[p], qaug)
        kaug_ref[...] = kaug.astype(BF16)
        for c in range(2):
            qx_ref[0, c, 2 * HEAD_DIM:, :] = qaug.astype(BF16)
            qx_ref[1, c, 2 * HEAD_DIM:, :] = (-qaug).astype(BF16)
        for u in range(per_q):
            rel = (lax.broadcasted_iota(jnp.int32, (tk, tq), 0)
                   - lax.broadcasted_iota(jnp.int32, (tk, tq), 1)) + u * tk
            fix_ref[u] = (2.0 * slope2) * jnp.maximum(rel, 0).astype(F32)
        lane_k = lax.broadcasted_iota(jnp.int32, (1, LANES), 1)
        group_norm = []
        for g in range(n_groups):
            kmax = jnp.max(jnp.abs(k_ref[g * tq:(g + 1) * tq, :].astype(F32)), axis=0, keepdims=True)
            sq = kmax * kmax
            group_norm.append([jnp.sqrt(jnp.sum(jnp.where((lane_k >= c * HEAD_DIM) & (lane_k < (c + 1) * HEAD_DIM),
                                                           sq, 0.0), axis=1, keepdims=True))
                               for c in range(2)])
        for c in range(2):
            running = jnp.zeros((1, 1), F32)
            for g in range(n_groups):
                running = jnp.maximum(running, group_norm[g][c])
                km_ref[0, c, g] = jnp.broadcast_to(running, (1, tq))
            running = jnp.zeros((1, 1), F32)
            for g in reversed(range(n_groups)):
                running = jnp.maximum(running, group_norm[g][c])
                km_ref[1, c, g] = jnp.broadcast_to(running, (1, tq))

    zeros = jnp.zeros((HEAD_DIM, tq), BF16)
    q_maps = (jnp.concatenate([qt_ref[0], zeros], axis=0),
              jnp.concatenate([zeros, qt_ref[1]], axis=0))
    for c in range(2):
        qx_ref[0, c, :2 * HEAD_DIM, :] = q_maps[c]
        qx_ref[1, c, :2 * HEAD_DIM, :] = q_maps[c]
    m_ref[...] = jnp.full_like(m_ref, NEG)
    acc_ref[...] = jnp.zeros_like(acc_ref)
    kaug = kaug_ref[...]

    n_k = n_groups * per_q

    def qk_stage(j, slot):
        jc = jnp.clip(j, 0, n_k - 1)
        start = pl.multiple_of(jc * tk, tk)
        kb = jnp.concatenate([k_ref[pl.ds(start, tk), :], kaug], axis=1)
        after = (jc >= (i + 1) * per_q).astype(jnp.int32)
        for c in range(2):
            s_ref[slot, c] = jnp.dot(kb, qx_ref[after, c], preferred_element_type=F32)

    ones = jnp.ones((SUM_ROWS, tk), BF16)

    def softmax_stage(j, slot, after, fix):
        start = pl.multiple_of(j * tk, tk)
        vb = jnp.concatenate([vt_ref[:, pl.ds(start, tk)], ones], axis=0)
        delta = (i * tq - start).astype(F32)
        off = slope2 * (-delta if after else delta)
        for c in range(2):
            s = s_ref[slot, c]
            if fix is not None:
                s = s - fix_ref[fix]
            m_old = m_ref[c]
            m_new = jnp.maximum(m_old, jnp.max(s, axis=0, keepdims=True) - off)
            alpha = jnp.exp2(m_old - m_new)
            p = jnp.exp2(s - (m_new + off))
            acc_ref[c] = alpha * acc_ref[c] + jnp.dot(vb, p.astype(BF16), preferred_element_type=F32)
            m_ref[c] = m_new

    def pair(j0, next_j0, after, diagonal):
        qk_stage(j0 + 1, 1)
        softmax_stage(j0, 0, after, 0 if diagonal else None)
        qk_stage(next_j0, 0)
        softmax_stage(j0 + 1, 1, after, 1 if diagonal else None)

    def run_groups(first, step, count, after):
        @pl.when(count > 0)
        def _():
            qk_stage(first * per_q, 0)

        def body(unroll, base):
            def run(t, carry):
                for u in range(unroll):
                    g = first + step * (base + t * unroll + u)
                    pair(g * per_q, (g + step) * per_q, after, False)
                return carry
            return run
        n_main = count // DIFF_UNROLL
        lax.fori_loop(0, n_main, body(DIFF_UNROLL, 0), 0)
        lax.fori_loop(0, count - n_main * DIFF_UNROLL, body(1, n_main * DIFF_UNROLL), 0)

    qk_stage(i * per_q, 0)
    pair(i * per_q, (i + 1) * per_q, False, True)

    qn = [jnp.sqrt(jnp.sum(jnp.square(qt_ref[c].astype(F32)), axis=0, keepdims=True)) for c in range(2)]

    def groups_needed(side, available):
        cnt = jnp.zeros((1, tq), F32)
        for t in range(n_groups - 1):
            g = jnp.clip(i - 1 - t if side == 0 else i + 1 + t, 0, n_groups - 1)
            penalty = slope2 * float(tq * t + 1)
            need = None
            for c in range(2):
                reach = (qn[c] * km_ref[side, c, g] - penalty) >= (m_ref[c] - SKIP_MARGIN)
                need = reach if need is None else (need | reach)
            cnt = cnt + jnp.where(t < available, jnp.where(need, 1.0, 0.0), 0.0)
        return jnp.max(cnt).astype(jnp.int32)

    n_before = groups_needed(0, i)
    n_after = groups_needed(1, n_groups - 1 - i)
    run_groups(i - 1, -1, n_before, False)
    run_groups(i + 1, 1, n_after, True)

    lam = lam_ref[...]
    lam_full = (jnp.exp(jnp.sum(lam[0:1] * lam[1:2], axis=-1, keepdims=True))
                - jnp.exp(jnp.sum(lam[2:3] * lam[3:4], axis=-1, keepdims=True)) + lambda_init)
    vdim = vt_ref.shape[0]
    o = (acc_ref[0, :vdim] / acc_ref[0, vdim:vdim + 1]
         - lam_full * (acc_ref[1, :vdim] / acc_ref[1, vdim:vdim + 1]))
    ms = jnp.mean(o * o, axis=0, keepdims=True)
    on = o * lax.rsqrt(ms + SUBLN_EPS) * subg_ref[...] * (1.0 - lambda_init)
    o_ref[...] = on.T.astype(BF16)


def _slope_pieces(slopes):
    s2 = slopes * LOG2E
    c1 = s2.astype(BF16).astype(F32)
    c2 = (s2 - c1).astype(BF16).astype(F32)
    c3 = (s2 - c1 - c2).astype(BF16).astype(F32)
    return jnp.stack([c1, c2, c3], axis=1).reshape(-1)


def _diff_attn(qt, k, vt, slopes, lam, subg, lambda_init):
    batch, heads, _, _, seq = qt.shape
    vdim = vt.shape[2]
    tq = DIFF_Q
    assert tq == 2 * 256 and DIFF_K == 256 and seq % tq == 0
    slopes = _slope_pieces(slopes)
    smem = pl.BlockSpec(memory_space=pltpu.SMEM)
    return pl.pallas_call(
        functools.partial(_diff_attn_kernel, lambda_init=lambda_init),
        grid=(batch, heads, seq // tq),
        in_specs=[smem,
                  pl.BlockSpec(lam.shape, lambda b, h, i: (0, 0)),
                  pl.BlockSpec(subg.shape, lambda b, h, i: (0, 0)),
                  pl.BlockSpec((None, None, 2, HEAD_DIM, tq), lambda b, h, i: (b, h, 0, 0, i)),
                  pl.BlockSpec((None, seq, 2 * HEAD_DIM), lambda b, h, i: (b, 0, h)),
                  pl.BlockSpec((None, None, vdim, seq), lambda b, h, i: (b, h, 0, 0))],
        out_specs=pl.BlockSpec((None, tq, vdim), lambda b, h, i: (b, i, h)),
        out_shape=jax.ShapeDtypeStruct((batch, seq, heads * vdim), BF16),
        scratch_shapes=[pltpu.VMEM((2, 1, tq), F32),
                        pltpu.VMEM((2, vdim + SUM_ROWS, tq), F32),
                        pltpu.VMEM((2, 2, 2 * HEAD_DIM + LANES, tq), BF16),
                        pltpu.VMEM((2, 2, DIFF_K, tq), F32),
                        pltpu.VMEM((tq // DIFF_K, DIFF_K, tq), F32),
                        pltpu.VMEM((DIFF_K, LANES), BF16),
                        pltpu.VMEM((2, 2, seq // tq, 1, tq), F32)],
        compiler_params=_params("arbitrary", "arbitrary", "arbitrary"),
        name="diff_attn",
    )(slopes, lam, subg, qt, k, vt)


def _alibi_slopes(n_heads):
    h = jnp.arange(1, n_heads + 1, dtype=F32)
    return jnp.exp2(-8.0 * h / n_heads)


def _lambda_init(layer_idx):
    return 0.8 - 0.6 * math.exp(-0.3 * layer_idx)


def kernel(x, norm_mix, norm_ffn, swa_w_qkv, swa_w_o, swa_sink, diff_w_qkv, diff_w_o, diff_lambda,
           diff_subln, moe_w_group, moe_b_group, moe_w_router, moe_b_router, moe_w_gate, moe_w_up,
           moe_w_down, final_norm):
    batch, seq, d = x.shape
    n = batch * seq
    depth = norm_mix.shape[0]
    x2 = x.reshape(n, d)
    q_scale = HEAD_DIM ** -0.5 * LOG2E
    for layer in range(depth):
        j = layer // 2
        g_mix = norm_mix[layer][None, :]
        x3 = x2.reshape(batch, seq, d)
        if layer % 2 == 0:
            n_heads = SWA_KV_HEADS * SWA_GROUP
            qw = n_heads * HEAD_DIM
            kw = SWA_KV_HEADS * HEAD_DIM
            w = swa_w_qkv[j]
            wqt = (w[:, :qw] * q_scale).T.astype(BF16)
            wk = w[:, qw:qw + kw].astype(BF16)
            wvt = w[:, qw + kw:].T.astype(BF16)
            qt, k, vt = _qkv_proj(x3, g_mix, wqt, wk, wvt, "swa_proj")
            attn = _swa_attn(qt, k, vt, _alibi_slopes(n_heads), swa_sink[j].astype(F32))
            x2 = _out_proj(attn, swa_w_o[j].astype(BF16), x2)
        else:
            heads = d // (2 * HEAD_DIM)
            w = diff_w_qkv[j]
            wqt = (w[:, :d] * q_scale).T.astype(BF16)
            wk = w[:, d:2 * d].astype(BF16)
            wvt = w[:, 2 * d:].T.astype(BF16)
            qt, k, vt = _qkv_proj(x3, g_mix, wqt, wk, wvt, "diff_proj")
            qt = qt.reshape(batch, heads, 2, HEAD_DIM, seq)
            vt = vt.reshape(batch, heads, 2 * HEAD_DIM, seq)
            attn = _diff_attn(qt, k, vt, _alibi_slopes(heads), diff_lambda[j].astype(F32),
                              diff_subln[j].astype(F32)[:, None], _lambda_init(layer))
            x2 = _out_proj(attn.reshape(n, d), diff_w_o[j].astype(BF16), x2)
        last = layer == depth - 1
        x2 = _moe(x2, norm_ffn[layer][None, :], moe_w_group[layer], moe_b_group[layer],
                  moe_w_router[layer], moe_b_router[layer], moe_w_gate[layer], moe_w_up[layer],
                  moe_w_down[layer], final_norm[None, :], last)
    return x2.reshape(batch, seq, d)
```

```python
import functools
import math

import jax
import jax.numpy as jnp
from jax import lax
from jax.experimental import pallas as pl
from jax.experimental.pallas import tpu as pltpu

F32 = jnp.float32
BF16 = jnp.bfloat16

HEAD_DIM = 64
WINDOW = 128
SWA_KV_HEADS = 4
SWA_GROUP = 4
N_GROUPS = 4
EXPERTS_PER_GROUP = 8
N_EXPERTS = N_GROUPS * EXPERTS_PER_GROUP
NORM_EPS = 1e-6
SUBLN_EPS = 1e-5
NEG = -1e30
LANES = 128
LOG2E = math.log2(math.e)

PROJ_ROWS = 512
SWA_Q = 128
MOE_ROWS = 256
MOVE_ROWS = 512
ISSUE_UNROLL = 8
DIFF_Q = 512
DIFF_K = 256
DIFF_UNROLL = 4
SKIP_MARGIN = 160.0
FAST_MARGIN = 64.0
SUM_ROWS = 16
VMEM_LIMIT = 56 * 1024 * 1024


def _params(*sem):
    return pltpu.CompilerParams(dimension_semantics=sem, vmem_limit_bytes=VMEM_LIMIT)


def _rms(x, g, eps):
    ms = jnp.mean(x * x, axis=-1, keepdims=True)
    return x * lax.rsqrt(ms + eps) * g


def _qkv_proj_kernel(x_ref, g_ref, wq_ref, wk_ref, wv_ref, qt_ref, k_ref, vt_ref, wqt_ref, wvt_ref):
    @pl.when((pl.program_id(0) == 0) & (pl.program_id(1) == 0))
    def _():
        wqt_ref[...] = wq_ref[...].T
        wvt_ref[...] = wv_ref[...].T

    xn = _rms(x_ref[...], g_ref[...], NORM_EPS).astype(BF16)
    nt = (((1,), (1,)), ((), ()))
    qt_ref[...] = lax.dot_general(wqt_ref[...], xn, nt, preferred_element_type=F32).astype(BF16)
    k_ref[...] = jnp.dot(xn, wk_ref[...], preferred_element_type=F32).astype(BF16)
    vt_ref[...] = lax.dot_general(wvt_ref[...], xn, nt, preferred_element_type=F32).astype(BF16)


def _qkv_proj(x3, g, wq, wk, wv, name):
    batch, seq, d = x3.shape
    rows = PROJ_ROWS
    qw, kw, vw = wq.shape[1], wk.shape[1], wv.shape[1]

    def full(w):
        return pl.BlockSpec(w.shape, lambda b, i: (0, 0))

    def transposed(width):
        return pl.BlockSpec((None, width, rows), lambda b, i: (b, 0, i))

    return pl.pallas_call(
        _qkv_proj_kernel,
        grid=(batch, seq // rows),
        in_specs=[pl.BlockSpec((None, rows, d), lambda b, i: (b, i, 0)),
                  pl.BlockSpec((1, d), lambda b, i: (0, 0)),
                  full(wq), full(wk), full(wv)],
        out_specs=[transposed(qw), pl.BlockSpec((None, rows, kw), lambda b, i: (b, i, 0)), transposed(vw)],
        out_shape=[jax.ShapeDtypeStruct((batch, qw, seq), BF16),
                   jax.ShapeDtypeStruct((batch, seq, kw), BF16),
                   jax.ShapeDtypeStruct((batch, vw, seq), BF16)],
        scratch_shapes=[pltpu.VMEM((qw, d), BF16), pltpu.VMEM((vw, d), BF16)],
        compiler_params=_params("arbitrary", "arbitrary"),
        name=name,
    )(x3, g, wq, wk, wv)


def _out_proj_kernel(a_ref, w_ref, x_ref, o_ref):
    o_ref[...] = x_ref[...] + jnp.dot(a_ref[...], w_ref[...], preferred_element_type=F32)


def _out_proj(a, w, x2):
    n, d = x2.shape
    return pl.pallas_call(
        _out_proj_kernel,
        grid=(n // PROJ_ROWS,),
        in_specs=[pl.BlockSpec((PROJ_ROWS, a.shape[1]), lambda i: (i, 0)),
                  pl.BlockSpec(w.shape, lambda i: (0, 0)),
                  pl.BlockSpec((PROJ_ROWS, d), lambda i: (i, 0))],
        out_specs=pl.BlockSpec((PROJ_ROWS, d), lambda i: (i, 0)),
        out_shape=jax.ShapeDtypeStruct((n, d), F32),
        compiler_params=_params("parallel"),
        name="out_proj",
    )(a, w, x2)


def _swa_attn_kernel(slope_ref, sink_ref, qt_ref, kp_ref, kc_ref, kn_ref, vp_ref, vc_ref, vn_ref, o_ref,
                     bias_ref, sinkrow_ref):
    first_step = (pl.program_id(0) == 0) & (pl.program_id(1) == 0)
    i = pl.program_id(1)
    last = pl.num_programs(1) - 1
    tq = SWA_Q
    nk = 3 * tq
    nl = SWA_GROUP * tq

    @pl.when(first_step)
    def _():
        row = lax.broadcasted_iota(jnp.int32, (nk, nl), 0)
        lane = lax.broadcasted_iota(jnp.int32, (nk, nl), 1)
        dist = jnp.abs(row - tq - (lane & (tq - 1)))
        distf = dist.astype(F32)
        lane1 = lax.broadcasted_iota(jnp.int32, (1, nl), 1)
        tq_bits = tq.bit_length() - 1
        for j in range(SWA_KV_HEADS):
            slope = jnp.zeros((nk, nl), F32)
            sink = jnp.zeros((1, nl), F32)
            for g in range(SWA_GROUP):
                h = j * SWA_GROUP + g
                slope = jnp.where((lane >> tq_bits) == g, slope_ref[h], slope)
                sink = jnp.where((lane1 >> tq_bits) == g, sink_ref[h], sink)
            sinkrow_ref[j] = sink
            bias = jnp.where(dist <= WINDOW, -slope * distf, NEG)
            bias_ref[0, j] = bias
            bias_ref[1, j] = jnp.where(row < tq, NEG, bias)
            bias_ref[2, j] = jnp.where(row >= 2 * tq, NEG, bias)

    edge = jnp.where(i == 0, 1, jnp.where(i == last, 2, 0))
    kwin = jnp.concatenate([kp_ref[...], kc_ref[...], kn_ref[...]], axis=0)
    kvw = kwin.shape[1]
    for j in range(SWA_KV_HEADS):
        rows = slice(j * HEAD_DIM, (j + 1) * HEAD_DIM)
        qj = jnp.concatenate([qt_ref[(j * SWA_GROUP + g) * HEAD_DIM:(j * SWA_GROUP + g + 1) * HEAD_DIM, :]
                              for g in range(SWA_GROUP)], axis=1)
        parts = []
        if j > 0:
            parts.append(jnp.zeros((j * HEAD_DIM, nl), BF16))
        parts.append(qj)
        if (j + 1) * HEAD_DIM < kvw:
            parts.append(jnp.zeros((kvw - (j + 1) * HEAD_DIM, nl), BF16))
        qpad = jnp.concatenate(parts, axis=0)
        s = jnp.dot(kwin, qpad, preferred_element_type=F32) + bias_ref[edge, j]
        sink = sinkrow_ref[j]
        m = jnp.maximum(jnp.max(s, axis=0, keepdims=True), sink)
        p = jnp.exp2(s - m)
        denom = jnp.sum(p, axis=0, keepdims=True) + jnp.exp2(sink - m)
        vwin = jnp.concatenate([vp_ref[rows, :], vc_ref[rows, :], vn_ref[rows, :]], axis=1)
        ot = jnp.dot(vwin, p.astype(BF16), preferred_element_type=F32) / denom
        for g in range(SWA_GROUP):
            h = j * SWA_GROUP + g
            o_ref[:, h * HEAD_DIM:(h + 1) * HEAD_DIM] = ot[:, g * tq:(g + 1) * tq].T.astype(BF16)


def _swa_attn(qt, k, vt, slopes, sink):
    batch, qw, seq = qt.shape
    kvw = k.shape[2]
    nb = seq // SWA_Q
    smem = pl.BlockSpec(memory_space=pltpu.SMEM)

    def k_spec(shift):
        return pl.BlockSpec((None, SWA_Q, kvw), lambda b, i: (b, jnp.clip(i + shift, 0, nb - 1), 0))

    def v_spec(shift):
        return pl.BlockSpec((None, kvw, SWA_Q), lambda b, i: (b, 0, jnp.clip(i + shift, 0, nb - 1)))

    out = pl.pallas_call(
        _swa_attn_kernel,
        grid=(batch, nb),
        in_specs=[smem, smem,
                  pl.BlockSpec((None, qw, SWA_Q), lambda b, i: (b, 0, i)),
                  k_spec(-1), k_spec(0), k_spec(1), v_spec(-1), v_spec(0), v_spec(1)],
        out_specs=pl.BlockSpec((None, SWA_Q, qw), lambda b, i: (b, i, 0)),
        out_shape=jax.ShapeDtypeStruct((batch, seq, qw), BF16),
        scratch_shapes=[pltpu.VMEM((3, SWA_KV_HEADS, 3 * SWA_Q, SWA_GROUP * SWA_Q), F32),
                        pltpu.VMEM((SWA_KV_HEADS, 1, SWA_GROUP * SWA_Q), F32)],
        compiler_params=_params("arbitrary", "arbitrary"),
        name="swa_attn",
    )(slopes * LOG2E, sink * LOG2E, qt, k, k, k, vt, vt, vt)
    return out.reshape(batch * seq, qw)


def _router_kernel(x_ref, g_ref, w_ref, b_ref, meta_ref, er_ref, plan_ref, carry_ref):
    t = pl.program_id(0)
    rows = x_ref.shape[0]

    @pl.when(t == 0)
    def _():
        carry_ref[...] = jnp.zeros_like(carry_ref)

    xn = _rms(x_ref[...], g_ref[...], NORM_EPS).astype(BF16)
    logits = jnp.dot(xn, w_ref[...], preferred_element_type=F32) + b_ref[...]
    lane = lax.broadcasted_iota(jnp.int32, (rows, LANES), 1)

    is_group = lane < N_GROUPS
    gl = jnp.where(is_group, logits, NEG)
    gmax = jnp.max(gl, axis=-1, keepdims=True)
    gidx = jnp.min(jnp.where(gl == gmax, lane, LANES), axis=-1, keepdims=True)
    zg = jnp.sum(jnp.where(is_group, jnp.exp(gl - gmax), 0.0), axis=-1, keepdims=True)
    pg = 1.0 / zg

    lo = N_GROUPS + EXPERTS_PER_GROUP * gidx
    in_group = (lane >= lo) & (lane < lo + EXPERTS_PER_GROUP)
    el = jnp.where(in_group, logits, NEG)
    m0 = jnp.max(el, axis=-1, keepdims=True)
    i0 = jnp.min(jnp.where(el == m0, lane, LANES), axis=-1, keepdims=True)
    el1 = jnp.where(lane == i0, NEG, el)
    m1 = jnp.max(el1, axis=-1, keepdims=True)
    i1 = jnp.min(jnp.where(el1 == m1, lane, LANES), axis=-1, keepdims=True)
    r = jnp.exp(m1 - m0)
    w0 = pg / (1.0 + r)
    w1 = pg * r / (1.0 + r)
    e0 = i0 - N_GROUPS
    e1 = i1 - N_GROUPS

    sel0 = lane == e0
    sel1 = lane == e1
    onehot = (sel0 | sel1).astype(BF16)
    rr = lax.broadcasted_iota(jnp.int32, (rows, rows), 0)
    cc = lax.broadcasted_iota(jnp.int32, (rows, rows), 1)
    tri = (cc < rr).astype(BF16)
    before = jnp.dot(tri, onehot, preferred_element_type=F32) + carry_ref[...]
    r0 = jnp.sum(jnp.where(sel0, before, 0.0), axis=-1, keepdims=True)
    r1 = jnp.sum(jnp.where(sel1, before, 0.0), axis=-1, keepdims=True)
    carry_ref[...] = carry_ref[...] + jnp.sum(onehot.astype(F32), axis=0, keepdims=True)

    meta = jnp.where(lane == 0, e0.astype(F32), 0.0)
    meta = jnp.where(lane == 1, e1.astype(F32), meta)
    meta = jnp.where(lane == 2, r0, meta)
    meta = jnp.where(lane == 3, r1, meta)
    meta = jnp.where(lane == 4, w0, meta)
    meta = jnp.where(lane == 5, w1, meta)
    meta_ref[...] = meta
    metat = meta.T
    for f in range(4):
        er_ref[:, f * rows:(f + 1) * rows] = metat[f:f + 1, :].astype(jnp.int32)

    @pl.when(t == pl.num_programs(0) - 1)
    def _plan():
        counts = jnp.broadcast_to(carry_ref[...], (8, LANES))
        padded = jnp.floor((counts + (MOE_ROWS - 1)) * (1.0 / MOE_ROWS)) * MOE_ROWS
        lane8 = lax.broadcasted_iota(jnp.int32, (8, LANES), 1)
        pend = padded
        shift = 1
        while shift < LANES:
            pend = pend + jnp.where(lane8 >= shift, pltpu.roll(pend, shift, 1), 0.0)
            shift *= 2
        starts_counts = jnp.where(lane8 < N_EXPERTS, pend - padded, pltpu.roll(counts, N_EXPERTS, 1))
        plan_ref[:, 0:LANES] = starts_counts[0:1].astype(jnp.int32)
        sq_r = lax.broadcasted_iota(jnp.int32, (LANES, LANES), 0)
        sq_c = lax.broadcasted_iota(jnp.int32, (LANES, LANES), 1)
        pend_col = jnp.sum(jnp.where(sq_r == sq_c, jnp.broadcast_to(pend[0:1], (LANES, LANES)), 0.0),
                           axis=1, keepdims=True)
        nbp = plan_ref.shape[1] - LANES
        blk_row = lax.broadcasted_iota(jnp.int32, (LANES, nbp), 0)
        blk_start = (lax.broadcasted_iota(jnp.int32, (LANES, nbp), 1) * MOE_ROWS).astype(F32)
        ended = (pend_col <= blk_start) & (blk_row < N_EXPERTS)
        block_e = jnp.minimum(jnp.sum(ended.astype(F32), axis=0, keepdims=True), N_EXPERTS - 1.0)
        n_used = jnp.sum(jnp.where(lane8[0:1] == N_EXPERTS - 1, pend[0:1], 0.0), axis=1,
                         keepdims=True) * (1.0 / MOE_ROWS)
        blk_lane = lax.broadcasted_iota(jnp.int32, (1, nbp), 1)
        plan_ref[:, LANES:] = jnp.where(blk_lane == nbp - 1, n_used, block_e).astype(jnp.int32)


def _plan_width(n):
    n_blocks = (2 * n + N_EXPERTS * MOE_ROWS) // MOE_ROWS
    return LANES + (n_blocks // LANES + 1) * LANES


def _router(x2, g, w_pad, b_pad):
    n, d = x2.shape
    rows = MOVE_ROWS
    tiles = n // rows
    pw = _plan_width(n)
    return pl.pallas_call(
        _router_kernel,
        grid=(tiles,),
        in_specs=[pl.BlockSpec((rows, d), lambda t: (t, 0)),
                  pl.BlockSpec((1, d), lambda t: (0, 0)),
                  pl.BlockSpec((d, LANES), lambda t: (0, 0)),
                  pl.BlockSpec((1, LANES), lambda t: (0, 0))],
        out_specs=[pl.BlockSpec((rows, LANES), lambda t: (t, 0)),
                   pl.BlockSpec((None, 1, 4 * rows), lambda t: (t, 0, 0)),
                   pl.BlockSpec((1, pw), lambda t: (0, 0))],
        out_shape=[jax.ShapeDtypeStruct((n, LANES), F32),
                   jax.ShapeDtypeStruct((tiles, 1, 4 * rows), jnp.int32),
                   jax.ShapeDtypeStruct((1, pw), jnp.int32)],
        scratch_shapes=[pltpu.VMEM((1, LANES), F32)],
        compiler_params=_params("arbitrary"),
        name="router",
    )(x2, g, w_pad, b_pad)


def _row_copy(src, src_row, dst, dst_row, sem):
    return pltpu.make_async_copy(src.at[pl.ds(src_row, 1)], dst.at[pl.ds(dst_row, 1)], sem)


def _dest_rows(er_ref, plan_ref, r, rows):
    d0 = plan_ref[er_ref[0, 0, r]] + er_ref[0, 0, 2 * rows + r]
    d1 = plan_ref[er_ref[0, 0, rows + r]] + er_ref[0, 0, 3 * rows + r]
    return d0, d1


def _dispatch_kernel(er_ref, plan_ref, x_ref, xs_hbm, zero_ref, sem, zsem):
    rows = x_ref.shape[0]

    @pl.when(pl.program_id(0) == 0)
    def _():
        zero_ref[...] = jnp.zeros_like(zero_ref)

        def fill_expert(e, total):
            count = plan_ref[N_EXPERTS + e]
            first = plan_ref[e] + count
            pad = (-count) & (MOE_ROWS - 1)

            def fill(r, c):
                _row_copy(zero_ref, 0, xs_hbm, first + r, zsem).start()
                return c

            lax.fori_loop(0, pad, fill, 0)
            return total + pad

        total = lax.fori_loop(0, N_EXPERTS, fill_expert, 0)

        def drain(r, c):
            _row_copy(zero_ref, 0, xs_hbm, 0, zsem).wait()
            return c

        lax.fori_loop(0, total, drain, 0)

        def block_copy(b):
            start = pl.multiple_of(b * MOE_ROWS, MOE_ROWS)
            return pltpu.make_async_copy(zero_ref, xs_hbm.at[pl.ds(start, MOE_ROWS)], zsem)

        n_used = plan_ref[plan_ref.shape[0] - 1]
        n_blocks = xs_hbm.shape[0] // MOE_ROWS

        def fill_block(b, c):
            block_copy(b).start()
            return c

        def drain_block(b, c):
            block_copy(b).wait()
            return c

        lax.fori_loop(n_used, n_blocks, fill_block, 0)
        lax.fori_loop(n_used, n_blocks, drain_block, 0)

    def start(r, c):
        d0, d1 = _dest_rows(er_ref, plan_ref, r, rows)
        _row_copy(x_ref, r, xs_hbm, d0, sem).start()
        _row_copy(x_ref, r, xs_hbm, d1, sem).start()
        return c

    lax.fori_loop(0, rows, start, 0, unroll=ISSUE_UNROLL)
    for _ in range(2):
        pltpu.make_async_copy(x_ref, xs_hbm.at[pl.ds(0, rows)], sem).wait()


def _dispatch(x2, er, plan, cap):
    n, d = x2.shape
    rows = MOVE_ROWS
    any_spec = pl.BlockSpec(memory_space=pl.ANY)
    smem = pl.BlockSpec(memory_space=pltpu.SMEM)
    return pl.pallas_call(
        _dispatch_kernel,
        grid=(n // rows,),
        in_specs=[pl.BlockSpec((1, 1, 4 * rows), lambda t: (t, 0, 0), memory_space=pltpu.SMEM),
                  smem,
                  pl.BlockSpec((rows, d), lambda t: (t, 0))],
        out_specs=any_spec,
        out_shape=jax.ShapeDtypeStruct((cap, d), F32),
        scratch_shapes=[pltpu.VMEM((MOE_ROWS, d), F32), pltpu.SemaphoreType.DMA(()),
                        pltpu.SemaphoreType.DMA(())],
        compiler_params=pltpu.CompilerParams(dimension_semantics=("arbitrary",),
                                             has_side_effects=True, vmem_limit_bytes=VMEM_LIMIT),
        name="moe_dispatch",
    )(er, plan, x2)


def _expert_kernel(plan_ref, xs_ref, g_ref, wg_ref, wu_ref, wd_ref, ys_ref, wgb, wub, wdb):
    b = pl.program_id(0)
    e = plan_ref[LANES + b]
    prev = plan_ref[LANES + jnp.maximum(b - 1, 0)]
    n_used = plan_ref[plan_ref.shape[0] - 1]

    @pl.when((b == 0) | (e != prev))
    def _():
        wgb[...] = wg_ref[...].astype(BF16)
        wub[...] = wu_ref[...].astype(BF16)
        wdb[...] = wd_ref[...].astype(BF16)

    @pl.when(b < n_used)
    def _():
        xn = _rms(xs_ref[...], g_ref[...], NORM_EPS).astype(BF16)
        hg = jnp.dot(xn, wgb[...], preferred_element_type=F32)
        hu = jnp.dot(xn, wub[...], preferred_element_type=F32)
        hid = (hg / (1.0 + jnp.exp(-hg))) * hu
        ys_ref[...] = jnp.dot(hid.astype(BF16), wdb[...], preferred_element_type=F32)

    @pl.when(b >= n_used)
    def _():
        ys_ref[...] = jnp.zeros_like(ys_ref)


def _experts(xs, g, w_gate, w_up, w_down, plan):
    cap, d = xs.shape
    de = w_gate.shape[-1]
    rows = MOE_ROWS
    grid_spec = pltpu.PrefetchScalarGridSpec(
        num_scalar_prefetch=1,
        grid=(cap // rows,),
        in_specs=[pl.BlockSpec((rows, d), lambda b, plan: (jnp.where(b < plan[plan.shape[0] - 1], b, 0), 0)),
                  pl.BlockSpec((1, d), lambda b, plan: (0, 0)),
                  pl.BlockSpec((None, d, de), lambda b, plan: (plan[LANES + b], 0, 0)),
                  pl.BlockSpec((None, d, de), lambda b, plan: (plan[LANES + b], 0, 0)),
                  pl.BlockSpec((None, de, d), lambda b, plan: (plan[LANES + b], 0, 0))],
        out_specs=pl.BlockSpec((rows, d), lambda b, plan: (b, 0)),
        scratch_shapes=[pltpu.VMEM((d, de), BF16), pltpu.VMEM((d, de), BF16), pltpu.VMEM((de, d), BF16)],
    )
    return pl.pallas_call(
        _expert_kernel,
        grid_spec=grid_spec,
        out_shape=jax.ShapeDtypeStruct((cap, d), F32),
        compiler_params=_params("arbitrary"),
        name="moe_experts",
    )(plan, xs, g, w_gate, w_up, w_down)


def _combine_kernel(er_ref, plan_ref, x_ref, meta_ref, g_ref, ys_hbm, o_ref, ybuf, sem, *, final_norm):
    rows = x_ref.shape[0]

    def start(r, c):
        d0, d1 = _dest_rows(er_ref, plan_ref, r, rows)
        _row_copy(ys_hbm, d0, ybuf.at[0], r, sem).start()
        _row_copy(ys_hbm, d1, ybuf.at[1], r, sem).start()
        return c

    lax.fori_loop(0, rows, start, 0, unroll=ISSUE_UNROLL)
    for k in range(2):
        pltpu.make_async_copy(ys_hbm.at[pl.ds(0, rows)], ybuf.at[k], sem).wait()
    meta = meta_ref[...]
    out = x_ref[...] + (meta[:, 4:5] * ybuf[0] + meta[:, 5:6] * ybuf[1])
    if final_norm:
        out = _rms(out, g_ref[...], NORM_EPS)
    o_ref[...] = out


def _combine(x2, meta, er, plan, ys, g, final_norm):
    n, d = x2.shape
    rows = MOVE_ROWS
    return pl.pallas_call(
        functools.partial(_combine_kernel, final_norm=final_norm),
        grid=(n // rows,),
        in_specs=[pl.BlockSpec((1, 1, 4 * rows), lambda t: (t, 0, 0), memory_space=pltpu.SMEM),
                  pl.BlockSpec(memory_space=pltpu.SMEM),
                  pl.BlockSpec((rows, d), lambda t: (t, 0)),
                  pl.BlockSpec((rows, LANES), lambda t: (t, 0)),
                  pl.BlockSpec((1, d), lambda t: (0, 0)),
                  pl.BlockSpec(memory_space=pl.ANY)],
        out_specs=pl.BlockSpec((rows, d), lambda t: (t, 0)),
        out_shape=jax.ShapeDtypeStruct((n, d), F32),
        scratch_shapes=[pltpu.VMEM((2, rows, d), F32), pltpu.SemaphoreType.DMA(())],
        compiler_params=_params("arbitrary"),
        name="moe_combine",
    )(er, plan, x2, meta, g, ys)


def _moe(x2, g, w_group, b_group, w_router, b_router, w_gate, w_up, w_down, g_final, final_norm):
    n, d = x2.shape
    pad = LANES - N_GROUPS - N_EXPERTS
    w_pad = jnp.concatenate([w_group, w_router, jnp.zeros((d, pad), F32)], axis=1).astype(BF16)
    b_pad = jnp.concatenate([b_group, b_router, jnp.zeros((pad,), F32)])[None, :]
    meta, er, plan = _router(x2, g, w_pad, b_pad)
    plan = plan.reshape(-1)
    cap = 2 * n + N_EXPERTS * MOE_ROWS
    xs = _dispatch(x2, er, plan, cap)
    ys = _experts(xs, g, w_gate, w_up, w_down, plan)
    return _combine(x2, meta, er, plan, ys, g_final, final_norm)


def _diff_attn_kernel(cp_ref, lam_ref, subg_ref, qt_ref, k_ref, vt_ref, o_ref,
                      m_ref, acc_ref, qx_ref, s_ref, fix_ref, kaug_ref, km_ref, *, lambda_init):
    h = pl.program_id(1)
    i = pl.program_id(2)
    tq = qt_ref.shape[-1]
    tk = DIFF_K
    per_q = tq // tk
    n_groups = k_ref.shape[0] // tq
    cp = [cp_ref[3 * h + p] for p in range(3)]
    slope2 = cp[0] + cp[1] + cp[2]

    @pl.when(i == 0)
    def _():
        kl = lax.broadcasted_iota(jnp.int32, (tk, LANES), 1)
        kr = lax.broadcasted_iota(jnp.int32, (tk, LANES), 0).astype(F32)
        qr = lax.broadcasted_iota(jnp.int32, (LANES, tq), 0)
        ql = lax.broadcasted_iota(jnp.int32, (LANES, tq), 1)
        di_lo = (ql & 255).astype(F32)
        di_hi = (ql >> 8).astype(F32)
        kaug = jnp.zeros((tk, LANES), F32)
        qaug = jnp.zeros((LANES, tq), F32)
        for p in range(3):
            kaug = jnp.where(kl == p, -cp[p], kaug)
            kaug = jnp.where(kl == 3 + p, -256.0 * cp[p], kaug)
            kaug = jnp.where(kl == 6 + p, kr, kaug)
            qaug = jnp.where(qr == p, di_lo, qaug)
            qaug = jnp.where(qr == 3 + p, di_hi, qaug)
            qaug = jnp.where(qr == 6 + p, cp[p], qaug)
        kaug_ref[...] = kaug.astype(BF16)
        for c in range(2):
            qx_ref[0, c, 2 * HEAD_DIM:, :] = qaug.astype(BF16)
            qx_ref[1, c, 2 * HEAD_DIM:, :] = (-qaug).astype(BF16)
        for u in range(per_q):
            rel = (lax.broadcasted_iota(jnp.int32, (tk, tq), 0)
                   - lax.broadcasted_iota(jnp.int32, (tk, tq), 1)) + u * tk
            fix_ref[u] = (2.0 * slope2) * jnp.maximum(rel, 0).astype(F32)
        lane_k = lax.broadcasted_iota(jnp.int32, (1, LANES), 1)
        group_norm = []
        for g in range(n_groups):
            kmax = jnp.max(jnp.abs(k_ref[g * tq:(g + 1) * tq, :].astype(F32)), axis=0, keepdims=True)
            sq = kmax * kmax
            group_norm.append([jnp.sqrt(jnp.sum(jnp.where((lane_k >= c * HEAD_DIM) & (lane_k < (c + 1) * HEAD_DIM),
                                                           sq, 0.0), axis=1, keepdims=True))
                               for c in range(2)])
        for c in range(2):
            running = jnp.zeros((1, 1), F32)
            for g in range(n_groups):
                running = jnp.maximum(running, group_norm[g][c])
                km_ref[0, c, g] = jnp.broadcast_to(running, (1, tq))
            running = jnp.zeros((1, 1), F32)
            for g in reversed(range(n_groups)):
                running = jnp.maximum(running, group_norm[g][c])
                km_ref[1, c, g] = jnp.broadcast_to(running, (1, tq))

    zeros = jnp.zeros((HEAD_DIM, tq), BF16)
    q_maps = (jnp.concatenate([qt_ref[0], zeros], axis=0),
              jnp.concatenate([zeros, qt_ref[1]], axis=0))
    for c in range(2):
        qx_ref[0, c, :2 * HEAD_DIM, :] = q_maps[c]
        qx_ref[1, c, :2 * HEAD_DIM, :] = q_maps[c]
    m_ref[...] = jnp.full_like(m_ref, NEG)
    acc_ref[...] = jnp.zeros_like(acc_ref)
    kaug = kaug_ref[...]

    n_k = n_groups * per_q

    def qk_stage(j, slot):
        jc = jnp.clip(j, 0, n_k - 1)
        start = pl.multiple_of(jc * tk, tk)
        kb = jnp.concatenate([k_ref[pl.ds(start, tk), :], kaug], axis=1)
        after = (jc >= (i + 1) * per_q).astype(jnp.int32)
        for c in range(2):
            s_ref[slot, c] = jnp.dot(kb, qx_ref[after, c], preferred_element_type=F32)

    ones = jnp.ones((SUM_ROWS, tk), BF16)

    def softmax_stage(j, slot, after, fix):
        start = pl.multiple_of(j * tk, tk)
        vb = jnp.concatenate([vt_ref[:, pl.ds(start, tk)], ones], axis=0)
        delta = (i * tq - start).astype(F32)
        off = slope2 * (-delta if after else delta)
        for c in range(2):
            s = s_ref[slot, c]
            if fix is not None:
                s = s - fix_ref[fix]
            m_old = m_ref[c]
            m_new = jnp.maximum(m_old, jnp.max(s, axis=0, keepdims=True) - off)
            alpha = jnp.exp2(m_old - m_new)
            p = jnp.exp2(s - (m_new + off))
            acc_ref[c] = alpha * acc_ref[c] + jnp.dot(vb, p.astype(BF16), preferred_element_type=F32)
            m_ref[c] = m_new

    def fixed_max_stage(j, slot, after):
        start = pl.multiple_of(j * tk, tk)
        vb = jnp.concatenate([vt_ref[:, pl.ds(start, tk)], ones], axis=0)
        delta = (i * tq - start).astype(F32)
        off = slope2 * (-delta if after else delta)
        for c in range(2):
            p = jnp.exp2(s_ref[slot, c] - (m_ref[c] + off))
            acc_ref[c] = acc_ref[c] + jnp.dot(vb, p.astype(BF16), preferred_element_type=F32)

    def pair(j0, next_j0, after, diagonal, fixed_max=False):
        def stage(j, slot, fix):
            if fixed_max:
                fixed_max_stage(j, slot, after)
            else:
                softmax_stage(j, slot, after, fix)
        qk_stage(j0 + 1, 1)
        stage(j0, 0, 0 if diagonal else None)
        qk_stage(next_j0, 0)
        stage(j0 + 1, 1, 1 if diagonal else None)

    def run_groups(first, step, count, after, then_group, fixed_max):
        def body(unroll, base):
            def run(t, carry):
                for u in range(unroll):
                    done = base + t * unroll + u
                    g = first + step * done
                    nxt = jnp.where(done == count - 1, then_group, g + step)
                    pair(g * per_q, nxt * per_q, after, False, fixed_max)
                return carry
            return run
        n_main = count // DIFF_UNROLL
        lax.fori_loop(0, n_main, body(DIFF_UNROLL, 0), 0)
        lax.fori_loop(0, count - n_main * DIFF_UNROLL, body(1, n_main * DIFF_UNROLL), 0)

    qk_stage(i * per_q, 0)
    pair(i * per_q, (i - 1) * per_q, False, True)

    qn = [jnp.sqrt(jnp.sum(jnp.square(qt_ref[c].astype(F32)), axis=0, keepdims=True)) for c in range(2)]

    def groups_needed(side, available):
        cnt = jnp.zeros((1, tq), F32)
        excess = jnp.full((1, tq), NEG, F32)
        for t in range(n_groups - 1):
            g = jnp.clip(i - 1 - t if side == 0 else i + 1 + t, 0, n_groups - 1)
            penalty = slope2 * float(tq * t + 1)
            need = None
            for c in range(2):
                bound = qn[c] * km_ref[side, c, g] - penalty
                reach = bound >= (m_ref[c] - SKIP_MARGIN)
                need = reach if need is None else (need | reach)
                if t == 0:
                    excess = jnp.maximum(excess, bound - m_ref[c])
            cnt = cnt + jnp.where(t < available, jnp.where(need, 1.0, 0.0), 0.0)
        return jnp.max(cnt).astype(jnp.int32), jnp.max(excess) <= FAST_MARGIN

    n_before, calm_before = groups_needed(0, i)
    n_after, calm_after = groups_needed(1, n_groups - 1 - i)

    def run_side(first, step, count, after, then_group, calm):
        @pl.when(calm)
        def _():
            run_groups(first, step, count, after, then_group, True)

        @pl.when(jnp.logical_not(calm))
        def _():
            run_groups(first, step, count, after, then_group, False)

    run_side(i - 1, -1, n_before, False, i + 1, calm_before)

    @pl.when((n_before == 0) & (n_after > 0))
    def _():
        qk_stage((i + 1) * per_q, 0)

    run_side(i + 1, 1, n_after, True, i + 1, calm_after)

    lam = lam_ref[...]
    lam_full = (jnp.exp(jnp.sum(lam[0:1] * lam[1:2], axis=-1, keepdims=True))
                - jnp.exp(jnp.sum(lam[2:3] * lam[3:4], axis=-1, keepdims=True)) + lambda_init)
    vdim = vt_ref.shape[0]
    o = (acc_ref[0, :vdim] / acc_ref[0, vdim:vdim + 1]
         - lam_full * (acc_ref[1, :vdim] / acc_ref[1, vdim:vdim + 1]))
    ms = jnp.mean(o * o, axis=0, keepdims=True)
    on = o * lax.rsqrt(ms + SUBLN_EPS) * subg_ref[...] * (1.0 - lambda_init)
    o_ref[...] = on.T.astype(BF16)


def _slope_pieces(slopes):
    s2 = slopes * LOG2E
    c1 = s2.astype(BF16).astype(F32)
    c2 = (s2 - c1).astype(BF16).astype(F32)
    c3 = (s2 - c1 - c2).astype(BF16).astype(F32)
    return jnp.stack([c1, c2, c3], axis=1).reshape(-1)


def _diff_attn(qt, k, vt, slopes, lam, subg, lambda_init):
    batch, heads, _, _, seq = qt.shape
    vdim = vt.shape[2]
    tq = DIFF_Q
    assert tq == 2 * 256 and DIFF_K == 256 and seq % tq == 0
    slopes = _slope_pieces(slopes)
    smem = pl.BlockSpec(memory_space=pltpu.SMEM)
    return pl.pallas_call(
        functools.partial(_diff_attn_kernel, lambda_init=lambda_init),
        grid=(batch, heads, seq // tq),
        in_specs=[smem,
                  pl.BlockSpec(lam.shape, lambda b, h, i: (0, 0)),
                  pl.BlockSpec(subg.shape, lambda b, h, i: (0, 0)),
                  pl.BlockSpec((None, None, 2, HEAD_DIM, tq), lambda b, h, i: (b, h, 0, 0, i)),
                  pl.BlockSpec((None, seq, 2 * HEAD_DIM), lambda b, h, i: (b, 0, h)),
                  pl.BlockSpec((None, None, vdim, seq), lambda b, h, i: (b, h, 0, 0))],
        out_specs=pl.BlockSpec((None, tq, vdim), lambda b, h, i: (b, i, h)),
        out_shape=jax.ShapeDtypeStruct((batch, seq, heads * vdim), BF16),
        scratch_shapes=[pltpu.VMEM((2, 1, tq), F32),
                        pltpu.VMEM((2, vdim + SUM_ROWS, tq), F32),
                        pltpu.VMEM((2, 2, 2 * HEAD_DIM + LANES, tq), BF16),
                        pltpu.VMEM((2, 2, DIFF_K, tq), F32),
                        pltpu.VMEM((tq // DIFF_K, DIFF_K, tq), F32),
                        pltpu.VMEM((DIFF_K, LANES), BF16),
                        pltpu.VMEM((2, 2, seq // tq, 1, tq), F32)],
        compiler_params=_params("arbitrary", "arbitrary", "arbitrary"),
        name="diff_attn",
    )(slopes, lam, subg, qt, k, vt)


def _alibi_slopes(n_heads):
    h = jnp.arange(1, n_heads + 1, dtype=F32)
    return jnp.exp2(-8.0 * h / n_heads)


def _lambda_init(layer_idx):
    return 0.8 - 0.6 * math.exp(-0.3 * layer_idx)


def kernel(x, norm_mix, norm_ffn, swa_w_qkv, swa_w_o, swa_sink, diff_w_qkv, diff_w_o, diff_lambda,
           diff_subln, moe_w_group, moe_b_group, moe_w_router, moe_b_router, moe_w_gate, moe_w_up,
           moe_w_down, final_norm):
    batch, seq, d = x.shape
    n = batch * seq
    depth = norm_mix.shape[0]
    x2 = x.reshape(n, d)
    q_scale = HEAD_DIM ** -0.5 * LOG2E
    for layer in range(depth):
        j = layer // 2
        g_mix = norm_mix[layer][None, :]
        x3 = x2.reshape(batch, seq, d)
        if layer % 2 == 0:
            n_heads = SWA_KV_HEADS * SWA_GROUP
            qw = n_heads * HEAD_DIM
            kw = SWA_KV_HEADS * HEAD_DIM
            w = swa_w_qkv[j]
            wq = (w[:, :qw] * q_scale).astype(BF16)
            wk = w[:, qw:qw + kw].astype(BF16)
            wv = w[:, qw + kw:].astype(BF16)
            qt, k, vt = _qkv_proj(x3, g_mix, wq, wk, wv, "swa_proj")
            attn = _swa_attn(qt, k, vt, _alibi_slopes(n_heads), swa_sink[j].astype(F32))
            x2 = _out_proj(attn, swa_w_o[j].astype(BF16), x2)
        else:
            heads = d // (2 * HEAD_DIM)
            w = diff_w_qkv[j]
            wq = (w[:, :d] * q_scale).astype(BF16)
            wk = w[:, d:2 * d].astype(BF16)
            wv = w[:, 2 * d:].astype(BF16)
            qt, k, vt = _qkv_proj(x3, g_mix, wq, wk, wv, "diff_proj")
            qt = qt.reshape(batch, heads, 2, HEAD_DIM, seq)
            vt = vt.reshape(batch, heads, 2 * HEAD_DIM, seq)
            attn = _diff_attn(qt, k, vt, _alibi_slopes(heads), diff_lambda[j].astype(F32),
                              diff_subln[j].astype(F32)[:, None], _lambda_init(layer))
            x2 = _out_proj(attn.reshape(n, d), diff_w_o[j].astype(BF16), x2)
        last = layer == depth - 1
        x2 = _moe(x2, norm_ffn[layer][None, :], moe_w_group[layer], moe_b_group[layer],
                  moe_w_router[layer], moe_b_router[layer], moe_w_gate[layer], moe_w_up[layer],
                  moe_w_down[layer], final_norm[None, :], last)
    return x2.reshape(batch, seq, d)
```

```python
import functools
import math

import jax
import jax.numpy as jnp
from jax import lax
from jax.experimental import pallas as pl
from jax.experimental.pallas import tpu as pltpu

F32 = jnp.float32
BF16 = jnp.bfloat16

HEAD_DIM = 64
WINDOW = 128
SWA_KV_HEADS = 4
SWA_GROUP = 4
N_GROUPS = 4
EXPERTS_PER_GROUP = 8
N_EXPERTS = N_GROUPS * EXPERTS_PER_GROUP
NORM_EPS = 1e-6
SUBLN_EPS = 1e-5
NEG = -1e30
LANES = 128
LOG2E = math.log2(math.e)

PROJ_ROWS = 512
SWA_Q = 128
MOE_ROWS = 256
MOVE_ROWS = 512
ISSUE_UNROLL = 8
DIFF_Q = 512
DIFF_K = 256
DIFF_UNROLL = 4
SKIP_MARGIN = 160.0
FAST_MARGIN = 64.0
SUM_ROWS = 16
VMEM_LIMIT = 56 * 1024 * 1024


def _params(*sem):
    return pltpu.CompilerParams(dimension_semantics=sem, vmem_limit_bytes=VMEM_LIMIT)


def _rms(x, g, eps):
    ms = jnp.mean(x * x, axis=-1, keepdims=True)
    return x * lax.rsqrt(ms + eps) * g


def _qkv_proj_kernel(x_ref, g_ref, wq_ref, wk_ref, wv_ref, qt_ref, k_ref, vt_ref, wqt_ref, wvt_ref):
    @pl.when((pl.program_id(0) == 0) & (pl.program_id(1) == 0))
    def _():
        wqt_ref[...] = wq_ref[...].T
        wvt_ref[...] = wv_ref[...].T

    xn = _rms(x_ref[...], g_ref[...], NORM_EPS).astype(BF16)
    nt = (((1,), (1,)), ((), ()))
    qt_ref[...] = lax.dot_general(wqt_ref[...], xn, nt, preferred_element_type=F32).astype(BF16)
    k_ref[...] = jnp.dot(xn, wk_ref[...], preferred_element_type=F32).astype(BF16)
    vt_ref[...] = lax.dot_general(wvt_ref[...], xn, nt, preferred_element_type=F32).astype(BF16)


def _qkv_proj(x3, g, wq, wk, wv, name):
    batch, seq, d = x3.shape
    rows = PROJ_ROWS
    qw, kw, vw = wq.shape[1], wk.shape[1], wv.shape[1]

    def full(w):
        return pl.BlockSpec(w.shape, lambda b, i: (0, 0))

    def transposed(width):
        return pl.BlockSpec((None, width, rows), lambda b, i: (b, 0, i))

    return pl.pallas_call(
        _qkv_proj_kernel,
        grid=(batch, seq // rows),
        in_specs=[pl.BlockSpec((None, rows, d), lambda b, i: (b, i, 0)),
                  pl.BlockSpec((1, d), lambda b, i: (0, 0)),
                  full(wq), full(wk), full(wv)],
        out_specs=[transposed(qw), pl.BlockSpec((None, rows, kw), lambda b, i: (b, i, 0)), transposed(vw)],
        out_shape=[jax.ShapeDtypeStruct((batch, qw, seq), BF16),
                   jax.ShapeDtypeStruct((batch, seq, kw), BF16),
                   jax.ShapeDtypeStruct((batch, vw, seq), BF16)],
        scratch_shapes=[pltpu.VMEM((qw, d), BF16), pltpu.VMEM((vw, d), BF16)],
        compiler_params=_params("arbitrary", "arbitrary"),
        name=name,
    )(x3, g, wq, wk, wv)


def _out_proj_kernel(a_ref, w_ref, x_ref, o_ref):
    o_ref[...] = x_ref[...] + jnp.dot(a_ref[...], w_ref[...], preferred_element_type=F32)


def _out_proj(a, w, x2):
    n, d = x2.shape
    return pl.pallas_call(
        _out_proj_kernel,
        grid=(n // PROJ_ROWS,),
        in_specs=[pl.BlockSpec((PROJ_ROWS, a.shape[1]), lambda i: (i, 0)),
                  pl.BlockSpec(w.shape, lambda i: (0, 0)),
                  pl.BlockSpec((PROJ_ROWS, d), lambda i: (i, 0))],
        out_specs=pl.BlockSpec((PROJ_ROWS, d), lambda i: (i, 0)),
        out_shape=jax.ShapeDtypeStruct((n, d), F32),
        compiler_params=_params("parallel"),
        name="out_proj",
    )(a, w, x2)


def _swa_attn_kernel(slope_ref, sink_ref, qt_ref, kp_ref, kc_ref, kn_ref, vp_ref, vc_ref, vn_ref, o_ref,
                     bias_ref, sinkrow_ref):
    first_step = (pl.program_id(0) == 0) & (pl.program_id(1) == 0)
    i = pl.program_id(1)
    last = pl.num_programs(1) - 1
    tq = SWA_Q
    nk = 3 * tq
    nl = SWA_GROUP * tq

    @pl.when(first_step)
    def _():
        row = lax.broadcasted_iota(jnp.int32, (nk, nl), 0)
        lane = lax.broadcasted_iota(jnp.int32, (nk, nl), 1)
        dist = jnp.abs(row - tq - (lane & (tq - 1)))
        distf = dist.astype(F32)
        lane1 = lax.broadcasted_iota(jnp.int32, (1, nl), 1)
        tq_bits = tq.bit_length() - 1
        for j in range(SWA_KV_HEADS):
            slope = jnp.zeros((nk, nl), F32)
            sink = jnp.zeros((1, nl), F32)
            for g in range(SWA_GROUP):
                h = j * SWA_GROUP + g
                slope = jnp.where((lane >> tq_bits) == g, slope_ref[h], slope)
                sink = jnp.where((lane1 >> tq_bits) == g, sink_ref[h], sink)
            sinkrow_ref[j] = sink
            bias = jnp.where(dist <= WINDOW, -slope * distf, NEG)
            bias_ref[0, j] = bias
            bias_ref[1, j] = jnp.where(row < tq, NEG, bias)
            bias_ref[2, j] = jnp.where(row >= 2 * tq, NEG, bias)

    edge = jnp.where(i == 0, 1, jnp.where(i == last, 2, 0))
    kwin = jnp.concatenate([kp_ref[...], kc_ref[...], kn_ref[...]], axis=0)
    kvw = kwin.shape[1]
    for j in range(SWA_KV_HEADS):
        rows = slice(j * HEAD_DIM, (j + 1) * HEAD_DIM)
        qj = jnp.concatenate([qt_ref[(j * SWA_GROUP + g) * HEAD_DIM:(j * SWA_GROUP + g + 1) * HEAD_DIM, :]
                              for g in range(SWA_GROUP)], axis=1)
        parts = []
        if j > 0:
            parts.append(jnp.zeros((j * HEAD_DIM, nl), BF16))
        parts.append(qj)
        if (j + 1) * HEAD_DIM < kvw:
            parts.append(jnp.zeros((kvw - (j + 1) * HEAD_DIM, nl), BF16))
        qpad = jnp.concatenate(parts, axis=0)
        s = jnp.dot(kwin, qpad, preferred_element_type=F32) + bias_ref[edge, j]
        sink = sinkrow_ref[j]
        m = jnp.maximum(jnp.max(s, axis=0, keepdims=True), sink)
        p = jnp.exp2(s - m)
        vwin = jnp.concatenate([vp_ref[rows, :], vc_ref[rows, :], vn_ref[rows, :]], axis=1)
        vwin = jnp.concatenate([vwin, jnp.ones((SUM_ROWS, nk), BF16)], axis=0)
        pv = jnp.dot(vwin, p.astype(BF16), preferred_element_type=F32)
        denom = pv[HEAD_DIM:HEAD_DIM + 1] + jnp.exp2(sink - m)
        ot = pv[:HEAD_DIM] / denom
        for g in range(SWA_GROUP):
            h = j * SWA_GROUP + g
            o_ref[:, h * HEAD_DIM:(h + 1) * HEAD_DIM] = ot[:, g * tq:(g + 1) * tq].T.astype(BF16)


def _swa_attn(qt, k, vt, slopes, sink):
    batch, qw, seq = qt.shape
    kvw = k.shape[2]
    nb = seq // SWA_Q
    smem = pl.BlockSpec(memory_space=pltpu.SMEM)

    def k_spec(shift):
        return pl.BlockSpec((None, SWA_Q, kvw), lambda b, i: (b, jnp.clip(i + shift, 0, nb - 1), 0))

    def v_spec(shift):
        return pl.BlockSpec((None, kvw, SWA_Q), lambda b, i: (b, 0, jnp.clip(i + shift, 0, nb - 1)))

    out = pl.pallas_call(
        _swa_attn_kernel,
        grid=(batch, nb),
        in_specs=[smem, smem,
                  pl.BlockSpec((None, qw, SWA_Q), lambda b, i: (b, 0, i)),
                  k_spec(-1), k_spec(0), k_spec(1), v_spec(-1), v_spec(0), v_spec(1)],
        out_specs=pl.BlockSpec((None, SWA_Q, qw), lambda b, i: (b, i, 0)),
        out_shape=jax.ShapeDtypeStruct((batch, seq, qw), BF16),
        scratch_shapes=[pltpu.VMEM((3, SWA_KV_HEADS, 3 * SWA_Q, SWA_GROUP * SWA_Q), F32),
                        pltpu.VMEM((SWA_KV_HEADS, 1, SWA_GROUP * SWA_Q), F32)],
        compiler_params=_params("arbitrary", "arbitrary"),
        name="swa_attn",
    )(slopes * LOG2E, sink * LOG2E, qt, k, k, k, vt, vt, vt)
    return out.reshape(batch * seq, qw)


def _router_kernel(x_ref, g_ref, w_ref, b_ref, meta_ref, dest_ref, plan_ref, carry_ref, er_ref):
    t = pl.program_id(0)
    rows = x_ref.shape[0]

    @pl.when(t == 0)
    def _():
        carry_ref[...] = jnp.zeros_like(carry_ref)

    xn = _rms(x_ref[...], g_ref[...], NORM_EPS).astype(BF16)
    logits = jnp.dot(xn, w_ref[...], preferred_element_type=F32) + b_ref[...]
    lane = lax.broadcasted_iota(jnp.int32, (rows, LANES), 1)

    is_group = lane < N_GROUPS
    gl = jnp.where(is_group, logits, NEG)
    gmax = jnp.max(gl, axis=-1, keepdims=True)
    gidx = jnp.min(jnp.where(gl == gmax, lane, LANES), axis=-1, keepdims=True)
    zg = jnp.sum(jnp.where(is_group, jnp.exp(gl - gmax), 0.0), axis=-1, keepdims=True)
    pg = 1.0 / zg

    lo = N_GROUPS + EXPERTS_PER_GROUP * gidx
    in_group = (lane >= lo) & (lane < lo + EXPERTS_PER_GROUP)
    el = jnp.where(in_group, logits, NEG)
    m0 = jnp.max(el, axis=-1, keepdims=True)
    i0 = jnp.min(jnp.where(el == m0, lane, LANES), axis=-1, keepdims=True)
    el1 = jnp.where(lane == i0, NEG, el)
    m1 = jnp.max(el1, axis=-1, keepdims=True)
    i1 = jnp.min(jnp.where(el1 == m1, lane, LANES), axis=-1, keepdims=True)
    r = jnp.exp(m1 - m0)
    w0 = pg / (1.0 + r)
    w1 = pg * r / (1.0 + r)
    e0 = i0 - N_GROUPS
    e1 = i1 - N_GROUPS

    sel0 = lane == e0
    sel1 = lane == e1
    onehot = (sel0 | sel1).astype(BF16)
    rr = lax.broadcasted_iota(jnp.int32, (rows, rows), 0)
    cc = lax.broadcasted_iota(jnp.int32, (rows, rows), 1)
    tri = (cc < rr).astype(BF16)
    before = jnp.dot(tri, onehot, preferred_element_type=F32) + carry_ref[...]
    r0 = jnp.sum(jnp.where(sel0, before, 0.0), axis=-1, keepdims=True)
    r1 = jnp.sum(jnp.where(sel1, before, 0.0), axis=-1, keepdims=True)
    carry_ref[...] = carry_ref[...] + jnp.sum(onehot.astype(F32), axis=0, keepdims=True)

    meta = jnp.where(lane == 0, e0.astype(F32), 0.0)
    meta = jnp.where(lane == 1, e1.astype(F32), meta)
    meta = jnp.where(lane == 2, r0, meta)
    meta = jnp.where(lane == 3, r1, meta)
    meta = jnp.where(lane == 4, w0, meta)
    meta = jnp.where(lane == 5, w1, meta)
    meta_ref[...] = meta
    metat = meta.T
    for f in range(4):
        er_ref[t, :, f * rows:(f + 1) * rows] = metat[f:f + 1, :].astype(jnp.int32)

    @pl.when(t == pl.num_programs(0) - 1)
    def _plan():
        counts = jnp.broadcast_to(carry_ref[...], (8, LANES))
        padded = jnp.floor((counts + (MOE_ROWS - 1)) * (1.0 / MOE_ROWS)) * MOE_ROWS
        lane8 = lax.broadcasted_iota(jnp.int32, (8, LANES), 1)
        pend = padded
        shift = 1
        while shift < LANES:
            pend = pend + jnp.where(lane8 >= shift, pltpu.roll(pend, shift, 1), 0.0)
            shift *= 2
        starts_counts = jnp.where(lane8 < N_EXPERTS, pend - padded, pltpu.roll(counts, N_EXPERTS, 1))
        plan_ref[:, 0:LANES] = starts_counts[0:1].astype(jnp.int32)
        sq_r = lax.broadcasted_iota(jnp.int32, (LANES, LANES), 0)
        sq_c = lax.broadcasted_iota(jnp.int32, (LANES, LANES), 1)
        pend_col = jnp.sum(jnp.where(sq_r == sq_c, jnp.broadcast_to(pend[0:1], (LANES, LANES)), 0.0),
                           axis=1, keepdims=True)
        nbp = plan_ref.shape[1] - LANES
        blk_row = lax.broadcasted_iota(jnp.int32, (LANES, nbp), 0)
        blk_start = (lax.broadcasted_iota(jnp.int32, (LANES, nbp), 1) * MOE_ROWS).astype(F32)
        ended = (pend_col <= blk_start) & (blk_row < N_EXPERTS)
        block_e = jnp.minimum(jnp.sum(ended.astype(F32), axis=0, keepdims=True), N_EXPERTS - 1.0)
        n_used = jnp.sum(jnp.where(lane8[0:1] == N_EXPERTS - 1, pend[0:1], 0.0), axis=1,
                         keepdims=True) * (1.0 / MOE_ROWS)
        blk_lane = lax.broadcasted_iota(jnp.int32, (1, nbp), 1)
        plan_ref[:, LANES:] = jnp.where(blk_lane == nbp - 1, n_used, block_e).astype(jnp.int32)

        starts = (pend - padded)[0:1]
        start_of = [jnp.sum(jnp.where(lane8[0:1] == k, starts, 0.0), axis=1, keepdims=True)
                    for k in range(N_EXPERTS)]

        def tile_dest(tt, carry):
            er = er_ref[tt]
            for slot in range(2):
                e = er[:, slot * rows:(slot + 1) * rows]
                rank = er[:, (2 + slot) * rows:(3 + slot) * rows]
                base = jnp.zeros((1, rows), F32)
                for k in range(N_EXPERTS):
                    base = jnp.where(e == k, start_of[k], base)
                dest_ref[tt, :, slot * rows:(slot + 1) * rows] = base.astype(jnp.int32) + rank
            return carry

        lax.fori_loop(0, pl.num_programs(0), tile_dest, 0)


def _plan_width(n):
    n_blocks = (2 * n + N_EXPERTS * MOE_ROWS) // MOE_ROWS
    return LANES + (n_blocks // LANES + 1) * LANES


def _router(x2, g, w_pad, b_pad):
    n, d = x2.shape
    rows = MOVE_ROWS
    tiles = n // rows
    pw = _plan_width(n)
    return pl.pallas_call(
        _router_kernel,
        grid=(tiles,),
        in_specs=[pl.BlockSpec((rows, d), lambda t: (t, 0)),
                  pl.BlockSpec((1, d), lambda t: (0, 0)),
                  pl.BlockSpec((d, LANES), lambda t: (0, 0)),
                  pl.BlockSpec((1, LANES), lambda t: (0, 0))],
        out_specs=[pl.BlockSpec((rows, LANES), lambda t: (t, 0)),
                   pl.BlockSpec((tiles, 1, 2 * rows), lambda t: (0, 0, 0)),
                   pl.BlockSpec((1, pw), lambda t: (0, 0))],
        out_shape=[jax.ShapeDtypeStruct((n, LANES), F32),
                   jax.ShapeDtypeStruct((tiles, 1, 2 * rows), jnp.int32),
                   jax.ShapeDtypeStruct((1, pw), jnp.int32)],
        scratch_shapes=[pltpu.VMEM((1, LANES), F32), pltpu.VMEM((tiles, 1, 4 * rows), jnp.int32)],
        compiler_params=_params("arbitrary"),
        name="router",
    )(x2, g, w_pad, b_pad)


def _row_copy(src, src_row, dst, dst_row, sem):
    return pltpu.make_async_copy(src.at[pl.ds(src_row, 1)], dst.at[pl.ds(dst_row, 1)], sem)


def _dest_rows(dest_ref, r, rows):
    return dest_ref[0, 0, r], dest_ref[0, 0, rows + r]


def _dispatch_kernel(dest_ref, plan_ref, x_ref, xs_hbm, zero_ref, sem, zsem):
    rows = x_ref.shape[0]

    @pl.when(pl.program_id(0) == 0)
    def _():
        zero_ref[...] = jnp.zeros_like(zero_ref)

        def fill_expert(e, total):
            count = plan_ref[N_EXPERTS + e]
            first = plan_ref[e] + count
            pad = (-count) & (MOE_ROWS - 1)

            def fill(r, c):
                _row_copy(zero_ref, 0, xs_hbm, first + r, zsem).start()
                return c

            lax.fori_loop(0, pad, fill, 0)
            return total + pad

        total = lax.fori_loop(0, N_EXPERTS, fill_expert, 0)

        def drain(r, c):
            _row_copy(zero_ref, 0, xs_hbm, 0, zsem).wait()
            return c

        lax.fori_loop(0, total, drain, 0)

        def block_copy(b):
            start = pl.multiple_of(b * MOE_ROWS, MOE_ROWS)
            return pltpu.make_async_copy(zero_ref, xs_hbm.at[pl.ds(start, MOE_ROWS)], zsem)

        n_used = plan_ref[plan_ref.shape[0] - 1]
        n_blocks = xs_hbm.shape[0] // MOE_ROWS

        def fill_block(b, c):
            block_copy(b).start()
            return c

        def drain_block(b, c):
            block_copy(b).wait()
            return c

        lax.fori_loop(n_used, n_blocks, fill_block, 0)
        lax.fori_loop(n_used, n_blocks, drain_block, 0)

    def start(r, c):
        d0, d1 = _dest_rows(dest_ref, r, rows)
        _row_copy(x_ref, r, xs_hbm, d0, sem).start()
        _row_copy(x_ref, r, xs_hbm, d1, sem).start()
        return c

    lax.fori_loop(0, rows, start, 0, unroll=ISSUE_UNROLL)
    for _ in range(2):
        pltpu.make_async_copy(x_ref, xs_hbm.at[pl.ds(0, rows)], sem).wait()


def _dispatch(x2, dest, plan, cap):
    n, d = x2.shape
    rows = MOVE_ROWS
    any_spec = pl.BlockSpec(memory_space=pl.ANY)
    smem = pl.BlockSpec(memory_space=pltpu.SMEM)
    return pl.pallas_call(
        _dispatch_kernel,
        grid=(n // rows,),
        in_specs=[pl.BlockSpec((1, 1, 2 * rows), lambda t: (t, 0, 0), memory_space=pltpu.SMEM),
                  smem,
                  pl.BlockSpec((rows, d), lambda t: (t, 0))],
        out_specs=any_spec,
        out_shape=jax.ShapeDtypeStruct((cap, d), F32),
        scratch_shapes=[pltpu.VMEM((MOE_ROWS, d), F32), pltpu.SemaphoreType.DMA(()),
                        pltpu.SemaphoreType.DMA(())],
        compiler_params=pltpu.CompilerParams(dimension_semantics=("arbitrary",),
                                             has_side_effects=True, vmem_limit_bytes=VMEM_LIMIT),
        name="moe_dispatch",
    )(dest, plan, x2)


def _expert_kernel(plan_ref, xs_ref, g_ref, wg_ref, wu_ref, wd_ref, ys_ref, wgb, wub, wdb):
    b = pl.program_id(0)
    e = plan_ref[LANES + b]
    prev = plan_ref[LANES + jnp.maximum(b - 1, 0)]
    n_used = plan_ref[plan_ref.shape[0] - 1]

    @pl.when((b == 0) | (e != prev))
    def _():
        wgb[...] = wg_ref[...].astype(BF16)
        wub[...] = wu_ref[...].astype(BF16)
        wdb[...] = wd_ref[...].astype(BF16)

    @pl.when(b < n_used)
    def _():
        xn = _rms(xs_ref[...], g_ref[...], NORM_EPS).astype(BF16)
        hg = jnp.dot(xn, wgb[...], preferred_element_type=F32)
        hu = jnp.dot(xn, wub[...], preferred_element_type=F32)
        hid = (hg / (1.0 + jnp.exp(-hg))) * hu
        ys_ref[...] = jnp.dot(hid.astype(BF16), wdb[...], preferred_element_type=F32)

    @pl.when(b >= n_used)
    def _():
        ys_ref[...] = jnp.zeros_like(ys_ref)


def _experts(xs, g, w_gate, w_up, w_down, plan, layer):
    cap, d = xs.shape
    de = w_gate.shape[-1]
    rows = MOE_ROWS

    def expert_block(b, plan):
        return (layer, plan[LANES + b], 0, 0)

    grid_spec = pltpu.PrefetchScalarGridSpec(
        num_scalar_prefetch=1,
        grid=(cap // rows,),
        in_specs=[pl.BlockSpec((rows, d), lambda b, plan: (jnp.where(b < plan[plan.shape[0] - 1], b, 0), 0)),
                  pl.BlockSpec((1, d), lambda b, plan: (0, 0)),
                  pl.BlockSpec((None, None, d, de), expert_block),
                  pl.BlockSpec((None, None, d, de), expert_block),
                  pl.BlockSpec((None, None, de, d), expert_block)],
        out_specs=pl.BlockSpec((rows, d), lambda b, plan: (b, 0)),
        scratch_shapes=[pltpu.VMEM((d, de), BF16), pltpu.VMEM((d, de), BF16), pltpu.VMEM((de, d), BF16)],
    )
    return pl.pallas_call(
        _expert_kernel,
        grid_spec=grid_spec,
        out_shape=jax.ShapeDtypeStruct((cap, d), F32),
        compiler_params=_params("arbitrary"),
        name="moe_experts",
    )(plan, xs, g, w_gate, w_up, w_down)


def _combine_kernel(dest_ref, x_ref, meta_ref, g_ref, ys_hbm, o_ref, ybuf, sem, *, final_norm):
    rows = x_ref.shape[0]

    def start(r, c):
        d0, d1 = _dest_rows(dest_ref, r, rows)
        _row_copy(ys_hbm, d0, ybuf.at[0], r, sem).start()
        _row_copy(ys_hbm, d1, ybuf.at[1], r, sem).start()
        return c

    lax.fori_loop(0, rows, start, 0, unroll=ISSUE_UNROLL)
    for k in range(2):
        pltpu.make_async_copy(ys_hbm.at[pl.ds(0, rows)], ybuf.at[k], sem).wait()
    meta = meta_ref[...]
    out = x_ref[...] + (meta[:, 4:5] * ybuf[0] + meta[:, 5:6] * ybuf[1])
    if final_norm:
        out = _rms(out, g_ref[...], NORM_EPS)
    o_ref[...] = out


def _combine(x2, meta, dest, ys, g, final_norm):
    n, d = x2.shape
    rows = MOVE_ROWS
    return pl.pallas_call(
        functools.partial(_combine_kernel, final_norm=final_norm),
        grid=(n // rows,),
        in_specs=[pl.BlockSpec((1, 1, 2 * rows), lambda t: (t, 0, 0), memory_space=pltpu.SMEM),
                  pl.BlockSpec((rows, d), lambda t: (t, 0)),
                  pl.BlockSpec((rows, LANES), lambda t: (t, 0)),
                  pl.BlockSpec((1, d), lambda t: (0, 0)),
                  pl.BlockSpec(memory_space=pl.ANY)],
        out_specs=pl.BlockSpec((rows, d), lambda t: (t, 0)),
        out_shape=jax.ShapeDtypeStruct((n, d), F32),
        scratch_shapes=[pltpu.VMEM((2, rows, d), F32), pltpu.SemaphoreType.DMA(())],
        compiler_params=_params("arbitrary"),
        name="moe_combine",
    )(dest, x2, meta, g, ys)


def _moe(x2, g, w_group, b_group, w_router, b_router, w_gate, w_up, w_down, layer, g_final, final_norm):
    n, d = x2.shape
    pad = LANES - N_GROUPS - N_EXPERTS
    w_pad = jnp.concatenate([w_group, w_router, jnp.zeros((d, pad), F32)], axis=1).astype(BF16)
    b_pad = jnp.concatenate([b_group, b_router, jnp.zeros((pad,), F32)])[None, :]
    meta, dest, plan = _router(x2, g, w_pad, b_pad)
    plan = plan.reshape(-1)
    cap = 2 * n + N_EXPERTS * MOE_ROWS
    xs = _dispatch(x2, dest, plan, cap)
    ys = _experts(xs, g, w_gate, w_up, w_down, plan, layer)
    return _combine(x2, meta, dest, ys, g_final, final_norm)


def _diff_attn_kernel(cp_ref, lam_ref, subg_ref, qt_ref, k_ref, vt_ref, o_ref,
                      m_ref, acc_ref, qx_ref, s_ref, fix_ref, kaug_ref, km_ref, *, lambda_init):
    h = pl.program_id(1)
    i = pl.program_id(2)
    tq = qt_ref.shape[-1]
    tk = DIFF_K
    per_q = tq // tk
    n_groups = k_ref.shape[0] // tq
    cp = [cp_ref[3 * h + p] for p in range(3)]
    slope2 = cp[0] + cp[1] + cp[2]

    @pl.when(i == 0)
    def _():
        kl = lax.broadcasted_iota(jnp.int32, (tk, LANES), 1)
        kr = lax.broadcasted_iota(jnp.int32, (tk, LANES), 0).astype(F32)
        qr = lax.broadcasted_iota(jnp.int32, (LANES, tq), 0)
        ql = lax.broadcasted_iota(jnp.int32, (LANES, tq), 1)
        di_lo = (ql & 255).astype(F32)
        di_hi = (ql >> 8).astype(F32)
        kaug = jnp.zeros((tk, LANES), F32)
        qaug = jnp.zeros((LANES, tq), F32)
        for p in range(3):
            kaug = jnp.where(kl == p, -cp[p], kaug)
            kaug = jnp.where(kl == 3 + p, -256.0 * cp[p], kaug)
            kaug = jnp.where(kl == 6 + p, kr, kaug)
            qaug = jnp.where(qr == p, di_lo, qaug)
            qaug = jnp.where(qr == 3 + p, di_hi, qaug)
            qaug = jnp.where(qr == 6 + p, cp[p], qaug)
        kaug_ref[...] = kaug.astype(BF16)
        for c in range(2):
            qx_ref[0, c, 2 * HEAD_DIM:, :] = qaug.astype(BF16)
            qx_ref[1, c, 2 * HEAD_DIM:, :] = (-qaug).astype(BF16)
        for u in range(per_q):
            rel = (lax.broadcasted_iota(jnp.int32, (tk, tq), 0)
                   - lax.broadcasted_iota(jnp.int32, (tk, tq), 1)) + u * tk
            fix_ref[u] = (2.0 * slope2) * jnp.maximum(rel, 0).astype(F32)
        lane_k = lax.broadcasted_iota(jnp.int32, (1, LANES), 1)
        group_norm = []
        for g in range(n_groups):
            kmax = jnp.max(jnp.abs(k_ref[g * tq:(g + 1) * tq, :].astype(F32)), axis=0, keepdims=True)
            sq = kmax * kmax
            group_norm.append([jnp.sqrt(jnp.sum(jnp.where((lane_k >= c * HEAD_DIM) & (lane_k < (c + 1) * HEAD_DIM),
                                                           sq, 0.0), axis=1, keepdims=True))
                               for c in range(2)])
        for c in range(2):
            running = jnp.zeros((1, 1), F32)
            for g in range(n_groups):
                running = jnp.maximum(running, group_norm[g][c])
                km_ref[0, c, g] = jnp.broadcast_to(running, (1, tq))
            running = jnp.zeros((1, 1), F32)
            for g in reversed(range(n_groups)):
                running = jnp.maximum(running, group_norm[g][c])
                km_ref[1, c, g] = jnp.broadcast_to(running, (1, tq))

    zeros = jnp.zeros((HEAD_DIM, tq), BF16)
    q_maps = (jnp.concatenate([qt_ref[0], zeros], axis=0),
              jnp.concatenate([zeros, qt_ref[1]], axis=0))
    for c in range(2):
        qx_ref[0, c, :2 * HEAD_DIM, :] = q_maps[c]
        qx_ref[1, c, :2 * HEAD_DIM, :] = q_maps[c]
    m_ref[...] = jnp.full_like(m_ref, NEG)
    acc_ref[...] = jnp.zeros_like(acc_ref)
    kaug = kaug_ref[...]

    n_k = n_groups * per_q

    def qk_stage(j, slot):
        jc = jnp.clip(j, 0, n_k - 1)
        start = pl.multiple_of(jc * tk, tk)
        kb = jnp.concatenate([k_ref[pl.ds(start, tk), :], kaug], axis=1)
        after = (jc >= (i + 1) * per_q).astype(jnp.int32)
        for c in range(2):
            s_ref[slot, c] = jnp.dot(kb, qx_ref[after, c], preferred_element_type=F32)

    ones = jnp.ones((SUM_ROWS, tk), BF16)

    def softmax_stage(j, slot, after, fix):
        start = pl.multiple_of(j * tk, tk)
        vb = jnp.concatenate([vt_ref[:, pl.ds(start, tk)], ones], axis=0)
        delta = (i * tq - start).astype(F32)
        off = slope2 * (-delta if after else delta)
        for c in range(2):
            s = s_ref[slot, c]
            if fix is not None:
                s = s - fix_ref[fix]
            m_old = m_ref[c]
            m_new = jnp.maximum(m_old, jnp.max(s, axis=0, keepdims=True) - off)
            alpha = jnp.exp2(m_old - m_new)
            p = jnp.exp2(s - (m_new + off))
            acc_ref[c] = alpha * acc_ref[c] + jnp.dot(vb, p.astype(BF16), preferred_element_type=F32)
            m_ref[c] = m_new

    def fixed_max_stage(j, slot, after):
        start = pl.multiple_of(j * tk, tk)
        vb = jnp.concatenate([vt_ref[:, pl.ds(start, tk)], ones], axis=0)
        delta = (i * tq - start).astype(F32)
        off = slope2 * (-delta if after else delta)
        for c in range(2):
            p = jnp.exp2(s_ref[slot, c] - (m_ref[c] + off))
            acc_ref[c] = acc_ref[c] + jnp.dot(vb, p.astype(BF16), preferred_element_type=F32)

    def pair(j0, next_j0, after, diagonal, fixed_max=False):
        def stage(j, slot, fix):
            if fixed_max:
                fixed_max_stage(j, slot, after)
            else:
                softmax_stage(j, slot, after, fix)
        qk_stage(j0 + 1, 1)
        stage(j0, 0, 0 if diagonal else None)
        qk_stage(next_j0, 0)
        stage(j0 + 1, 1, 1 if diagonal else None)

    def run_groups(first, step, count, after, then_group, fixed_max):
        def body(unroll, base):
            def run(t, carry):
                for u in range(unroll):
                    done = base + t * unroll + u
                    g = first + step * done
                    nxt = jnp.where(done == count - 1, then_group, g + step)
                    pair(g * per_q, nxt * per_q, after, False, fixed_max)
                return carry
            return run
        n_main = count // DIFF_UNROLL
        lax.fori_loop(0, n_main, body(DIFF_UNROLL, 0), 0)
        lax.fori_loop(0, count - n_main * DIFF_UNROLL, body(1, n_main * DIFF_UNROLL), 0)

    qk_stage(i * per_q, 0)
    pair(i * per_q, (i - 1) * per_q, False, True)

    qn = [jnp.sqrt(jnp.sum(jnp.square(qt_ref[c].astype(F32)), axis=0, keepdims=True)) for c in range(2)]

    def groups_needed(side, available):
        cnt = jnp.zeros((1, tq), F32)
        excess = jnp.full((1, tq), NEG, F32)
        for t in range(n_groups - 1):
            g = jnp.clip(i - 1 - t if side == 0 else i + 1 + t, 0, n_groups - 1)
            penalty = slope2 * float(tq * t + 1)
            need = None
            for c in range(2):
                bound = qn[c] * km_ref[side, c, g] - penalty
                reach = bound >= (m_ref[c] - SKIP_MARGIN)
                need = reach if need is None else (need | reach)
                if t == 0:
                    excess = jnp.maximum(excess, bound - m_ref[c])
            cnt = cnt + jnp.where(t < available, jnp.where(need, 1.0, 0.0), 0.0)
        return jnp.max(cnt).astype(jnp.int32), jnp.max(excess) <= FAST_MARGIN

    n_before, calm_before = groups_needed(0, i)
    n_after, calm_after = groups_needed(1, n_groups - 1 - i)

    def run_side(first, step, count, after, then_group, calm):
        @pl.when(calm)
        def _():
            run_groups(first, step, count, after, then_group, True)

        @pl.when(jnp.logical_not(calm))
        def _():
            run_groups(first, step, count, after, then_group, False)

    run_side(i - 1, -1, n_before, False, i + 1, calm_before)

    @pl.when((n_before == 0) & (n_after > 0))
    def _():
        qk_stage((i + 1) * per_q, 0)

    run_side(i + 1, 1, n_after, True, i + 1, calm_after)

    lam = lam_ref[...]
    lam_full = (jnp.exp(jnp.sum(lam[0:1] * lam[1:2], axis=-1, keepdims=True))
                - jnp.exp(jnp.sum(lam[2:3] * lam[3:4], axis=-1, keepdims=True)) + lambda_init)
    vdim = vt_ref.shape[0]
    o = (acc_ref[0, :vdim] / acc_ref[0, vdim:vdim + 1]
         - lam_full * (acc_ref[1, :vdim] / acc_ref[1, vdim:vdim + 1]))
    ms = jnp.mean(o * o, axis=0, keepdims=True)
    on = o * lax.rsqrt(ms + SUBLN_EPS) * subg_ref[...] * (1.0 - lambda_init)
    o_ref[...] = on.T.astype(BF16)


def _slope_pieces(slopes):
    s2 = slopes * LOG2E
    c1 = s2.astype(BF16).astype(F32)
    c2 = (s2 - c1).astype(BF16).astype(F32)
    c3 = (s2 - c1 - c2).astype(BF16).astype(F32)
    return jnp.stack([c1, c2, c3], axis=1).reshape(-1)


def _diff_attn(qt, k, vt, slopes, lam, subg, lambda_init):
    batch, heads, _, _, seq = qt.shape
    vdim = vt.shape[2]
    tq = DIFF_Q
    assert tq == 2 * 256 and DIFF_K == 256 and seq % tq == 0
    slopes = _slope_pieces(slopes)
    smem = pl.BlockSpec(memory_space=pltpu.SMEM)
    return pl.pallas_call(
        functools.partial(_diff_attn_kernel, lambda_init=lambda_init),
        grid=(batch, heads, seq // tq),
        in_specs=[smem,
                  pl.BlockSpec(lam.shape, lambda b, h, i: (0, 0)),
                  pl.BlockSpec(subg.shape, lambda b, h, i: (0, 0)),
                  pl.BlockSpec((None, None, 2, HEAD_DIM, tq), lambda b, h, i: (b, h, 0, 0, i)),
                  pl.BlockSpec((None, seq, 2 * HEAD_DIM), lambda b, h, i: (b, 0, h)),
                  pl.BlockSpec((None, None, vdim, seq), lambda b, h, i: (b, h, 0, 0))],
        out_specs=pl.BlockSpec((None, tq, vdim), lambda b, h, i: (b, i, h)),
        out_shape=jax.ShapeDtypeStruct((batch, seq, heads * vdim), BF16),
        scratch_shapes=[pltpu.VMEM((2, 1, tq), F32),
                        pltpu.VMEM((2, vdim + SUM_ROWS, tq), F32),
                        pltpu.VMEM((2, 2, 2 * HEAD_DIM + LANES, tq), BF16),
                        pltpu.VMEM((2, 2, DIFF_K, tq), F32),
                        pltpu.VMEM((tq // DIFF_K, DIFF_K, tq), F32),
                        pltpu.VMEM((DIFF_K, LANES), BF16),
                        pltpu.VMEM((2, 2, seq // tq, 1, tq), F32)],
        compiler_params=_params("arbitrary", "arbitrary", "arbitrary"),
        name="diff_attn",
    )(slopes, lam, subg, qt, k, vt)


def _alibi_slopes(n_heads):
    h = jnp.arange(1, n_heads + 1, dtype=F32)
    return jnp.exp2(-8.0 * h / n_heads)


def _lambda_init(layer_idx):
    return 0.8 - 0.6 * math.exp(-0.3 * layer_idx)


def kernel(x, norm_mix, norm_ffn, swa_w_qkv, swa_w_o, swa_sink, diff_w_qkv, diff_w_o, diff_lambda,
           diff_subln, moe_w_group, moe_b_group, moe_w_router, moe_b_router, moe_w_gate, moe_w_up,
           moe_w_down, final_norm):
    batch, seq, d = x.shape
    n = batch * seq
    depth = norm_mix.shape[0]
    x2 = x.reshape(n, d)
    q_scale = HEAD_DIM ** -0.5 * LOG2E
    for layer in range(depth):
        j = layer // 2
        g_mix = norm_mix[layer][None, :]
        x3 = x2.reshape(batch, seq, d)
        if layer % 2 == 0:
            n_heads = SWA_KV_HEADS * SWA_GROUP
            qw = n_heads * HEAD_DIM
            kw = SWA_KV_HEADS * HEAD_DIM
            w = swa_w_qkv[j]
            wq = (w[:, :qw] * q_scale).astype(BF16)
            wk = w[:, qw:qw + kw].astype(BF16)
            wv = w[:, qw + kw:].astype(BF16)
            qt, k, vt = _qkv_proj(x3, g_mix, wq, wk, wv, "swa_proj")
            attn = _swa_attn(qt, k, vt, _alibi_slopes(n_heads), swa_sink[j].astype(F32))
            x2 = _out_proj(attn, swa_w_o[j].astype(BF16), x2)
        else:
            heads = d // (2 * HEAD_DIM)
            w = diff_w_qkv[j]
            wq = (w[:, :d] * q_scale).astype(BF16)
            wk = w[:, d:2 * d].astype(BF16)
            wv = w[:, 2 * d:].astype(BF16)
            qt, k, vt = _qkv_proj(x3, g_mix, wq, wk, wv, "diff_proj")
            qt = qt.reshape(batch, heads, 2, HEAD_DIM, seq)
            vt = vt.reshape(batch, heads, 2 * HEAD_DIM, seq)
            attn = _diff_attn(qt, k, vt, _alibi_slopes(heads), diff_lambda[j].astype(F32),
                              diff_subln[j].astype(F32)[:, None], _lambda_init(layer))
            x2 = _out_proj(attn.reshape(n, d), diff_w_o[j].astype(BF16), x2)
        last = layer == depth - 1
        x2 = _moe(x2, norm_ffn[layer][None, :], moe_w_group[layer], moe_b_group[layer],
                  moe_w_router[layer], moe_b_router[layer], moe_w_gate, moe_w_up, moe_w_down, layer,
                  final_norm[None, :], last)
    return x2.reshape(batch, seq, d)
```

```python
import functools
import math

import jax
import jax.numpy as jnp
from jax import lax
from jax.experimental import pallas as pl
from jax.experimental.pallas import tpu as pltpu

F32 = jnp.float32
BF16 = jnp.bfloat16

HEAD_DIM = 64
WINDOW = 128
SWA_KV_HEADS = 4
SWA_GROUP = 4
N_GROUPS = 4
EXPERTS_PER_GROUP = 8
N_EXPERTS = N_GROUPS * EXPERTS_PER_GROUP
NORM_EPS = 1e-6
SUBLN_EPS = 1e-5
NEG = -1e30
LANES = 128
LOG2E = math.log2(math.e)

PROJ_ROWS = 512
SWA_Q = 128
MOE_ROWS = 512
MOVE_ROWS = 512
ISSUE_UNROLL = 8
DIFF_Q = 512
DIFF_K = 256
DIFF_UNROLL = 4
SKIP_MARGIN = 160.0
FAST_MARGIN = 64.0
SUM_ROWS = 16
VMEM_LIMIT = 56 * 1024 * 1024


def _params(*sem):
    return pltpu.CompilerParams(dimension_semantics=sem, vmem_limit_bytes=VMEM_LIMIT)


def _rms(x, g, eps):
    ms = jnp.mean(x * x, axis=-1, keepdims=True)
    return x * lax.rsqrt(ms + eps) * g


def _qkv_proj_kernel(x_ref, g_ref, wq_ref, wk_ref, wv_ref, qt_ref, k_ref, vt_ref, wqt_ref, wvt_ref):
    @pl.when((pl.program_id(0) == 0) & (pl.program_id(1) == 0))
    def _():
        wqt_ref[...] = wq_ref[...].T
        wvt_ref[...] = wv_ref[...].T

    xn = _rms(x_ref[...], g_ref[...], NORM_EPS).astype(BF16)
    nt = (((1,), (1,)), ((), ()))
    qt_ref[...] = lax.dot_general(wqt_ref[...], xn, nt, preferred_element_type=F32).astype(BF16)
    k_ref[...] = jnp.dot(xn, wk_ref[...], preferred_element_type=F32).astype(BF16)
    vt_ref[...] = lax.dot_general(wvt_ref[...], xn, nt, preferred_element_type=F32).astype(BF16)


def _qkv_proj(x3, g, wq, wk, wv, name):
    batch, seq, d = x3.shape
    rows = PROJ_ROWS
    qw, kw, vw = wq.shape[1], wk.shape[1], wv.shape[1]

    def full(w):
        return pl.BlockSpec(w.shape, lambda b, i: (0, 0))

    def transposed(width):
        return pl.BlockSpec((None, width, rows), lambda b, i: (b, 0, i))

    return pl.pallas_call(
        _qkv_proj_kernel,
        grid=(batch, seq // rows),
        in_specs=[pl.BlockSpec((None, rows, d), lambda b, i: (b, i, 0)),
                  pl.BlockSpec((1, d), lambda b, i: (0, 0)),
                  full(wq), full(wk), full(wv)],
        out_specs=[transposed(qw), pl.BlockSpec((None, rows, kw), lambda b, i: (b, i, 0)), transposed(vw)],
        out_shape=[jax.ShapeDtypeStruct((batch, qw, seq), BF16),
                   jax.ShapeDtypeStruct((batch, seq, kw), BF16),
                   jax.ShapeDtypeStruct((batch, vw, seq), BF16)],
        scratch_shapes=[pltpu.VMEM((qw, d), BF16), pltpu.VMEM((vw, d), BF16)],
        compiler_params=_params("arbitrary", "arbitrary"),
        name=name,
    )(x3, g, wq, wk, wv)


def _swa_attn_kernel(slope_ref, sink_ref, qt_ref, kp_ref, kc_ref, kn_ref, vp_ref, vc_ref, vn_ref, o_ref,
                     bias_ref, sinkrow_ref):
    first_step = (pl.program_id(0) == 0) & (pl.program_id(1) == 0)
    i = pl.program_id(1)
    last = pl.num_programs(1) - 1
    tq = SWA_Q
    nk = 3 * tq
    nl = SWA_GROUP * tq

    @pl.when(first_step)
    def _():
        row = lax.broadcasted_iota(jnp.int32, (nk, nl), 0)
        lane = lax.broadcasted_iota(jnp.int32, (nk, nl), 1)
        dist = jnp.abs(row - tq - (lane & (tq - 1)))
        distf = dist.astype(F32)
        lane1 = lax.broadcasted_iota(jnp.int32, (1, nl), 1)
        tq_bits = tq.bit_length() - 1
        for j in range(SWA_KV_HEADS):
            slope = jnp.zeros((nk, nl), F32)
            sink = jnp.zeros((1, nl), F32)
            for g in range(SWA_GROUP):
                h = j * SWA_GROUP + g
                slope = jnp.where((lane >> tq_bits) == g, slope_ref[h], slope)
                sink = jnp.where((lane1 >> tq_bits) == g, sink_ref[h], sink)
            sinkrow_ref[j] = sink
            bias = jnp.where(dist <= WINDOW, -slope * distf, NEG)
            bias_ref[0, j] = bias
            bias_ref[1, j] = jnp.where(row < tq, NEG, bias)
            bias_ref[2, j] = jnp.where(row >= 2 * tq, NEG, bias)

    edge = jnp.where(i == 0, 1, jnp.where(i == last, 2, 0))
    kwin = jnp.concatenate([kp_ref[...], kc_ref[...], kn_ref[...]], axis=0)
    kvw = kwin.shape[1]
    for j in range(SWA_KV_HEADS):
        rows = slice(j * HEAD_DIM, (j + 1) * HEAD_DIM)
        qj = jnp.concatenate([qt_ref[(j * SWA_GROUP + g) * HEAD_DIM:(j * SWA_GROUP + g + 1) * HEAD_DIM, :]
                              for g in range(SWA_GROUP)], axis=1)
        parts = []
        if j > 0:
            parts.append(jnp.zeros((j * HEAD_DIM, nl), BF16))
        parts.append(qj)
        if (j + 1) * HEAD_DIM < kvw:
            parts.append(jnp.zeros((kvw - (j + 1) * HEAD_DIM, nl), BF16))
        qpad = jnp.concatenate(parts, axis=0)
        s = jnp.dot(kwin, qpad, preferred_element_type=F32) + bias_ref[edge, j]
        sink = sinkrow_ref[j]
        m = jnp.maximum(jnp.max(s, axis=0, keepdims=True), sink)
        p = jnp.exp2(s - m)
        vwin = jnp.concatenate([vp_ref[rows, :], vc_ref[rows, :], vn_ref[rows, :]], axis=1)
        vwin = jnp.concatenate([vwin, jnp.ones((SUM_ROWS, nk), BF16)], axis=0)
        pv = jnp.dot(vwin, p.astype(BF16), preferred_element_type=F32)
        denom = pv[HEAD_DIM:HEAD_DIM + 1] + jnp.exp2(sink - m)
        ot = pv[:HEAD_DIM] / denom
        for g in range(SWA_GROUP):
            h = j * SWA_GROUP + g
            o_ref[:, h * HEAD_DIM:(h + 1) * HEAD_DIM] = ot[:, g * tq:(g + 1) * tq].T.astype(BF16)


def _swa_attn(qt, k, vt, slopes, sink):
    batch, qw, seq = qt.shape
    kvw = k.shape[2]
    nb = seq // SWA_Q
    smem = pl.BlockSpec(memory_space=pltpu.SMEM)

    def k_spec(shift):
        return pl.BlockSpec((None, SWA_Q, kvw), lambda b, i: (b, jnp.clip(i + shift, 0, nb - 1), 0))

    def v_spec(shift):
        return pl.BlockSpec((None, kvw, SWA_Q), lambda b, i: (b, 0, jnp.clip(i + shift, 0, nb - 1)))

    out = pl.pallas_call(
        _swa_attn_kernel,
        grid=(batch, nb),
        in_specs=[smem, smem,
                  pl.BlockSpec((None, qw, SWA_Q), lambda b, i: (b, 0, i)),
                  k_spec(-1), k_spec(0), k_spec(1), v_spec(-1), v_spec(0), v_spec(1)],
        out_specs=pl.BlockSpec((None, SWA_Q, qw), lambda b, i: (b, i, 0)),
        out_shape=jax.ShapeDtypeStruct((batch, seq, qw), BF16),
        scratch_shapes=[pltpu.VMEM((3, SWA_KV_HEADS, 3 * SWA_Q, SWA_GROUP * SWA_Q), F32),
                        pltpu.VMEM((SWA_KV_HEADS, 1, SWA_GROUP * SWA_Q), F32)],
        compiler_params=_params("arbitrary", "arbitrary"),
        name="swa_attn",
    )(slopes * LOG2E, sink * LOG2E, qt, k, k, k, vt, vt, vt)
    return out.reshape(batch * seq, qw)


def _router_kernel(a_ref, wo_ref, x_ref, g_ref, w_ref, b_ref, xo_ref, meta_ref, dest_ref, plan_ref,
                   carry_ref, er_ref):
    t = pl.program_id(0)
    rows = x_ref.shape[0]

    @pl.when(t == 0)
    def _():
        carry_ref[...] = jnp.zeros_like(carry_ref)

    x = x_ref[...] + jnp.dot(a_ref[...], wo_ref[...], preferred_element_type=F32)
    xo_ref[...] = x
    xn = _rms(x, g_ref[...], NORM_EPS).astype(BF16)
    logits = jnp.dot(xn, w_ref[...], preferred_element_type=F32) + b_ref[...]
    lane = lax.broadcasted_iota(jnp.int32, (rows, LANES), 1)

    is_group = lane < N_GROUPS
    gl = jnp.where(is_group, logits, NEG)
    gmax = jnp.max(gl, axis=-1, keepdims=True)
    gidx = jnp.min(jnp.where(gl == gmax, lane, LANES), axis=-1, keepdims=True)
    zg = jnp.sum(jnp.where(is_group, jnp.exp(gl - gmax), 0.0), axis=-1, keepdims=True)
    pg = 1.0 / zg

    lo = N_GROUPS + EXPERTS_PER_GROUP * gidx
    in_group = (lane >= lo) & (lane < lo + EXPERTS_PER_GROUP)
    el = jnp.where(in_group, logits, NEG)
    m0 = jnp.max(el, axis=-1, keepdims=True)
    i0 = jnp.min(jnp.where(el == m0, lane, LANES), axis=-1, keepdims=True)
    el1 = jnp.where(lane == i0, NEG, el)
    m1 = jnp.max(el1, axis=-1, keepdims=True)
    i1 = jnp.min(jnp.where(el1 == m1, lane, LANES), axis=-1, keepdims=True)
    r = jnp.exp(m1 - m0)
    w0 = pg / (1.0 + r)
    w1 = pg * r / (1.0 + r)
    e0 = i0 - N_GROUPS
    e1 = i1 - N_GROUPS

    sel0 = lane == e0
    sel1 = lane == e1
    onehot = (sel0 | sel1).astype(BF16)
    rr = lax.broadcasted_iota(jnp.int32, (rows, rows), 0)
    cc = lax.broadcasted_iota(jnp.int32, (rows, rows), 1)
    tri = (cc < rr).astype(BF16)
    before = jnp.dot(tri, onehot, preferred_element_type=F32) + carry_ref[...]
    r0 = jnp.sum(jnp.where(sel0, before, 0.0), axis=-1, keepdims=True)
    r1 = jnp.sum(jnp.where(sel1, before, 0.0), axis=-1, keepdims=True)
    carry_ref[...] = carry_ref[...] + jnp.sum(onehot.astype(F32), axis=0, keepdims=True)

    meta = jnp.where(lane == 0, e0.astype(F32), 0.0)
    meta = jnp.where(lane == 1, e1.astype(F32), meta)
    meta = jnp.where(lane == 2, r0, meta)
    meta = jnp.where(lane == 3, r1, meta)
    meta = jnp.where(lane == 4, w0, meta)
    meta = jnp.where(lane == 5, w1, meta)
    meta_ref[...] = meta
    metat = meta.T
    for f in range(4):
        er_ref[t, :, f * rows:(f + 1) * rows] = metat[f:f + 1, :].astype(jnp.int32)

    @pl.when(t == pl.num_programs(0) - 1)
    def _plan():
        counts = jnp.broadcast_to(carry_ref[...], (8, LANES))
        padded = jnp.floor((counts + (MOE_ROWS - 1)) * (1.0 / MOE_ROWS)) * MOE_ROWS
        lane8 = lax.broadcasted_iota(jnp.int32, (8, LANES), 1)
        pend = padded
        shift = 1
        while shift < LANES:
            pend = pend + jnp.where(lane8 >= shift, pltpu.roll(pend, shift, 1), 0.0)
            shift *= 2
        starts_counts = jnp.where(lane8 < N_EXPERTS, pend - padded, pltpu.roll(counts, N_EXPERTS, 1))
        plan_ref[:, 0:LANES] = starts_counts[0:1].astype(jnp.int32)
        sq_r = lax.broadcasted_iota(jnp.int32, (LANES, LANES), 0)
        sq_c = lax.broadcasted_iota(jnp.int32, (LANES, LANES), 1)
        pend_col = jnp.sum(jnp.where(sq_r == sq_c, jnp.broadcast_to(pend[0:1], (LANES, LANES)), 0.0),
                           axis=1, keepdims=True)
        nbp = plan_ref.shape[1] - LANES
        blk_row = lax.broadcasted_iota(jnp.int32, (LANES, nbp), 0)
        blk_start = (lax.broadcasted_iota(jnp.int32, (LANES, nbp), 1) * MOE_ROWS).astype(F32)
        ended = (pend_col <= blk_start) & (blk_row < N_EXPERTS)
        block_e = jnp.minimum(jnp.sum(ended.astype(F32), axis=0, keepdims=True), N_EXPERTS - 1.0)
        n_used = jnp.sum(jnp.where(lane8[0:1] == N_EXPERTS - 1, pend[0:1], 0.0), axis=1,
                         keepdims=True) * (1.0 / MOE_ROWS)
        blk_lane = lax.broadcasted_iota(jnp.int32, (1, nbp), 1)
        plan_ref[:, LANES:] = jnp.where(blk_lane == nbp - 1, n_used, block_e).astype(jnp.int32)

        starts = (pend - padded)[0:1]
        start_of = [jnp.sum(jnp.where(lane8[0:1] == k, starts, 0.0), axis=1, keepdims=True)
                    for k in range(N_EXPERTS)]

        def tile_dest(tt, carry):
            er = er_ref[tt]
            for slot in range(2):
                e = er[:, slot * rows:(slot + 1) * rows]
                rank = er[:, (2 + slot) * rows:(3 + slot) * rows]
                base = jnp.zeros((1, rows), F32)
                for k in range(N_EXPERTS):
                    base = jnp.where(e == k, start_of[k], base)
                dest_ref[tt, :, slot * rows:(slot + 1) * rows] = base.astype(jnp.int32) + rank
            return carry

        lax.fori_loop(0, pl.num_programs(0), tile_dest, 0)


def _plan_width(n):
    n_blocks = (2 * n + N_EXPERTS * MOE_ROWS) // MOE_ROWS
    return LANES + (n_blocks // LANES + 1) * LANES


def _router(attn, w_o, x2, g, w_pad, b_pad):
    n, d = x2.shape
    rows = MOVE_ROWS
    tiles = n // rows
    pw = _plan_width(n)
    return pl.pallas_call(
        _router_kernel,
        grid=(tiles,),
        in_specs=[pl.BlockSpec((rows, attn.shape[1]), lambda t: (t, 0)),
                  pl.BlockSpec(w_o.shape, lambda t: (0, 0)),
                  pl.BlockSpec((rows, d), lambda t: (t, 0)),
                  pl.BlockSpec((1, d), lambda t: (0, 0)),
                  pl.BlockSpec((d, LANES), lambda t: (0, 0)),
                  pl.BlockSpec((1, LANES), lambda t: (0, 0))],
        out_specs=[pl.BlockSpec((rows, d), lambda t: (t, 0)),
                   pl.BlockSpec((rows, LANES), lambda t: (t, 0)),
                   pl.BlockSpec((tiles, 1, 2 * rows), lambda t: (0, 0, 0)),
                   pl.BlockSpec((1, pw), lambda t: (0, 0))],
        out_shape=[jax.ShapeDtypeStruct((n, d), F32),
                   jax.ShapeDtypeStruct((n, LANES), F32),
                   jax.ShapeDtypeStruct((tiles, 1, 2 * rows), jnp.int32),
                   jax.ShapeDtypeStruct((1, pw), jnp.int32)],
        scratch_shapes=[pltpu.VMEM((1, LANES), F32), pltpu.VMEM((tiles, 1, 4 * rows), jnp.int32)],
        compiler_params=_params("arbitrary"),
        name="router",
    )(attn, w_o, x2, g, w_pad, b_pad)


def _row_copy(src, src_row, dst, dst_row, sem):
    return pltpu.make_async_copy(src.at[pl.ds(src_row, 1)], dst.at[pl.ds(dst_row, 1)], sem)


def _dest_rows(dest_ref, r, rows):
    return dest_ref[0, 0, r], dest_ref[0, 0, rows + r]


def _dispatch_kernel(dest_ref, plan_ref, x_ref, xs_hbm, zero_ref, sem, zsem):
    rows = x_ref.shape[0]

    @pl.when(pl.program_id(0) == 0)
    def _():
        zero_ref[...] = jnp.zeros_like(zero_ref)

        def fill_expert(e, total):
            count = plan_ref[N_EXPERTS + e]
            first = plan_ref[e] + count
            pad = (-count) & (MOE_ROWS - 1)

            def fill(r, c):
                _row_copy(zero_ref, 0, xs_hbm, first + r, zsem).start()
                return c

            lax.fori_loop(0, pad, fill, 0)
            return total + pad

        total = lax.fori_loop(0, N_EXPERTS, fill_expert, 0)

        def drain(r, c):
            _row_copy(zero_ref, 0, xs_hbm, 0, zsem).wait()
            return c

        lax.fori_loop(0, total, drain, 0)

        def block_copy(b):
            start = pl.multiple_of(b * MOE_ROWS, MOE_ROWS)
            return pltpu.make_async_copy(zero_ref, xs_hbm.at[pl.ds(start, MOE_ROWS)], zsem)

        n_used = plan_ref[plan_ref.shape[0] - 1]
        n_blocks = xs_hbm.shape[0] // MOE_ROWS

        def fill_block(b, c):
            block_copy(b).start()
            return c

        def drain_block(b, c):
            block_copy(b).wait()
            return c

        lax.fori_loop(n_used, n_blocks, fill_block, 0)
        lax.fori_loop(n_used, n_blocks, drain_block, 0)

    def start(r, c):
        d0, d1 = _dest_rows(dest_ref, r, rows)
        _row_copy(x_ref, r, xs_hbm, d0, sem).start()
        _row_copy(x_ref, r, xs_hbm, d1, sem).start(priority=1)
        return c

    lax.fori_loop(0, rows, start, 0, unroll=ISSUE_UNROLL)
    for _ in range(2):
        pltpu.make_async_copy(x_ref, xs_hbm.at[pl.ds(0, rows)], sem).wait()


def _dispatch(x2, dest, plan, cap):
    n, d = x2.shape
    rows = MOVE_ROWS
    any_spec = pl.BlockSpec(memory_space=pl.ANY)
    smem = pl.BlockSpec(memory_space=pltpu.SMEM)
    return pl.pallas_call(
        _dispatch_kernel,
        grid=(n // rows,),
        in_specs=[pl.BlockSpec((1, 1, 2 * rows), lambda t: (t, 0, 0), memory_space=pltpu.SMEM),
                  smem,
                  pl.BlockSpec((rows, d), lambda t: (t, 0))],
        out_specs=any_spec,
        out_shape=jax.ShapeDtypeStruct((cap, d), F32),
        scratch_shapes=[pltpu.VMEM((MOE_ROWS, d), F32), pltpu.SemaphoreType.DMA(()),
                        pltpu.SemaphoreType.DMA(())],
        compiler_params=pltpu.CompilerParams(dimension_semantics=("arbitrary",),
                                             has_side_effects=True, vmem_limit_bytes=VMEM_LIMIT),
        name="moe_dispatch",
    )(dest, plan, x2)


def _expert_kernel(plan_ref, xs_ref, g_ref, wg_ref, wu_ref, wd_ref, ys_ref, wgb, wub, wdb):
    b = pl.program_id(0)
    e = plan_ref[LANES + b]
    prev = plan_ref[LANES + jnp.maximum(b - 1, 0)]
    n_used = plan_ref[plan_ref.shape[0] - 1]

    @pl.when((b == 0) | (e != prev))
    def _():
        wgb[...] = wg_ref[...].astype(BF16)
        wub[...] = wu_ref[...].astype(BF16)
        wdb[...] = wd_ref[...].astype(BF16)

    @pl.when(b < n_used)
    def _():
        xn = _rms(xs_ref[...], g_ref[...], NORM_EPS).astype(BF16)
        hg = jnp.dot(xn, wgb[...], preferred_element_type=F32)
        hu = jnp.dot(xn, wub[...], preferred_element_type=F32)
        hid = (hg / (1.0 + jnp.exp(-hg))) * hu
        ys_ref[...] = jnp.dot(hid.astype(BF16), wdb[...], preferred_element_type=F32)

    @pl.when(b >= n_used)
    def _():
        ys_ref[...] = jnp.zeros_like(ys_ref)


def _experts(xs, g, w_gate, w_up, w_down, plan, layer):
    cap, d = xs.shape
    de = w_gate.shape[-1]
    rows = MOE_ROWS

    def expert_block(b, plan):
        return (layer, plan[LANES + b], 0, 0)

    grid_spec = pltpu.PrefetchScalarGridSpec(
        num_scalar_prefetch=1,
        grid=(cap // rows,),
        in_specs=[pl.BlockSpec((rows, d), lambda b, plan: (jnp.where(b < plan[plan.shape[0] - 1], b, 0), 0)),
                  pl.BlockSpec((1, d), lambda b, plan: (0, 0)),
                  pl.BlockSpec((None, None, d, de), expert_block),
                  pl.BlockSpec((None, None, d, de), expert_block),
                  pl.BlockSpec((None, None, de, d), expert_block)],
        out_specs=pl.BlockSpec((rows, d), lambda b, plan: (b, 0)),
        scratch_shapes=[pltpu.VMEM((d, de), BF16), pltpu.VMEM((d, de), BF16), pltpu.VMEM((de, d), BF16)],
    )
    return pl.pallas_call(
        _expert_kernel,
        grid_spec=grid_spec,
        out_shape=jax.ShapeDtypeStruct((cap, d), F32),
        compiler_params=_params("arbitrary"),
        name="moe_experts",
    )(plan, xs, g, w_gate, w_up, w_down)


def _combine_kernel(dest_ref, x_ref, meta_ref, g_ref, ys_hbm, o_ref, ybuf, sem, *, final_norm):
    rows = x_ref.shape[0]

    def start(r, c):
        d0, d1 = _dest_rows(dest_ref, r, rows)
        _row_copy(ys_hbm, d0, ybuf.at[0], r, sem).start()
        _row_copy(ys_hbm, d1, ybuf.at[1], r, sem).start(priority=1)
        return c

    lax.fori_loop(0, rows, start, 0, unroll=ISSUE_UNROLL)
    for k in range(2):
        pltpu.make_async_copy(ys_hbm.at[pl.ds(0, rows)], ybuf.at[k], sem).wait()
    meta = meta_ref[...]
    out = x_ref[...] + (meta[:, 4:5] * ybuf[0] + meta[:, 5:6] * ybuf[1])
    if final_norm:
        out = _rms(out, g_ref[...], NORM_EPS)
    o_ref[...] = out


def _combine(x2, meta, dest, ys, g, final_norm):
    n, d = x2.shape
    rows = MOVE_ROWS
    return pl.pallas_call(
        functools.partial(_combine_kernel, final_norm=final_norm),
        grid=(n // rows,),
        in_specs=[pl.BlockSpec((1, 1, 2 * rows), lambda t: (t, 0, 0), memory_space=pltpu.SMEM),
                  pl.BlockSpec((rows, d), lambda t: (t, 0)),
                  pl.BlockSpec((rows, LANES), lambda t: (t, 0)),
                  pl.BlockSpec((1, d), lambda t: (0, 0)),
                  pl.BlockSpec(memory_space=pl.ANY)],
        out_specs=pl.BlockSpec((rows, d), lambda t: (t, 0)),
        out_shape=jax.ShapeDtypeStruct((n, d), F32),
        scratch_shapes=[pltpu.VMEM((2, rows, d), F32), pltpu.SemaphoreType.DMA(())],
        compiler_params=_params("arbitrary"),
        name="moe_combine",
    )(dest, x2, meta, g, ys)


def _attn_out_moe(attn, w_o, x2, g, w_group, b_group, w_router, b_router, w_gate, w_up, w_down, layer,
                  g_final, final_norm):
    n, d = x2.shape
    pad = LANES - N_GROUPS - N_EXPERTS
    w_pad = jnp.concatenate([w_group, w_router, jnp.zeros((d, pad), F32)], axis=1).astype(BF16)
    b_pad = jnp.concatenate([b_group, b_router, jnp.zeros((pad,), F32)])[None, :]
    x2, meta, dest, plan = _router(attn, w_o, x2, g, w_pad, b_pad)
    plan = plan.reshape(-1)
    cap = 2 * n + N_EXPERTS * MOE_ROWS
    xs = _dispatch(x2, dest, plan, cap)
    ys = _experts(xs, g, w_gate, w_up, w_down, plan, layer)
    return _combine(x2, meta, dest, ys, g_final, final_norm)


def _diff_attn_kernel(cp_ref, lam_ref, subg_ref, qt_ref, k_ref, vt_ref, o_ref,
                      m_ref, acc_ref, qx_ref, s_ref, fix_ref, kaug_ref, km_ref, *, lambda_init):
    h = pl.program_id(1)
    i = pl.program_id(2)
    tq = qt_ref.shape[-1]
    tk = DIFF_K
    per_q = tq // tk
    n_groups = k_ref.shape[0] // tq
    cp = [cp_ref[3 * h + p] for p in range(3)]
    slope2 = cp[0] + cp[1] + cp[2]

    @pl.when(i == 0)
    def _():
        kl = lax.broadcasted_iota(jnp.int32, (tk, LANES), 1)
        kr = lax.broadcasted_iota(jnp.int32, (tk, LANES), 0).astype(F32)
        qr = lax.broadcasted_iota(jnp.int32, (LANES, tq), 0)
        ql = lax.broadcasted_iota(jnp.int32, (LANES, tq), 1)
        di_lo = (ql & 255).astype(F32)
        di_hi = (ql >> 8).astype(F32)
        kaug = jnp.zeros((tk, LANES), F32)
        qaug = jnp.zeros((LANES, tq), F32)
        for p in range(3):
            kaug = jnp.where(kl == p, -cp[p], kaug)
            kaug = jnp.where(kl == 3 + p, -256.0 * cp[p], kaug)
            kaug = jnp.where(kl == 6 + p, kr, kaug)
            qaug = jnp.where(qr == p, di_lo, qaug)
            qaug = jnp.where(qr == 3 + p, di_hi, qaug)
            qaug = jnp.where(qr == 6 + p, cp[p], qaug)
        kaug_ref[...] = kaug.astype(BF16)
        for c in range(2):
            qx_ref[0, c, 2 * HEAD_DIM:, :] = qaug.astype(BF16)
            qx_ref[1, c, 2 * HEAD_DIM:, :] = (-qaug).astype(BF16)
        for u in range(per_q):
            rel = (lax.broadcasted_iota(jnp.int32, (tk, tq), 0)
                   - lax.broadcasted_iota(jnp.int32, (tk, tq), 1)) + u * tk
            fix_ref[u] = (2.0 * slope2) * jnp.maximum(rel, 0).astype(F32)
        lane_k = lax.broadcasted_iota(jnp.int32, (1, LANES), 1)
        group_norm = []
        for g in range(n_groups):
            kmax = jnp.max(jnp.abs(k_ref[g * tq:(g + 1) * tq, :].astype(F32)), axis=0, keepdims=True)
            sq = kmax * kmax
            group_norm.append([jnp.sqrt(jnp.sum(jnp.where((lane_k >= c * HEAD_DIM) & (lane_k < (c + 1) * HEAD_DIM),
                                                           sq, 0.0), axis=1, keepdims=True))
                               for c in range(2)])
        for c in range(2):
            running = jnp.zeros((1, 1), F32)
            row = jnp.zeros((1, LANES), F32)
            for g in range(n_groups):
                running = jnp.maximum(running, group_norm[g][c])
                row = jnp.where(lane_k == g, running, row)
            km_ref[0, c] = row
            running = jnp.zeros((1, 1), F32)
            row = jnp.zeros((1, LANES), F32)
            for g in reversed(range(n_groups)):
                running = jnp.maximum(running, group_norm[g][c])
                row = jnp.where(lane_k == g, running, row)
            km_ref[1, c] = row

    zeros = jnp.zeros((HEAD_DIM, tq), BF16)
    q_maps = (jnp.concatenate([qt_ref[0], zeros], axis=0),
              jnp.concatenate([zeros, qt_ref[1]], axis=0))
    for c in range(2):
        qx_ref[0, c, :2 * HEAD_DIM, :] = q_maps[c]
        qx_ref[1, c, :2 * HEAD_DIM, :] = q_maps[c]
    m_ref[...] = jnp.full_like(m_ref, NEG)
    acc_ref[...] = jnp.zeros_like(acc_ref)
    kaug = kaug_ref[...]

    n_k = n_groups * per_q

    def qk_stage(j, slot):
        jc = jnp.clip(j, 0, n_k - 1)
        start = pl.multiple_of(jc * tk, tk)
        kb = jnp.concatenate([k_ref[pl.ds(start, tk), :], kaug], axis=1)
        after = (jc >= (i + 1) * per_q).astype(jnp.int32)
        for c in range(2):
            s_ref[slot, c] = jnp.dot(kb, qx_ref[after, c], preferred_element_type=F32)

    ones = jnp.ones((SUM_ROWS, tk), BF16)

    def softmax_stage(j, slot, after, fix):
        start = pl.multiple_of(j * tk, tk)
        vb = jnp.concatenate([vt_ref[:, pl.ds(start, tk)], ones], axis=0)
        delta = (i * tq - start).astype(F32)
        off = slope2 * (-delta if after else delta)
        for c in range(2):
            s = s_ref[slot, c]
            if fix is not None:
                s = s - fix_ref[fix]
            m_old = m_ref[c]
            m_new = jnp.maximum(m_old, jnp.max(s, axis=0, keepdims=True) - off)
            alpha = jnp.exp2(m_old - m_new)
            p = jnp.exp2(s - (m_new + off))
            acc_ref[c] = alpha * acc_ref[c] + jnp.dot(vb, p.astype(BF16), preferred_element_type=F32)
            m_ref[c] = m_new

    def fixed_max_stage(j, slot, after):
        start = pl.multiple_of(j * tk, tk)
        vb = jnp.concatenate([vt_ref[:, pl.ds(start, tk)], ones], axis=0)
        delta = (i * tq - start).astype(F32)
        off = slope2 * (-delta if after else delta)
        for c in range(2):
            p = jnp.exp2(s_ref[slot, c] - (m_ref[c] + off))
            acc_ref[c] = acc_ref[c] + jnp.dot(vb, p.astype(BF16), preferred_element_type=F32)

    def pair(j0, next_j0, after, diagonal, fixed_max=False):
        def stage(j, slot, fix):
            if fixed_max:
                fixed_max_stage(j, slot, after)
            else:
                softmax_stage(j, slot, after, fix)
        qk_stage(j0 + 1, 1)
        stage(j0, 0, 0 if diagonal else None)
        qk_stage(next_j0, 0)
        stage(j0 + 1, 1, 1 if diagonal else None)

    def run_groups(first, step, count, after, then_group, fixed_max):
        def body(unroll, base):
            def run(t, carry):
                for u in range(unroll):
                    done = base + t * unroll + u
                    g = first + step * done
                    nxt = jnp.where(done == count - 1, then_group, g + step)
                    pair(g * per_q, nxt * per_q, after, False, fixed_max)
                return carry
            return run
        n_main = count // DIFF_UNROLL
        lax.fori_loop(0, n_main, body(DIFF_UNROLL, 0), 0)
        lax.fori_loop(0, count - n_main * DIFF_UNROLL, body(1, n_main * DIFF_UNROLL), 0)

    qk_stage(i * per_q, 0)
    pair(i * per_q, (i - 1) * per_q, False, True)

    qn = [jnp.sqrt(jnp.sum(jnp.square(qt_ref[c].astype(F32)), axis=0, keepdims=True)) for c in range(2)]

    step_t = lax.broadcasted_iota(jnp.int32, (n_groups, 1), 0)
    step_grid = lax.broadcasted_iota(jnp.int32, (n_groups, LANES), 0)
    lane_grid = lax.broadcasted_iota(jnp.int32, (n_groups, LANES), 1)
    penalty = slope2 * (step_t * tq + 1).astype(F32)

    def groups_needed(side, available):
        group_of_step = (i - 1 - step_grid) if side == 0 else (i + 1 + step_grid)
        pick = lane_grid == group_of_step
        need = None
        excess = jnp.full((1, tq), NEG, F32)
        for c in range(2):
            km = jnp.sum(jnp.where(pick, km_ref[side, c], 0.0), axis=1, keepdims=True)
            bound = qn[c] * km - penalty
            reach = bound >= (m_ref[c] - SKIP_MARGIN)
            need = reach if need is None else (need | reach)
            excess = jnp.maximum(excess, bound[0:1] - m_ref[c])
        cnt = jnp.sum(jnp.where(need & (step_t < available), 1.0, 0.0), axis=0, keepdims=True)
        return jnp.max(cnt).astype(jnp.int32), jnp.max(excess) <= FAST_MARGIN

    n_before, calm_before = groups_needed(0, i)
    n_after, calm_after = groups_needed(1, n_groups - 1 - i)

    def run_side(first, step, count, after, then_group, calm):
        @pl.when(calm)
        def _():
            run_groups(first, step, count, after, then_group, True)

        @pl.when(jnp.logical_not(calm))
        def _():
            run_groups(first, step, count, after, then_group, False)

    run_side(i - 1, -1, n_before, False, i + 1, calm_before)

    @pl.when((n_before == 0) & (n_after > 0))
    def _():
        qk_stage((i + 1) * per_q, 0)

    run_side(i + 1, 1, n_after, True, i + 1, calm_after)

    lam = lam_ref[...]
    lam_full = (jnp.exp(jnp.sum(lam[0:1] * lam[1:2], axis=-1, keepdims=True))
                - jnp.exp(jnp.sum(lam[2:3] * lam[3:4], axis=-1, keepdims=True)) + lambda_init)
    vdim = vt_ref.shape[0]
    o = (acc_ref[0, :vdim] / acc_ref[0, vdim:vdim + 1]
         - lam_full * (acc_ref[1, :vdim] / acc_ref[1, vdim:vdim + 1]))
    ms = jnp.mean(o * o, axis=0, keepdims=True)
    on = o * lax.rsqrt(ms + SUBLN_EPS) * subg_ref[...] * (1.0 - lambda_init)
    o_ref[...] = on.T.astype(BF16)


def _slope_pieces(slopes):
    s2 = slopes * LOG2E
    c1 = s2.astype(BF16).astype(F32)
    c2 = (s2 - c1).astype(BF16).astype(F32)
    c3 = (s2 - c1 - c2).astype(BF16).astype(F32)
    return jnp.stack([c1, c2, c3], axis=1).reshape(-1)


def _diff_attn(qt, k, vt, slopes, lam, subg, lambda_init):
    batch, heads, _, _, seq = qt.shape
    vdim = vt.shape[2]
    tq = DIFF_Q
    assert tq == 2 * 256 and DIFF_K == 256 and seq % tq == 0
    slopes = _slope_pieces(slopes)
    smem = pl.BlockSpec(memory_space=pltpu.SMEM)
    return pl.pallas_call(
        functools.partial(_diff_attn_kernel, lambda_init=lambda_init),
        grid=(batch, heads, seq // tq),
        in_specs=[smem,
                  pl.BlockSpec(lam.shape, lambda b, h, i: (0, 0)),
                  pl.BlockSpec(subg.shape, lambda b, h, i: (0, 0)),
                  pl.BlockSpec((None, None, 2, HEAD_DIM, tq), lambda b, h, i: (b, h, 0, 0, i)),
                  pl.BlockSpec((None, seq, 2 * HEAD_DIM), lambda b, h, i: (b, 0, h)),
                  pl.BlockSpec((None, None, vdim, seq), lambda b, h, i: (b, h, 0, 0))],
        out_specs=pl.BlockSpec((None, tq, vdim), lambda b, h, i: (b, i, h)),
        out_shape=jax.ShapeDtypeStruct((batch, seq, heads * vdim), BF16),
        scratch_shapes=[pltpu.VMEM((2, 1, tq), F32),
                        pltpu.VMEM((2, vdim + SUM_ROWS, tq), F32),
                        pltpu.VMEM((2, 2, 2 * HEAD_DIM + LANES, tq), BF16),
                        pltpu.VMEM((2, 2, DIFF_K, tq), F32),
                        pltpu.VMEM((tq // DIFF_K, DIFF_K, tq), F32),
                        pltpu.VMEM((DIFF_K, LANES), BF16),
                        pltpu.VMEM((2, 2, 1, LANES), F32)],
        compiler_params=_params("arbitrary", "arbitrary", "arbitrary"),
        name="diff_attn",
    )(slopes, lam, subg, qt, k, vt)


def _alibi_slopes(n_heads):
    h = jnp.arange(1, n_heads + 1, dtype=F32)
    return jnp.exp2(-8.0 * h / n_heads)


def _lambda_init(layer_idx):
    return 0.8 - 0.6 * math.exp(-0.3 * layer_idx)


def kernel(x, norm_mix, norm_ffn, swa_w_qkv, swa_w_o, swa_sink, diff_w_qkv, diff_w_o, diff_lambda,
           diff_subln, moe_w_group, moe_b_group, moe_w_router, moe_b_router, moe_w_gate, moe_w_up,
           moe_w_down, final_norm):
    batch, seq, d = x.shape
    n = batch * seq
    depth = norm_mix.shape[0]
    x2 = x.reshape(n, d)
    q_scale = HEAD_DIM ** -0.5 * LOG2E
    for layer in range(depth):
        j = layer // 2
        g_mix = norm_mix[layer][None, :]
        x3 = x2.reshape(batch, seq, d)
        if layer % 2 == 0:
            n_heads = SWA_KV_HEADS * SWA_GROUP
            qw = n_heads * HEAD_DIM
            kw = SWA_KV_HEADS * HEAD_DIM
            w = swa_w_qkv[j]
            wq = (w[:, :qw] * q_scale).astype(BF16)
            wk = w[:, qw:qw + kw].astype(BF16)
            wv = w[:, qw + kw:].astype(BF16)
            qt, k, vt = _qkv_proj(x3, g_mix, wq, wk, wv, "swa_proj")
            attn = _swa_attn(qt, k, vt, _alibi_slopes(n_heads), swa_sink[j].astype(F32))
            w_o = swa_w_o[j].astype(BF16)
        else:
            heads = d // (2 * HEAD_DIM)
            w = diff_w_qkv[j]
            wq = (w[:, :d] * q_scale).astype(BF16)
            wk = w[:, d:2 * d].astype(BF16)
            wv = w[:, 2 * d:].astype(BF16)
            qt, k, vt = _qkv_proj(x3, g_mix, wq, wk, wv, "diff_proj")
            qt = qt.reshape(batch, heads, 2, HEAD_DIM, seq)
            vt = vt.reshape(batch, heads, 2 * HEAD_DIM, seq)
            attn = _diff_attn(qt, k, vt, _alibi_slopes(heads), diff_lambda[j].astype(F32),
                              diff_subln[j].astype(F32)[:, None], _lambda_init(layer))
            attn = attn.reshape(n, d)
            w_o = diff_w_o[j].astype(BF16)
        last = layer == depth - 1
        x2 = _attn_out_moe(attn, w_o, x2, norm_ffn[layer][None, :], moe_w_group[layer], moe_b_group[layer],
                           moe_w_router[layer], moe_b_router[layer], moe_w_gate, moe_w_up, moe_w_down,
                           layer, final_norm[None, :], last)
    return x2.reshape(batch, seq, d)
```

```python
import functools
import math

import jax
import jax.numpy as jnp
from jax import lax
from jax.experimental import pallas as pl
from jax.experimental.pallas import tpu as pltpu

F32 = jnp.float32
BF16 = jnp.bfloat16

HEAD_DIM = 64
WINDOW = 128
SWA_KV_HEADS = 4
SWA_GROUP = 4
N_GROUPS = 4
EXPERTS_PER_GROUP = 8
N_EXPERTS = N_GROUPS * EXPERTS_PER_GROUP
NORM_EPS = 1e-6
SUBLN_EPS = 1e-5
NEG = -1e30
LANES = 128
LOG2E = math.log2(math.e)

PROJ_ROWS = 512
SWA_Q = 128
MOE_ROWS = 512
MOVE_ROWS = 512
ISSUE_UNROLL = 8
DIFF_Q = 512
DIFF_K = 256
DIFF_UNROLL = 4
SKIP_MARGIN = 160.0
NORM_SLACK = 1.0 + 2.0 ** -6
FAST_MARGIN = 64.0
SUM_ROWS = 16
VMEM_LIMIT = 56 * 1024 * 1024


def _params(*sem):
    return pltpu.CompilerParams(dimension_semantics=sem, vmem_limit_bytes=VMEM_LIMIT)


def _rms(x, g, eps):
    ms = jnp.mean(x * x, axis=-1, keepdims=True)
    return x * lax.rsqrt(ms + eps) * g


def _qkv_proj_kernel(x_ref, g_ref, wq_ref, wk_ref, wv_ref, qt_ref, k_ref, vt_ref, wqt_ref, wvt_ref):
    @pl.when((pl.program_id(0) == 0) & (pl.program_id(1) == 0))
    def _():
        wqt_ref[...] = wq_ref[...].T
        wvt_ref[...] = wv_ref[...].T

    xn = _rms(x_ref[...], g_ref[...], NORM_EPS).astype(BF16)
    nt = (((1,), (1,)), ((), ()))
    qt_ref[...] = lax.dot_general(wqt_ref[...], xn, nt, preferred_element_type=F32).astype(BF16)
    k_ref[...] = jnp.dot(xn, wk_ref[...], preferred_element_type=F32).astype(BF16)
    vt_ref[...] = lax.dot_general(wvt_ref[...], xn, nt, preferred_element_type=F32).astype(BF16)


def _qkv_proj(x3, g, wq, wk, wv, name):
    batch, seq, d = x3.shape
    rows = PROJ_ROWS
    qw, kw, vw = wq.shape[1], wk.shape[1], wv.shape[1]

    def full(w):
        return pl.BlockSpec(w.shape, lambda b, i: (0, 0))

    def transposed(width):
        return pl.BlockSpec((None, width, rows), lambda b, i: (b, 0, i))

    return pl.pallas_call(
        _qkv_proj_kernel,
        grid=(batch, seq // rows),
        in_specs=[pl.BlockSpec((None, rows, d), lambda b, i: (b, i, 0)),
                  pl.BlockSpec((1, d), lambda b, i: (0, 0)),
                  full(wq), full(wk), full(wv)],
        out_specs=[transposed(qw), pl.BlockSpec((None, rows, kw), lambda b, i: (b, i, 0)), transposed(vw)],
        out_shape=[jax.ShapeDtypeStruct((batch, qw, seq), BF16),
                   jax.ShapeDtypeStruct((batch, seq, kw), BF16),
                   jax.ShapeDtypeStruct((batch, vw, seq), BF16)],
        scratch_shapes=[pltpu.VMEM((qw, d), BF16), pltpu.VMEM((vw, d), BF16)],
        compiler_params=_params("arbitrary", "arbitrary"),
        name=name,
    )(x3, g, wq, wk, wv)


def _swa_attn_kernel(slope_ref, sink_ref, qt_ref, kp_ref, kc_ref, kn_ref, vp_ref, vc_ref, vn_ref, o_ref,
                     bias_ref, sinkrow_ref):
    first_step = (pl.program_id(0) == 0) & (pl.program_id(1) == 0)
    i = pl.program_id(1)
    last = pl.num_programs(1) - 1
    tq = SWA_Q
    nk = 3 * tq
    nl = SWA_GROUP * tq

    @pl.when(first_step)
    def _():
        row = lax.broadcasted_iota(jnp.int32, (nk, nl), 0)
        lane = lax.broadcasted_iota(jnp.int32, (nk, nl), 1)
        dist = jnp.abs(row - tq - (lane & (tq - 1)))
        distf = dist.astype(F32)
        lane1 = lax.broadcasted_iota(jnp.int32, (1, nl), 1)
        tq_bits = tq.bit_length() - 1
        for j in range(SWA_KV_HEADS):
            slope = jnp.zeros((nk, nl), F32)
            sink = jnp.zeros((1, nl), F32)
            for g in range(SWA_GROUP):
                h = j * SWA_GROUP + g
                slope = jnp.where((lane >> tq_bits) == g, slope_ref[h], slope)
                sink = jnp.where((lane1 >> tq_bits) == g, sink_ref[h], sink)
            sinkrow_ref[j] = sink
            bias = jnp.where(dist <= WINDOW, -slope * distf, NEG)
            bias_ref[0, j] = bias
            bias_ref[1, j] = jnp.where(row < tq, NEG, bias)
            bias_ref[2, j] = jnp.where(row >= 2 * tq, NEG, bias)

    edge = jnp.where(i == 0, 1, jnp.where(i == last, 2, 0))
    kwin = jnp.concatenate([kp_ref[...], kc_ref[...], kn_ref[...]], axis=0)
    kvw = kwin.shape[1]
    for j in range(SWA_KV_HEADS):
        rows = slice(j * HEAD_DIM, (j + 1) * HEAD_DIM)
        qj = jnp.concatenate([qt_ref[(j * SWA_GROUP + g) * HEAD_DIM:(j * SWA_GROUP + g + 1) * HEAD_DIM, :]
                              for g in range(SWA_GROUP)], axis=1)
        parts = []
        if j > 0:
            parts.append(jnp.zeros((j * HEAD_DIM, nl), BF16))
        parts.append(qj)
        if (j + 1) * HEAD_DIM < kvw:
            parts.append(jnp.zeros((kvw - (j + 1) * HEAD_DIM, nl), BF16))
        qpad = jnp.concatenate(parts, axis=0)
        s = jnp.dot(kwin, qpad, preferred_element_type=F32) + bias_ref[edge, j]
        sink = sinkrow_ref[j]
        m = jnp.maximum(jnp.max(s, axis=0, keepdims=True), sink)
        p = jnp.exp2(s - m)
        vwin = jnp.concatenate([vp_ref[rows, :], vc_ref[rows, :], vn_ref[rows, :]], axis=1)
        vwin = jnp.concatenate([vwin, jnp.ones((SUM_ROWS, nk), BF16)], axis=0)
        pv = jnp.dot(vwin, p.astype(BF16), preferred_element_type=F32)
        denom = pv[HEAD_DIM:HEAD_DIM + 1] + jnp.exp2(sink - m)
        ot = pv[:HEAD_DIM] / denom
        for g in range(SWA_GROUP):
            h = j * SWA_GROUP + g
            o_ref[:, h * HEAD_DIM:(h + 1) * HEAD_DIM] = ot[:, g * tq:(g + 1) * tq].T.astype(BF16)


def _swa_attn(qt, k, vt, slopes, sink):
    batch, qw, seq = qt.shape
    kvw = k.shape[2]
    nb = seq // SWA_Q
    smem = pl.BlockSpec(memory_space=pltpu.SMEM)

    def k_spec(shift):
        return pl.BlockSpec((None, SWA_Q, kvw), lambda b, i: (b, jnp.clip(i + shift, 0, nb - 1), 0))

    def v_spec(shift):
        return pl.BlockSpec((None, kvw, SWA_Q), lambda b, i: (b, 0, jnp.clip(i + shift, 0, nb - 1)))

    out = pl.pallas_call(
        _swa_attn_kernel,
        grid=(batch, nb),
        in_specs=[smem, smem,
                  pl.BlockSpec((None, qw, SWA_Q), lambda b, i: (b, 0, i)),
                  k_spec(-1), k_spec(0), k_spec(1), v_spec(-1), v_spec(0), v_spec(1)],
        out_specs=pl.BlockSpec((None, SWA_Q, qw), lambda b, i: (b, i, 0)),
        out_shape=jax.ShapeDtypeStruct((batch, seq, qw), BF16),
        scratch_shapes=[pltpu.VMEM((3, SWA_KV_HEADS, 3 * SWA_Q, SWA_GROUP * SWA_Q), F32),
                        pltpu.VMEM((SWA_KV_HEADS, 1, SWA_GROUP * SWA_Q), F32)],
        compiler_params=_params("arbitrary", "arbitrary"),
        name="swa_attn",
    )(slopes * LOG2E, sink * LOG2E, qt, k, k, k, vt, vt, vt)
    return out.reshape(batch * seq, qw)


def _router_kernel(a_ref, wo_ref, x_ref, g_ref, w_ref, b_ref, xo_ref, meta_ref, dest_ref, plan_ref,
                   carry_ref, er_ref):
    t = pl.program_id(0)
    rows = x_ref.shape[0]

    @pl.when(t == 0)
    def _():
        carry_ref[...] = jnp.zeros_like(carry_ref)

    x = x_ref[...] + jnp.dot(a_ref[...], wo_ref[...], preferred_element_type=F32)
    xo_ref[...] = x
    xn = _rms(x, g_ref[...], NORM_EPS).astype(BF16)
    logits = jnp.dot(xn, w_ref[...], preferred_element_type=F32) + b_ref[...]
    lane = lax.broadcasted_iota(jnp.int32, (rows, LANES), 1)

    is_group = lane < N_GROUPS
    gl = jnp.where(is_group, logits, NEG)
    gmax = jnp.max(gl, axis=-1, keepdims=True)
    gidx = jnp.min(jnp.where(gl == gmax, lane, LANES), axis=-1, keepdims=True)
    zg = jnp.sum(jnp.where(is_group, jnp.exp(gl - gmax), 0.0), axis=-1, keepdims=True)
    pg = 1.0 / zg

    lo = N_GROUPS + EXPERTS_PER_GROUP * gidx
    in_group = (lane >= lo) & (lane < lo + EXPERTS_PER_GROUP)
    el = jnp.where(in_group, logits, NEG)
    m0 = jnp.max(el, axis=-1, keepdims=True)
    i0 = jnp.min(jnp.where(el == m0, lane, LANES), axis=-1, keepdims=True)
    el1 = jnp.where(lane == i0, NEG, el)
    m1 = jnp.max(el1, axis=-1, keepdims=True)
    i1 = jnp.min(jnp.where(el1 == m1, lane, LANES), axis=-1, keepdims=True)
    r = jnp.exp(m1 - m0)
    w0 = pg / (1.0 + r)
    w1 = pg * r / (1.0 + r)
    e0 = i0 - N_GROUPS
    e1 = i1 - N_GROUPS

    sel0 = lane == e0
    sel1 = lane == e1
    onehot = (sel0 | sel1).astype(BF16)
    rr = lax.broadcasted_iota(jnp.int32, (rows, rows), 0)
    cc = lax.broadcasted_iota(jnp.int32, (rows, rows), 1)
    tri = (cc < rr).astype(BF16)
    before = jnp.dot(tri, onehot, preferred_element_type=F32) + carry_ref[...]
    r0 = jnp.sum(jnp.where(sel0, before, 0.0), axis=-1, keepdims=True)
    r1 = jnp.sum(jnp.where(sel1, before, 0.0), axis=-1, keepdims=True)
    carry_ref[...] = carry_ref[...] + jnp.sum(onehot.astype(F32), axis=0, keepdims=True)

    meta = jnp.where(lane == 0, e0.astype(F32), 0.0)
    meta = jnp.where(lane == 1, e1.astype(F32), meta)
    meta = jnp.where(lane == 2, r0, meta)
    meta = jnp.where(lane == 3, r1, meta)
    meta = jnp.where(lane == 4, w0, meta)
    meta = jnp.where(lane == 5, w1, meta)
    meta_ref[...] = meta
    metat = meta.T
    for f in range(4):
        er_ref[t, :, f * rows:(f + 1) * rows] = metat[f:f + 1, :].astype(jnp.int32)

    @pl.when(t == pl.num_programs(0) - 1)
    def _plan():
        counts = jnp.broadcast_to(carry_ref[...], (8, LANES))
        padded = jnp.floor((counts + (MOE_ROWS - 1)) * (1.0 / MOE_ROWS)) * MOE_ROWS
        lane8 = lax.broadcasted_iota(jnp.int32, (8, LANES), 1)
        pend = padded
        shift = 1
        while shift < LANES:
            pend = pend + jnp.where(lane8 >= shift, pltpu.roll(pend, shift, 1), 0.0)
            shift *= 2
        starts_counts = jnp.where(lane8 < N_EXPERTS, pend - padded, pltpu.roll(counts, N_EXPERTS, 1))
        plan_ref[:, 0:LANES] = starts_counts[0:1].astype(jnp.int32)
        sq_r = lax.broadcasted_iota(jnp.int32, (LANES, LANES), 0)
        sq_c = lax.broadcasted_iota(jnp.int32, (LANES, LANES), 1)
        pend_col = jnp.sum(jnp.where(sq_r == sq_c, jnp.broadcast_to(pend[0:1], (LANES, LANES)), 0.0),
                           axis=1, keepdims=True)
        nbp = plan_ref.shape[1] - LANES
        blk_row = lax.broadcasted_iota(jnp.int32, (LANES, nbp), 0)
        blk_start = (lax.broadcasted_iota(jnp.int32, (LANES, nbp), 1) * MOE_ROWS).astype(F32)
        ended = (pend_col <= blk_start) & (blk_row < N_EXPERTS)
        block_e = jnp.minimum(jnp.sum(ended.astype(F32), axis=0, keepdims=True), N_EXPERTS - 1.0)
        n_used = jnp.sum(jnp.where(lane8[0:1] == N_EXPERTS - 1, pend[0:1], 0.0), axis=1,
                         keepdims=True) * (1.0 / MOE_ROWS)
        blk_lane = lax.broadcasted_iota(jnp.int32, (1, nbp), 1)
        plan_ref[:, LANES:] = jnp.where(blk_lane == nbp - 1, n_used, block_e).astype(jnp.int32)

        starts = (pend - padded)[0:1]
        start_of = [jnp.sum(jnp.where(lane8[0:1] == k, starts, 0.0), axis=1, keepdims=True)
                    for k in range(N_EXPERTS)]

        def tile_dest(tt, carry):
            er = er_ref[tt]
            for slot in range(2):
                e = er[:, slot * rows:(slot + 1) * rows]
                rank = er[:, (2 + slot) * rows:(3 + slot) * rows]
                base = jnp.zeros((1, rows), F32)
                for k in range(N_EXPERTS):
                    base = jnp.where(e == k, start_of[k], base)
                dest_ref[tt, :, slot * rows:(slot + 1) * rows] = base.astype(jnp.int32) + rank
            return carry

        lax.fori_loop(0, pl.num_programs(0), tile_dest, 0)


def _plan_width(n):
    n_blocks = (2 * n + N_EXPERTS * MOE_ROWS) // MOE_ROWS
    return LANES + (n_blocks // LANES + 1) * LANES


def _router(attn, w_o, x2, g, w_pad, b_pad):
    n, d = x2.shape
    rows = MOVE_ROWS
    tiles = n // rows
    pw = _plan_width(n)
    return pl.pallas_call(
        _router_kernel,
        grid=(tiles,),
        in_specs=[pl.BlockSpec((rows, attn.shape[1]), lambda t: (t, 0)),
                  pl.BlockSpec(w_o.shape, lambda t: (0, 0)),
                  pl.BlockSpec((rows, d), lambda t: (t, 0)),
                  pl.BlockSpec((1, d), lambda t: (0, 0)),
                  pl.BlockSpec((d, LANES), lambda t: (0, 0)),
                  pl.BlockSpec((1, LANES), lambda t: (0, 0))],
        out_specs=[pl.BlockSpec((rows, d), lambda t: (t, 0)),
                   pl.BlockSpec((rows, LANES), lambda t: (t, 0)),
                   pl.BlockSpec((tiles, 1, 2 * rows), lambda t: (0, 0, 0)),
                   pl.BlockSpec((1, pw), lambda t: (0, 0))],
        out_shape=[jax.ShapeDtypeStruct((n, d), F32),
                   jax.ShapeDtypeStruct((n, LANES), F32),
                   jax.ShapeDtypeStruct((tiles, 1, 2 * rows), jnp.int32),
                   jax.ShapeDtypeStruct((1, pw), jnp.int32)],
        scratch_shapes=[pltpu.VMEM((1, LANES), F32), pltpu.VMEM((tiles, 1, 4 * rows), jnp.int32)],
        compiler_params=_params("arbitrary"),
        name="router",
    )(attn, w_o, x2, g, w_pad, b_pad)


def _row_copy(src, src_row, dst, dst_row, sem):
    return pltpu.make_async_copy(src.at[pl.ds(src_row, 1)], dst.at[pl.ds(dst_row, 1)], sem)


def _dest_rows(dest_ref, r, rows):
    return dest_ref[0, 0, r], dest_ref[0, 0, rows + r]


def _dispatch_kernel(dest_ref, plan_ref, x_ref, xs_hbm, zero_ref, sem, zsem):
    rows = x_ref.shape[0]

    @pl.when(pl.program_id(0) == 0)
    def _():
        zero_ref[...] = jnp.zeros_like(zero_ref)

        def fill_expert(e, total):
            count = plan_ref[N_EXPERTS + e]
            first = plan_ref[e] + count
            pad = (-count) & (MOE_ROWS - 1)

            def fill(r, c):
                _row_copy(zero_ref, 0, xs_hbm, first + r, zsem).start()
                return c

            def fill_many(q, c):
                for u in range(ISSUE_UNROLL):
                    _row_copy(zero_ref, 0, xs_hbm, first + q * ISSUE_UNROLL + u, zsem).start(priority=u % 2)
                return c

            bulk = pad // ISSUE_UNROLL
            lax.fori_loop(0, bulk, fill_many, 0)
            lax.fori_loop(bulk * ISSUE_UNROLL, pad, fill, 0)
            return total + pad

        total = lax.fori_loop(0, N_EXPERTS, fill_expert, 0)

        def block_copy(b):
            start = pl.multiple_of(b * MOE_ROWS, MOE_ROWS)
            return pltpu.make_async_copy(zero_ref, xs_hbm.at[pl.ds(start, MOE_ROWS)], zsem)

        def drain_rows(r, c):
            block_copy(0).wait()
            return c

        def drain(r, c):
            _row_copy(zero_ref, 0, xs_hbm, 0, zsem).wait()
            return c

        whole = total // MOE_ROWS
        lax.fori_loop(0, whole, drain_rows, 0)
        lax.fori_loop(whole * MOE_ROWS, total, drain, 0)

        n_used = plan_ref[plan_ref.shape[0] - 1]
        n_blocks = xs_hbm.shape[0] // MOE_ROWS

        def fill_block(b, c):
            block_copy(b).start()
            return c

        def drain_block(b, c):
            block_copy(b).wait()
            return c

        lax.fori_loop(n_used, n_blocks, fill_block, 0)
        lax.fori_loop(n_used, n_blocks, drain_block, 0)

    def start(r, c):
        d0, d1 = _dest_rows(dest_ref, r, rows)
        _row_copy(x_ref, r, xs_hbm, d0, sem).start()
        _row_copy(x_ref, r, xs_hbm, d1, sem).start(priority=1)
        return c

    lax.fori_loop(0, rows, start, 0, unroll=ISSUE_UNROLL)
    for _ in range(2):
        pltpu.make_async_copy(x_ref, xs_hbm.at[pl.ds(0, rows)], sem).wait()


def _dispatch(x2, dest, plan, cap):
    n, d = x2.shape
    rows = MOVE_ROWS
    any_spec = pl.BlockSpec(memory_space=pl.ANY)
    smem = pl.BlockSpec(memory_space=pltpu.SMEM)
    return pl.pallas_call(
        _dispatch_kernel,
        grid=(n // rows,),
        in_specs=[pl.BlockSpec((1, 1, 2 * rows), lambda t: (t, 0, 0), memory_space=pltpu.SMEM),
                  smem,
                  pl.BlockSpec((rows, d), lambda t: (t, 0))],
        out_specs=any_spec,
        out_shape=jax.ShapeDtypeStruct((cap, d), F32),
        scratch_shapes=[pltpu.VMEM((MOE_ROWS, d), F32), pltpu.SemaphoreType.DMA(()),
                        pltpu.SemaphoreType.DMA(())],
        compiler_params=pltpu.CompilerParams(dimension_semantics=("arbitrary",),
                                             has_side_effects=True, vmem_limit_bytes=VMEM_LIMIT),
        name="moe_dispatch",
    )(dest, plan, x2)


def _expert_kernel(plan_ref, xs_ref, g_ref, wg_ref, wu_ref, wd_ref, ys_ref, wgb, wub, wdb):
    b = pl.program_id(0)
    e = plan_ref[LANES + b]
    prev = plan_ref[LANES + jnp.maximum(b - 1, 0)]
    n_used = plan_ref[plan_ref.shape[0] - 1]

    @pl.when((b == 0) | (e != prev))
    def _():
        wgb[...] = wg_ref[...].astype(BF16)
        wub[...] = wu_ref[...].astype(BF16)
        wdb[...] = wd_ref[...].astype(BF16)

    @pl.when(b < n_used)
    def _():
        xn = _rms(xs_ref[...], g_ref[...], NORM_EPS).astype(BF16)
        hg = jnp.dot(xn, wgb[...], preferred_element_type=F32)
        hu = jnp.dot(xn, wub[...], preferred_element_type=F32)
        hid = (hg / (1.0 + jnp.exp(-hg))) * hu
        ys_ref[...] = jnp.dot(hid.astype(BF16), wdb[...], preferred_element_type=F32)

    @pl.when(b >= n_used)
    def _():
        ys_ref[...] = jnp.zeros_like(ys_ref)


def _experts(xs, g, w_gate, w_up, w_down, plan, layer):
    cap, d = xs.shape
    de = w_gate.shape[-1]
    rows = MOE_ROWS

    def expert_block(b, plan):
        return (layer, plan[LANES + b], 0, 0)

    grid_spec = pltpu.PrefetchScalarGridSpec(
        num_scalar_prefetch=1,
        grid=(cap // rows,),
        in_specs=[pl.BlockSpec((rows, d), lambda b, plan: (jnp.where(b < plan[plan.shape[0] - 1], b, 0), 0)),
                  pl.BlockSpec((1, d), lambda b, plan: (0, 0)),
                  pl.BlockSpec((None, None, d, de), expert_block),
                  pl.BlockSpec((None, None, d, de), expert_block),
                  pl.BlockSpec((None, None, de, d), expert_block)],
        out_specs=pl.BlockSpec((rows, d), lambda b, plan: (b, 0)),
        scratch_shapes=[pltpu.VMEM((d, de), BF16), pltpu.VMEM((d, de), BF16), pltpu.VMEM((de, d), BF16)],
    )
    return pl.pallas_call(
        _expert_kernel,
        grid_spec=grid_spec,
        out_shape=jax.ShapeDtypeStruct((cap, d), F32),
        compiler_params=_params("arbitrary"),
        name="moe_experts",
    )(plan, xs, g, w_gate, w_up, w_down)


def _combine_kernel(dest_ref, x_ref, meta_ref, g_ref, ys_hbm, o_ref, ybuf, sem, *, final_norm):
    rows = x_ref.shape[0]

    def start(r, c):
        d0, d1 = _dest_rows(dest_ref, r, rows)
        _row_copy(ys_hbm, d0, ybuf.at[0], r, sem).start()
        _row_copy(ys_hbm, d1, ybuf.at[1], r, sem).start(priority=1)
        return c

    lax.fori_loop(0, rows, start, 0, unroll=ISSUE_UNROLL)
    for k in range(2):
        pltpu.make_async_copy(ys_hbm.at[pl.ds(0, rows)], ybuf.at[k], sem).wait()
    meta = meta_ref[...]
    out = x_ref[...] + (meta[:, 4:5] * ybuf[0] + meta[:, 5:6] * ybuf[1])
    if final_norm:
        out = _rms(out, g_ref[...], NORM_EPS)
    o_ref[...] = out


def _combine(x2, meta, dest, ys, g, final_norm):
    n, d = x2.shape
    rows = MOVE_ROWS
    return pl.pallas_call(
        functools.partial(_combine_kernel, final_norm=final_norm),
        grid=(n // rows,),
        in_specs=[pl.BlockSpec((1, 1, 2 * rows), lambda t: (t, 0, 0), memory_space=pltpu.SMEM),
                  pl.BlockSpec((rows, d), lambda t: (t, 0)),
                  pl.BlockSpec((rows, LANES), lambda t: (t, 0)),
                  pl.BlockSpec((1, d), lambda t: (0, 0)),
                  pl.BlockSpec(memory_space=pl.ANY)],
        out_specs=pl.BlockSpec((rows, d), lambda t: (t, 0)),
        out_shape=jax.ShapeDtypeStruct((n, d), F32),
        scratch_shapes=[pltpu.VMEM((2, rows, d), F32), pltpu.SemaphoreType.DMA(())],
        compiler_params=_params("arbitrary"),
        name="moe_combine",
    )(dest, x2, meta, g, ys)


def _attn_out_moe(attn, w_o, x2, g, w_group, b_group, w_router, b_router, w_gate, w_up, w_down, layer,
                  g_final, final_norm):
    n, d = x2.shape
    pad = LANES - N_GROUPS - N_EXPERTS
    w_pad = jnp.concatenate([w_group, w_router, jnp.zeros((d, pad), F32)], axis=1).astype(BF16)
    b_pad = jnp.concatenate([b_group, b_router, jnp.zeros((pad,), F32)])[None, :]
    x2, meta, dest, plan = _router(attn, w_o, x2, g, w_pad, b_pad)
    plan = plan.reshape(-1)
    cap = 2 * n + N_EXPERTS * MOE_ROWS
    xs = _dispatch(x2, dest, plan, cap)
    ys = _experts(xs, g, w_gate, w_up, w_down, plan, layer)
    return _combine(x2, meta, dest, ys, g_final, final_norm)


def _diff_attn_kernel(cp_ref, lam_ref, subg_ref, qt_ref, k_ref, vt_ref, o_ref,
                      m_ref, acc_ref, qx_ref, s_ref, fix_ref, kaug_ref, km_ref, *, lambda_init):
    h = pl.program_id(1)
    i = pl.program_id(2)
    tq = qt_ref.shape[-1]
    tk = DIFF_K
    per_q = tq // tk
    n_groups = k_ref.shape[0] // tq
    cp = [cp_ref[3 * h + p] for p in range(3)]
    slope2 = cp[0] + cp[1] + cp[2]

    @pl.when(i == 0)
    def _():
        kl = lax.broadcasted_iota(jnp.int32, (tk, LANES), 1)
        kr = lax.broadcasted_iota(jnp.int32, (tk, LANES), 0).astype(F32)
        qr = lax.broadcasted_iota(jnp.int32, (LANES, tq), 0)
        ql = lax.broadcasted_iota(jnp.int32, (LANES, tq), 1)
        di_lo = (ql & 255).astype(F32)
        di_hi = (ql >> 8).astype(F32)
        kaug = jnp.zeros((tk, LANES), F32)
        qaug = jnp.zeros((LANES, tq), F32)
        for p in range(3):
            kaug = jnp.where(kl == p, -cp[p], kaug)
            kaug = jnp.where(kl == 3 + p, -256.0 * cp[p], kaug)
            kaug = jnp.where(kl == 6 + p, kr, kaug)
            qaug = jnp.where(qr == p, di_lo, qaug)
            qaug = jnp.where(qr == 3 + p, di_hi, qaug)
            qaug = jnp.where(qr == 6 + p, cp[p], qaug)
        kaug_ref[...] = kaug.astype(BF16)
        for c in range(2):
            qx_ref[0, c, 2 * HEAD_DIM:, :] = qaug.astype(BF16)
            qx_ref[1, c, 2 * HEAD_DIM:, :] = (-qaug).astype(BF16)
        for u in range(per_q):
            rel = (lax.broadcasted_iota(jnp.int32, (tk, tq), 0)
                   - lax.broadcasted_iota(jnp.int32, (tk, tq), 1)) + u * tk
            fix_ref[u] = (2.0 * slope2) * jnp.maximum(rel, 0).astype(F32)
        lane_k = lax.broadcasted_iota(jnp.int32, (1, LANES), 1)
        sel_r = lax.broadcasted_iota(jnp.int32, (LANES, LANES), 0)
        sel_c = lax.broadcasted_iota(jnp.int32, (LANES, LANES), 1)
        selector = ((sel_r >= sel_c * HEAD_DIM) & (sel_r < (sel_c + 1) * HEAD_DIM)).astype(BF16)
        group_norm = []
        for g in range(n_groups):
            kf = k_ref[g * tq:(g + 1) * tq, :].astype(F32)
            norms2 = jnp.dot((kf * kf).astype(BF16), selector, preferred_element_type=F32)
            top = jnp.max(norms2, axis=0, keepdims=True)
            group_norm.append([jnp.sqrt(jnp.sum(jnp.where(lane_k == c, top, 0.0), axis=1, keepdims=True))
                               * NORM_SLACK for c in range(2)])
        for c in range(2):
            running = jnp.zeros((1, 1), F32)
            row = jnp.zeros((1, LANES), F32)
            for g in range(n_groups):
                running = jnp.maximum(running, group_norm[g][c])
                row = jnp.where(lane_k == g, running, row)
            km_ref[0, c] = row
            running = jnp.zeros((1, 1), F32)
            row = jnp.zeros((1, LANES), F32)
            for g in reversed(range(n_groups)):
                running = jnp.maximum(running, group_norm[g][c])
                row = jnp.where(lane_k == g, running, row)
            km_ref[1, c] = row

    zeros = jnp.zeros((HEAD_DIM, tq), BF16)
    q_maps = (jnp.concatenate([qt_ref[0], zeros], axis=0),
              jnp.concatenate([zeros, qt_ref[1]], axis=0))
    for c in range(2):
        qx_ref[0, c, :2 * HEAD_DIM, :] = q_maps[c]
        qx_ref[1, c, :2 * HEAD_DIM, :] = q_maps[c]
    m_ref[...] = jnp.full_like(m_ref, NEG)
    acc_ref[...] = jnp.zeros_like(acc_ref)
    kaug = kaug_ref[...]

    n_k = n_groups * per_q

    def qk_stage(j, slot):
        jc = jnp.clip(j, 0, n_k - 1)
        start = pl.multiple_of(jc * tk, tk)
        kb = jnp.concatenate([k_ref[pl.ds(start, tk), :], kaug], axis=1)
        after = (jc >= (i + 1) * per_q).astype(jnp.int32)
        for c in range(2):
            s_ref[slot, c] = jnp.dot(kb, qx_ref[after, c], preferred_element_type=F32)

    ones = jnp.ones((SUM_ROWS, tk), BF16)

    def softmax_stage(j, slot, after, fix):
        start = pl.multiple_of(j * tk, tk)
        vb = jnp.concatenate([vt_ref[:, pl.ds(start, tk)], ones], axis=0)
        delta = (i * tq - start).astype(F32)
        off = slope2 * (-delta if after else delta)
        for c in range(2):
            s = s_ref[slot, c]
            if fix is not None:
                s = s - fix_ref[fix]
            m_old = m_ref[c]
            m_new = jnp.maximum(m_old, jnp.max(s, axis=0, keepdims=True) - off)
            alpha = jnp.exp2(m_old - m_new)
            p = jnp.exp2(s - (m_new + off))
            acc_ref[c] = alpha * acc_ref[c] + jnp.dot(vb, p.astype(BF16), preferred_element_type=F32)
            m_ref[c] = m_new

    def fixed_max_stage(j, slot, after):
        start = pl.multiple_of(j * tk, tk)
        vb = jnp.concatenate([vt_ref[:, pl.ds(start, tk)], ones], axis=0)
        delta = (i * tq - start).astype(F32)
        off = slope2 * (-delta if after else delta)
        for c in range(2):
            p = jnp.exp2(s_ref[slot, c] - (m_ref[c] + off))
            acc_ref[c] = acc_ref[c] + jnp.dot(vb, p.astype(BF16), preferred_element_type=F32)

    def pair(j0, next_j0, after, diagonal, fixed_max=False):
        def stage(j, slot, fix):
            if fixed_max:
                fixed_max_stage(j, slot, after)
            else:
                softmax_stage(j, slot, after, fix)
        qk_stage(j0 + 1, 1)
        stage(j0, 0, 0 if diagonal else None)
        qk_stage(next_j0, 0)
        stage(j0 + 1, 1, 1 if diagonal else None)

    def run_groups(first, step, count, after, then_group, fixed_max):
        def body(unroll, base):
            def run(t, carry):
                for u in range(unroll):
                    done = base + t * unroll + u
                    g = first + step * done
                    nxt = jnp.where(done == count - 1, then_group, g + step)
                    pair(g * per_q, nxt * per_q, after, False, fixed_max)
                return carry
            return run
        n_main = count // DIFF_UNROLL
        lax.fori_loop(0, n_main, body(DIFF_UNROLL, 0), 0)
        lax.fori_loop(0, count - n_main * DIFF_UNROLL, body(1, n_main * DIFF_UNROLL), 0)

    qk_stage(i * per_q, 0)
    pair(i * per_q, (i - 1) * per_q, False, True)

    qn = [jnp.sqrt(jnp.sum(jnp.square(qt_ref[c].astype(F32)), axis=0, keepdims=True)) for c in range(2)]

    step_t = lax.broadcasted_iota(jnp.int32, (n_groups, 1), 0)
    step_grid = lax.broadcasted_iota(jnp.int32, (n_groups, LANES), 0)
    lane_grid = lax.broadcasted_iota(jnp.int32, (n_groups, LANES), 1)
    penalty = slope2 * (step_t * tq + 1).astype(F32)

    def groups_needed(side, available):
        group_of_step = (i - 1 - step_grid) if side == 0 else (i + 1 + step_grid)
        pick = lane_grid == group_of_step
        need = None
        excess = jnp.full((1, tq), NEG, F32)
        for c in range(2):
            km = jnp.sum(jnp.where(pick, km_ref[side, c], 0.0), axis=1, keepdims=True)
            bound = qn[c] * km - penalty
            reach = bound >= (m_ref[c] - SKIP_MARGIN)
            need = reach if need is None else (need | reach)
            excess = jnp.maximum(excess, bound[0:1] - m_ref[c])
        cnt = jnp.sum(jnp.where(need & (step_t < available), 1.0, 0.0), axis=0, keepdims=True)
        return jnp.max(cnt).astype(jnp.int32), jnp.max(excess) <= FAST_MARGIN

    n_before, calm_before = groups_needed(0, i)
    n_after, calm_after = groups_needed(1, n_groups - 1 - i)

    def run_side(first, step, count, after, then_group, calm):
        @pl.when(calm)
        def _():
            run_groups(first, step, count, after, then_group, True)

        @pl.when(jnp.logical_not(calm))
        def _():
            run_groups(first, step, count, after, then_group, False)

    run_side(i - 1, -1, n_before, False, i + 1, calm_before)

    @pl.when((n_before == 0) & (n_after > 0))
    def _():
        qk_stage((i + 1) * per_q, 0)

    run_side(i + 1, 1, n_after, True, i + 1, calm_after)

    lam = lam_ref[...]
    lam_full = (jnp.exp(jnp.sum(lam[0:1] * lam[1:2], axis=-1, keepdims=True))
                - jnp.exp(jnp.sum(lam[2:3] * lam[3:4], axis=-1, keepdims=True)) + lambda_init)
    vdim = vt_ref.shape[0]
    o = (acc_ref[0, :vdim] / acc_ref[0, vdim:vdim + 1]
         - lam_full * (acc_ref[1, :vdim] / acc_ref[1, vdim:vdim + 1]))
    ms = jnp.mean(o * o, axis=0, keepdims=True)
    on = o * lax.rsqrt(ms + SUBLN_EPS) * subg_ref[...] * (1.0 - lambda_init)
    o_ref[...] = on.T.astype(BF16)


def _slope_pieces(slopes):
    s2 = slopes * LOG2E
    c1 = s2.astype(BF16).astype(F32)
    c2 = (s2 - c1).astype(BF16).astype(F32)
    c3 = (s2 - c1 - c2).astype(BF16).astype(F32)
    return jnp.stack([c1, c2, c3], axis=1).reshape(-1)


def _diff_attn(qt, k, vt, slopes, lam, subg, lambda_init):
    batch, heads, _, _, seq = qt.shape
    vdim = vt.shape[2]
    tq = DIFF_Q
    assert tq == 2 * 256 and DIFF_K == 256 and seq % tq == 0
    slopes = _slope_pieces(slopes)
    smem = pl.BlockSpec(memory_space=pltpu.SMEM)
    return pl.pallas_call(
        functools.partial(_diff_attn_kernel, lambda_init=lambda_init),
        grid=(batch, heads, seq // tq),
        in_specs=[smem,
                  pl.BlockSpec(lam.shape, lambda b, h, i: (0, 0)),
                  pl.BlockSpec(subg.shape, lambda b, h, i: (0, 0)),
                  pl.BlockSpec((None, None, 2, HEAD_DIM, tq), lambda b, h, i: (b, h, 0, 0, i)),
                  pl.BlockSpec((None, seq, 2 * HEAD_DIM), lambda b, h, i: (b, 0, h)),
                  pl.BlockSpec((None, None, vdim, seq), lambda b, h, i: (b, h, 0, 0))],
        out_specs=pl.BlockSpec((None, tq, vdim), lambda b, h, i: (b, i, h)),
        out_shape=jax.ShapeDtypeStruct((batch, seq, heads * vdim), BF16),
        scratch_shapes=[pltpu.VMEM((2, 1, tq), F32),
                        pltpu.VMEM((2, vdim + SUM_ROWS, tq), F32),
                        pltpu.VMEM((2, 2, 2 * HEAD_DIM + LANES, tq), BF16),
                        pltpu.VMEM((2, 2, DIFF_K, tq), F32),
                        pltpu.VMEM((tq // DIFF_K, DIFF_K, tq), F32),
                        pltpu.VMEM((DIFF_K, LANES), BF16),
                        pltpu.VMEM((2, 2, 1, LANES), F32)],
        compiler_params=_params("arbitrary", "arbitrary", "arbitrary"),
        name="diff_attn",
    )(slopes, lam, subg, qt, k, vt)


def _alibi_slopes(n_heads):
    h = jnp.arange(1, n_heads + 1, dtype=F32)
    return jnp.exp2(-8.0 * h / n_heads)


def _lambda_init(layer_idx):
    return 0.8 - 0.6 * math.exp(-0.3 * layer_idx)


def kernel(x, norm_mix, norm_ffn, swa_w_qkv, swa_w_o, swa_sink, diff_w_qkv, diff_w_o, diff_lambda,
           diff_subln, moe_w_group, moe_b_group, moe_w_router, moe_b_router, moe_w_gate, moe_w_up,
           moe_w_down, final_norm):
    batch, seq, d = x.shape
    n = batch * seq
    depth = norm_mix.shape[0]
    x2 = x.reshape(n, d)
    q_scale = HEAD_DIM ** -0.5 * LOG2E
    for layer in range(depth):
        j = layer // 2
        g_mix = norm_mix[layer][None, :]
        x3 = x2.reshape(batch, seq, d)
        if layer % 2 == 0:
            n_heads = SWA_KV_HEADS * SWA_GROUP
            qw = n_heads * HEAD_DIM
            kw = SWA_KV_HEADS * HEAD_DIM
            w = swa_w_qkv[j]
            wq = (w[:, :qw] * q_scale).astype(BF16)
            wk = w[:, qw:qw + kw].astype(BF16)
            wv = w[:, qw + kw:].astype(BF16)
            qt, k, vt = _qkv_proj(x3, g_mix, wq, wk, wv, "swa_proj")
            attn = _swa_attn(qt, k, vt, _alibi_slopes(n_heads), swa_sink[j].astype(F32))
            w_o = swa_w_o[j].astype(BF16)
        else:
            heads = d // (2 * HEAD_DIM)
            w = diff_w_qkv[j]
            wq = (w[:, :d] * q_scale).astype(BF16)
            wk = w[:, d:2 * d].astype(BF16)
            wv = w[:, 2 * d:].astype(BF16)
            qt, k, vt = _qkv_proj(x3, g_mix, wq, wk, wv, "diff_proj")
            qt = qt.reshape(batch, heads, 2, HEAD_DIM, seq)
            vt = vt.reshape(batch, heads, 2 * HEAD_DIM, seq)
            attn = _diff_attn(qt, k, vt, _alibi_slopes(heads), diff_lambda[j].astype(F32),
                              diff_subln[j].astype(F32)[:, None], _lambda_init(layer))
            attn = attn.reshape(n, d)
            w_o = diff_w_o[j].astype(BF16)
        last = layer == depth - 1
        x2 = _attn_out_moe(attn, w_o, x2, norm_ffn[layer][None, :], moe_w_group[layer], moe_b_group[layer],
                           moe_w_router[layer], moe_b_router[layer], moe_w_gate, moe_w_up, moe_w_down,
                           layer, final_norm[None, :], last)
    return x2.reshape(batch, seq, d)
```

```python
import functools
import math

import jax
import jax.numpy as jnp
from jax import lax
from jax.experimental import pallas as pl
from jax.experimental.pallas import tpu as pltpu

F32 = jnp.float32
BF16 = jnp.bfloat16

HEAD_DIM = 64
WINDOW = 128
SWA_KV_HEADS = 4
SWA_GROUP = 4
N_GROUPS = 4
EXPERTS_PER_GROUP = 8
N_EXPERTS = N_GROUPS * EXPERTS_PER_GROUP
NORM_EPS = 1e-6
SUBLN_EPS = 1e-5
NEG = -1e30
LANES = 128
LOG2E = math.log2(math.e)

PROJ_ROWS = 512
SWA_Q = 128
MOE_ROWS = 512
MOVE_ROWS = 512
ISSUE_UNROLL = 8
DIFF_Q = 512
DIFF_K = 256
DIFF_UNROLL = 4
SKIP_MARGIN = 152.0
NORM_SLACK = 1.0 + 2.0 ** -6
FAST_MARGIN = 64.0
SUM_ROWS = 16
VMEM_LIMIT = 56 * 1024 * 1024


def _params(*sem):
    return pltpu.CompilerParams(dimension_semantics=sem, vmem_limit_bytes=VMEM_LIMIT)


def _rms(x, g, eps):
    ms = jnp.mean(x * x, axis=-1, keepdims=True)
    return x * lax.rsqrt(ms + eps) * g


def _qkv_proj_kernel(x_ref, g_ref, wq_ref, wk_ref, wv_ref, qt_ref, k_ref, vt_ref, wqt_ref, wvt_ref):
    @pl.when((pl.program_id(0) == 0) & (pl.program_id(1) == 0))
    def _():
        wqt_ref[...] = wq_ref[...].T
        wvt_ref[...] = wv_ref[...].T

    xn = _rms(x_ref[...], g_ref[...], NORM_EPS).astype(BF16)
    nt = (((1,), (1,)), ((), ()))
    qt_ref[...] = lax.dot_general(wqt_ref[...], xn, nt, preferred_element_type=F32).astype(BF16)
    k_ref[...] = jnp.dot(xn, wk_ref[...], preferred_element_type=F32).astype(BF16)
    vt_ref[...] = lax.dot_general(wvt_ref[...], xn, nt, preferred_element_type=F32).astype(BF16)


def _qkv_proj(x3, g, wq, wk, wv, name):
    batch, seq, d = x3.shape
    rows = PROJ_ROWS
    qw, kw, vw = wq.shape[1], wk.shape[1], wv.shape[1]

    def full(w):
        return pl.BlockSpec(w.shape, lambda b, i: (0, 0))

    def transposed(width):
        return pl.BlockSpec((None, width, rows), lambda b, i: (b, 0, i))

    return pl.pallas_call(
        _qkv_proj_kernel,
        grid=(batch, seq // rows),
        in_specs=[pl.BlockSpec((None, rows, d), lambda b, i: (b, i, 0)),
                  pl.BlockSpec((1, d), lambda b, i: (0, 0)),
                  full(wq), full(wk), full(wv)],
        out_specs=[transposed(qw), pl.BlockSpec((None, rows, kw), lambda b, i: (b, i, 0)), transposed(vw)],
        out_shape=[jax.ShapeDtypeStruct((batch, qw, seq), BF16),
                   jax.ShapeDtypeStruct((batch, seq, kw), BF16),
                   jax.ShapeDtypeStruct((batch, vw, seq), BF16)],
        scratch_shapes=[pltpu.VMEM((qw, d), BF16), pltpu.VMEM((vw, d), BF16)],
        compiler_params=_params("arbitrary", "arbitrary"),
        name=name,
    )(x3, g, wq, wk, wv)


def _swa_attn_kernel(slope_ref, sink_ref, qt_ref, kp_ref, kc_ref, kn_ref, vp_ref, vc_ref, vn_ref, o_ref,
                     bias_ref, sinkrow_ref):
    first_step = (pl.program_id(0) == 0) & (pl.program_id(1) == 0)
    i = pl.program_id(1)
    last = pl.num_programs(1) - 1
    tq = SWA_Q
    nk = 3 * tq
    nl = SWA_GROUP * tq

    @pl.when(first_step)
    def _():
        row = lax.broadcasted_iota(jnp.int32, (nk, nl), 0)
        lane = lax.broadcasted_iota(jnp.int32, (nk, nl), 1)
        dist = jnp.abs(row - tq - (lane & (tq - 1)))
        distf = dist.astype(F32)
        lane1 = lax.broadcasted_iota(jnp.int32, (1, nl), 1)
        tq_bits = tq.bit_length() - 1
        for j in range(SWA_KV_HEADS):
            slope = jnp.zeros((nk, nl), F32)
            sink = jnp.zeros((1, nl), F32)
            for g in range(SWA_GROUP):
                h = j * SWA_GROUP + g
                slope = jnp.where((lane >> tq_bits) == g, slope_ref[h], slope)
                sink = jnp.where((lane1 >> tq_bits) == g, sink_ref[h], sink)
            sinkrow_ref[j] = sink
            bias = jnp.where(dist <= WINDOW, -slope * distf, NEG)
            bias_ref[0, j] = bias
            bias_ref[1, j] = jnp.where(row < tq, NEG, bias)
            bias_ref[2, j] = jnp.where(row >= 2 * tq, NEG, bias)

    edge = jnp.where(i == 0, 1, jnp.where(i == last, 2, 0))
    kwin = jnp.concatenate([kp_ref[...], kc_ref[...], kn_ref[...]], axis=0)
    kvw = kwin.shape[1]
    for j in range(SWA_KV_HEADS):
        rows = slice(j * HEAD_DIM, (j + 1) * HEAD_DIM)
        qj = jnp.concatenate([qt_ref[(j * SWA_GROUP + g) * HEAD_DIM:(j * SWA_GROUP + g + 1) * HEAD_DIM, :]
                              for g in range(SWA_GROUP)], axis=1)
        parts = []
        if j > 0:
            parts.append(jnp.zeros((j * HEAD_DIM, nl), BF16))
        parts.append(qj)
        if (j + 1) * HEAD_DIM < kvw:
            parts.append(jnp.zeros((kvw - (j + 1) * HEAD_DIM, nl), BF16))
        qpad = jnp.concatenate(parts, axis=0)
        s = jnp.dot(kwin, qpad, preferred_element_type=F32) + bias_ref[edge, j]
        sink = sinkrow_ref[j]
        m = jnp.maximum(jnp.max(s, axis=0, keepdims=True), sink)
        p = jnp.exp2(s - m)
        vwin = jnp.concatenate([vp_ref[rows, :], vc_ref[rows, :], vn_ref[rows, :]], axis=1)
        vwin = jnp.concatenate([vwin, jnp.ones((SUM_ROWS, nk), BF16)], axis=0)
        pv = jnp.dot(vwin, p.astype(BF16), preferred_element_type=F32)
        denom = pv[HEAD_DIM:HEAD_DIM + 1] + jnp.exp2(sink - m)
        ot = pv[:HEAD_DIM] / denom
        for g in range(SWA_GROUP):
            h = j * SWA_GROUP + g
            o_ref[:, h * HEAD_DIM:(h + 1) * HEAD_DIM] = ot[:, g * tq:(g + 1) * tq].T.astype(BF16)


def _swa_attn(qt, k, vt, slopes, sink):
    batch, qw, seq = qt.shape
    kvw = k.shape[2]
    nb = seq // SWA_Q
    smem = pl.BlockSpec(memory_space=pltpu.SMEM)

    def k_spec(shift):
        return pl.BlockSpec((None, SWA_Q, kvw), lambda b, i: (b, jnp.clip(i + shift, 0, nb - 1), 0))

    def v_spec(shift):
        return pl.BlockSpec((None, kvw, SWA_Q), lambda b, i: (b, 0, jnp.clip(i + shift, 0, nb - 1)))

    out = pl.pallas_call(
        _swa_attn_kernel,
        grid=(batch, nb),
        in_specs=[smem, smem,
                  pl.BlockSpec((None, qw, SWA_Q), lambda b, i: (b, 0, i)),
                  k_spec(-1), k_spec(0), k_spec(1), v_spec(-1), v_spec(0), v_spec(1)],
        out_specs=pl.BlockSpec((None, SWA_Q, qw), lambda b, i: (b, i, 0)),
        out_shape=jax.ShapeDtypeStruct((batch, seq, qw), BF16),
        scratch_shapes=[pltpu.VMEM((3, SWA_KV_HEADS, 3 * SWA_Q, SWA_GROUP * SWA_Q), F32),
                        pltpu.VMEM((SWA_KV_HEADS, 1, SWA_GROUP * SWA_Q), F32)],
        compiler_params=_params("arbitrary", "arbitrary"),
        name="swa_attn",
    )(slopes * LOG2E, sink * LOG2E, qt, k, k, k, vt, vt, vt)
    return out.reshape(batch * seq, qw)


def _router_kernel(a_ref, wo_ref, x_ref, g_ref, w_ref, b_ref, xo_ref, xp_ref, meta_ref, dest_ref, plan_ref,
                   carry_ref, er_ref):
    t = pl.program_id(0)
    rows = x_ref.shape[0]

    @pl.when(t == 0)
    def _():
        carry_ref[...] = jnp.zeros_like(carry_ref)

    x = x_ref[...] + jnp.dot(a_ref[...], wo_ref[...], preferred_element_type=F32)
    xo_ref[...] = x
    xn = _rms(x, g_ref[...], NORM_EPS).astype(BF16)
    half = x.shape[1] // 2
    bits = lax.bitcast_convert_type(xn.astype(jnp.bfloat16).astype(F32), jnp.int32)
    xp_ref[...] = (bits[:, half:] & -65536) | lax.shift_right_logical(bits[:, :half], 16)
    logits = jnp.dot(xn, w_ref[...], preferred_element_type=F32) + b_ref[...]
    lane = lax.broadcasted_iota(jnp.int32, (rows, LANES), 1)

    is_group = lane < N_GROUPS
    gl = jnp.where(is_group, logits, NEG)
    gmax = jnp.max(gl, axis=-1, keepdims=True)
    gidx = jnp.min(jnp.where(gl == gmax, lane, LANES), axis=-1, keepdims=True)
    zg = jnp.sum(jnp.where(is_group, jnp.exp(gl - gmax), 0.0), axis=-1, keepdims=True)
    pg = 1.0 / zg

    lo = N_GROUPS + EXPERTS_PER_GROUP * gidx
    in_group = (lane >= lo) & (lane < lo + EXPERTS_PER_GROUP)
    el = jnp.where(in_group, logits, NEG)
    m0 = jnp.max(el, axis=-1, keepdims=True)
    i0 = jnp.min(jnp.where(el == m0, lane, LANES), axis=-1, keepdims=True)
    el1 = jnp.where(lane == i0, NEG, el)
    m1 = jnp.max(el1, axis=-1, keepdims=True)
    i1 = jnp.min(jnp.where(el1 == m1, lane, LANES), axis=-1, keepdims=True)
    r = jnp.exp(m1 - m0)
    w0 = pg / (1.0 + r)
    w1 = pg * r / (1.0 + r)
    e0 = i0 - N_GROUPS
    e1 = i1 - N_GROUPS

    sel0 = lane == e0
    sel1 = lane == e1
    onehot = (sel0 | sel1).astype(BF16)
    rr = lax.broadcasted_iota(jnp.int32, (rows, rows), 0)
    cc = lax.broadcasted_iota(jnp.int32, (rows, rows), 1)
    tri = (cc < rr).astype(BF16)
    before = jnp.dot(tri, onehot, preferred_element_type=F32) + carry_ref[...]
    r0 = jnp.sum(jnp.where(sel0, before, 0.0), axis=-1, keepdims=True)
    r1 = jnp.sum(jnp.where(sel1, before, 0.0), axis=-1, keepdims=True)
    carry_ref[...] = carry_ref[...] + jnp.sum(onehot.astype(F32), axis=0, keepdims=True)

    meta = jnp.where(lane == 0, e0.astype(F32), 0.0)
    meta = jnp.where(lane == 1, e1.astype(F32), meta)
    meta = jnp.where(lane == 2, r0, meta)
    meta = jnp.where(lane == 3, r1, meta)
    meta = jnp.where(lane == 4, w0, meta)
    meta = jnp.where(lane == 5, w1, meta)
    meta_ref[...] = meta
    metat = meta.T
    for f in range(4):
        er_ref[t, :, f * rows:(f + 1) * rows] = metat[f:f + 1, :].astype(jnp.int32)

    @pl.when(t == pl.num_programs(0) - 1)
    def _plan():
        counts = jnp.broadcast_to(carry_ref[...], (8, LANES))
        padded = jnp.floor((counts + (MOE_ROWS - 1)) * (1.0 / MOE_ROWS)) * MOE_ROWS
        lane8 = lax.broadcasted_iota(jnp.int32, (8, LANES), 1)
        pend = padded
        shift = 1
        while shift < LANES:
            pend = pend + jnp.where(lane8 >= shift, pltpu.roll(pend, shift, 1), 0.0)
            shift *= 2
        starts_counts = jnp.where(lane8 < N_EXPERTS, pend - padded, pltpu.roll(counts, N_EXPERTS, 1))
        plan_ref[:, 0:LANES] = starts_counts[0:1].astype(jnp.int32)
        sq_r = lax.broadcasted_iota(jnp.int32, (LANES, LANES), 0)
        sq_c = lax.broadcasted_iota(jnp.int32, (LANES, LANES), 1)
        pend_col = jnp.sum(jnp.where(sq_r == sq_c, jnp.broadcast_to(pend[0:1], (LANES, LANES)), 0.0),
                           axis=1, keepdims=True)
        nbp = plan_ref.shape[1] - LANES
        blk_row = lax.broadcasted_iota(jnp.int32, (LANES, nbp), 0)
        blk_start = (lax.broadcasted_iota(jnp.int32, (LANES, nbp), 1) * MOE_ROWS).astype(F32)
        ended = (pend_col <= blk_start) & (blk_row < N_EXPERTS)
        block_e = jnp.minimum(jnp.sum(ended.astype(F32), axis=0, keepdims=True), N_EXPERTS - 1.0)
        n_used = jnp.sum(jnp.where(lane8[0:1] == N_EXPERTS - 1, pend[0:1], 0.0), axis=1,
                         keepdims=True) * (1.0 / MOE_ROWS)
        blk_lane = lax.broadcasted_iota(jnp.int32, (1, nbp), 1)
        plan_ref[:, LANES:] = jnp.where(blk_lane == nbp - 1, n_used, block_e).astype(jnp.int32)

        starts = (pend - padded)[0:1]
        start_of = [jnp.sum(jnp.where(lane8[0:1] == k, starts, 0.0), axis=1, keepdims=True)
                    for k in range(N_EXPERTS)]

        def tile_dest(tt, carry):
            er = er_ref[tt]
            for slot in range(2):
                e = er[:, slot * rows:(slot + 1) * rows]
                rank = er[:, (2 + slot) * rows:(3 + slot) * rows]
                base = jnp.zeros((1, rows), F32)
                for k in range(N_EXPERTS):
                    base = jnp.where(e == k, start_of[k], base)
                dest_ref[tt, :, slot * rows:(slot + 1) * rows] = base.astype(jnp.int32) + rank
            return carry

        lax.fori_loop(0, pl.num_programs(0), tile_dest, 0)


def _plan_width(n):
    n_blocks = (2 * n + N_EXPERTS * MOE_ROWS) // MOE_ROWS
    return LANES + (n_blocks // LANES + 1) * LANES


def _router(attn, w_o, x2, g, w_pad, b_pad):
    n, d = x2.shape
    rows = MOVE_ROWS
    tiles = n // rows
    pw = _plan_width(n)
    return pl.pallas_call(
        _router_kernel,
        grid=(tiles,),
        in_specs=[pl.BlockSpec((rows, attn.shape[1]), lambda t: (t, 0)),
                  pl.BlockSpec(w_o.shape, lambda t: (0, 0)),
                  pl.BlockSpec((rows, d), lambda t: (t, 0)),
                  pl.BlockSpec((1, d), lambda t: (0, 0)),
                  pl.BlockSpec((d, LANES), lambda t: (0, 0)),
                  pl.BlockSpec((1, LANES), lambda t: (0, 0))],
        out_specs=[pl.BlockSpec((rows, d), lambda t: (t, 0)),
                   pl.BlockSpec((rows, d // 2), lambda t: (t, 0)),
                   pl.BlockSpec((rows, LANES), lambda t: (t, 0)),
                   pl.BlockSpec((tiles, 1, 2 * rows), lambda t: (0, 0, 0)),
                   pl.BlockSpec((1, pw), lambda t: (0, 0))],
        out_shape=[jax.ShapeDtypeStruct((n, d), F32),
                   jax.ShapeDtypeStruct((n, d // 2), jnp.int32),
                   jax.ShapeDtypeStruct((n, LANES), F32),
                   jax.ShapeDtypeStruct((tiles, 1, 2 * rows), jnp.int32),
                   jax.ShapeDtypeStruct((1, pw), jnp.int32)],
        scratch_shapes=[pltpu.VMEM((1, LANES), F32), pltpu.VMEM((tiles, 1, 4 * rows), jnp.int32)],
        compiler_params=_params("arbitrary"),
        name="router",
    )(attn, w_o, x2, g, w_pad, b_pad)


def _row_copy(src, src_row, dst, dst_row, sem):
    return pltpu.make_async_copy(src.at[pl.ds(src_row, 1)], dst.at[pl.ds(dst_row, 1)], sem)


def _dest_rows(dest_ref, r, rows):
    return dest_ref[0, 0, r], dest_ref[0, 0, rows + r]


def _dispatch_kernel(dest_ref, plan_ref, x_ref, xs_hbm, zero_ref, sem, zsem):
    rows = x_ref.shape[0]

    @pl.when(pl.program_id(0) == 0)
    def _():
        zero_ref[...] = jnp.zeros_like(zero_ref)

        def fill_expert(e, total):
            count = plan_ref[N_EXPERTS + e]
            first = plan_ref[e] + count
            pad = (-count) & (MOE_ROWS - 1)

            def fill(r, c):
                _row_copy(zero_ref, 0, xs_hbm, first + r, zsem).start()
                return c

            def fill_many(q, c):
                for u in range(ISSUE_UNROLL):
                    _row_copy(zero_ref, 0, xs_hbm, first + q * ISSUE_UNROLL + u, zsem).start(priority=u % 2)
                return c

            bulk = pad // ISSUE_UNROLL
            lax.fori_loop(0, bulk, fill_many, 0)
            lax.fori_loop(bulk * ISSUE_UNROLL, pad, fill, 0)
            return total + pad

        total = lax.fori_loop(0, N_EXPERTS, fill_expert, 0)

        def block_copy(b):
            start = pl.multiple_of(b * MOE_ROWS, MOE_ROWS)
            return pltpu.make_async_copy(zero_ref, xs_hbm.at[pl.ds(start, MOE_ROWS)], zsem)

        def drain_rows(r, c):
            block_copy(0).wait()
            return c

        def drain(r, c):
            _row_copy(zero_ref, 0, xs_hbm, 0, zsem).wait()
            return c

        whole = total // MOE_ROWS
        lax.fori_loop(0, whole, drain_rows, 0)
        lax.fori_loop(whole * MOE_ROWS, total, drain, 0)

        n_used = plan_ref[plan_ref.shape[0] - 1]
        n_blocks = xs_hbm.shape[0] // MOE_ROWS

        def fill_block(b, c):
            block_copy(b).start()
            return c

        def drain_block(b, c):
            block_copy(b).wait()
            return c

        lax.fori_loop(n_used, n_blocks, fill_block, 0)
        lax.fori_loop(n_used, n_blocks, drain_block, 0)

    def start(r, c):
        d0, d1 = _dest_rows(dest_ref, r, rows)
        _row_copy(x_ref, r, xs_hbm, d0, sem).start()
        _row_copy(x_ref, r, xs_hbm, d1, sem).start(priority=1)
        return c

    lax.fori_loop(0, rows, start, 0, unroll=ISSUE_UNROLL)
    for _ in range(2):
        pltpu.make_async_copy(x_ref, xs_hbm.at[pl.ds(0, rows)], sem).wait()


def _dispatch(x2, dest, plan, cap):
    n, d = x2.shape
    rows = MOVE_ROWS
    any_spec = pl.BlockSpec(memory_space=pl.ANY)
    smem = pl.BlockSpec(memory_space=pltpu.SMEM)
    return pl.pallas_call(
        _dispatch_kernel,
        grid=(n // rows,),
        in_specs=[pl.BlockSpec((1, 1, 2 * rows), lambda t: (t, 0, 0), memory_space=pltpu.SMEM),
                  smem,
                  pl.BlockSpec((rows, d), lambda t: (t, 0))],
        out_specs=any_spec,
        out_shape=jax.ShapeDtypeStruct((cap, d), x2.dtype),
        scratch_shapes=[pltpu.VMEM((MOE_ROWS, d), x2.dtype), pltpu.SemaphoreType.DMA(()),
                        pltpu.SemaphoreType.DMA(())],
        compiler_params=pltpu.CompilerParams(dimension_semantics=("arbitrary",),
                                             has_side_effects=True, vmem_limit_bytes=VMEM_LIMIT),
        name="moe_dispatch",
    )(dest, plan, x2)


def _expert_kernel(plan_ref, xs_ref, wg_ref, wu_ref, wd_ref, ys_ref, wgb, wub, wdb):
    b = pl.program_id(0)
    e = plan_ref[LANES + b]
    prev = plan_ref[LANES + jnp.maximum(b - 1, 0)]
    n_used = plan_ref[plan_ref.shape[0] - 1]

    @pl.when((b == 0) | (e != prev))
    def _():
        wgb[...] = wg_ref[...].astype(BF16)
        wub[...] = wu_ref[...].astype(BF16)
        wdb[...] = wd_ref[...].astype(BF16)

    @pl.when(b < n_used)
    def _():
        words = xs_ref[...]
        half = words.shape[1]
        x_lo = lax.bitcast_convert_type(words << 16, F32).astype(BF16)
        x_hi = lax.bitcast_convert_type(words & -65536, F32).astype(BF16)

        def project(w_ref):
            return (jnp.dot(x_lo, w_ref[:half, :], preferred_element_type=F32)
                    + jnp.dot(x_hi, w_ref[half:, :], preferred_element_type=F32))

        hg = project(wgb)
        hu = project(wub)
        hid = (hg / (1.0 + jnp.exp(-hg))) * hu
        ys_ref[...] = jnp.dot(hid.astype(BF16), wdb[...], preferred_element_type=F32)

    @pl.when(b >= n_used)
    def _():
        ys_ref[...] = jnp.zeros_like(ys_ref)


def _experts(xs, w_gate, w_up, w_down, plan, layer):
    cap, half = xs.shape
    d = 2 * half
    de = w_gate.shape[-1]
    rows = MOE_ROWS

    def expert_block(b, plan):
        return (layer, plan[LANES + b], 0, 0)

    grid_spec = pltpu.PrefetchScalarGridSpec(
        num_scalar_prefetch=1,
        grid=(cap // rows,),
        in_specs=[pl.BlockSpec((rows, half), lambda b, plan: (jnp.where(b < plan[plan.shape[0] - 1], b, 0), 0)),
                  pl.BlockSpec((None, None, d, de), expert_block),
                  pl.BlockSpec((None, None, d, de), expert_block),
                  pl.BlockSpec((None, None, de, d), expert_block)],
        out_specs=pl.BlockSpec((rows, d), lambda b, plan: (b, 0)),
        scratch_shapes=[pltpu.VMEM((d, de), BF16), pltpu.VMEM((d, de), BF16), pltpu.VMEM((de, d), BF16)],
    )
    return pl.pallas_call(
        _expert_kernel,
        grid_spec=grid_spec,
        out_shape=jax.ShapeDtypeStruct((cap, d), F32),
        compiler_params=_params("arbitrary"),
        name="moe_experts",
    )(plan, xs, w_gate, w_up, w_down)


def _combine_kernel(dest_ref, x_ref, meta_ref, g_ref, ys_hbm, o_ref, ybuf, sem, *, final_norm):
    rows = x_ref.shape[0]

    def start(r, c):
        d0, d1 = _dest_rows(dest_ref, r, rows)
        _row_copy(ys_hbm, d0, ybuf.at[0], r, sem).start()
        _row_copy(ys_hbm, d1, ybuf.at[1], r, sem).start(priority=1)
        return c

    lax.fori_loop(0, rows, start, 0, unroll=ISSUE_UNROLL)
    for k in range(2):
        pltpu.make_async_copy(ys_hbm.at[pl.ds(0, rows)], ybuf.at[k], sem).wait()
    meta = meta_ref[...]
    out = x_ref[...] + (meta[:, 4:5] * ybuf[0] + meta[:, 5:6] * ybuf[1])
    if final_norm:
        out = _rms(out, g_ref[...], NORM_EPS)
    o_ref[...] = out


def _combine(x2, meta, dest, ys, g, final_norm):
    n, d = x2.shape
    rows = MOVE_ROWS
    return pl.pallas_call(
        functools.partial(_combine_kernel, final_norm=final_norm),
        grid=(n // rows,),
        in_specs=[pl.BlockSpec((1, 1, 2 * rows), lambda t: (t, 0, 0), memory_space=pltpu.SMEM),
                  pl.BlockSpec((rows, d), lambda t: (t, 0)),
                  pl.BlockSpec((rows, LANES), lambda t: (t, 0)),
                  pl.BlockSpec((1, d), lambda t: (0, 0)),
                  pl.BlockSpec(memory_space=pl.ANY)],
        out_specs=pl.BlockSpec((rows, d), lambda t: (t, 0)),
        out_shape=jax.ShapeDtypeStruct((n, d), F32),
        scratch_shapes=[pltpu.VMEM((2, rows, d), F32), pltpu.SemaphoreType.DMA(())],
        compiler_params=_params("arbitrary"),
        name="moe_combine",
    )(dest, x2, meta, g, ys)


def _attn_out_moe(attn, w_o, x2, g, w_group, b_group, w_router, b_router, w_gate, w_up, w_down, layer,
                  g_final, final_norm):
    n, d = x2.shape
    pad = LANES - N_GROUPS - N_EXPERTS
    w_pad = jnp.concatenate([w_group, w_router, jnp.zeros((d, pad), F32)], axis=1).astype(BF16)
    b_pad = jnp.concatenate([b_group, b_router, jnp.zeros((pad,), F32)])[None, :]
    x2, xp, meta, dest, plan = _router(attn, w_o, x2, g, w_pad, b_pad)
    plan = plan.reshape(-1)
    cap = 2 * n + N_EXPERTS * MOE_ROWS
    xs = _dispatch(xp, dest, plan, cap)
    ys = _experts(xs, w_gate, w_up, w_down, plan, layer)
    return _combine(x2, meta, dest, ys, g_final, final_norm)


def _diff_attn_kernel(cp_ref, lam_ref, subg_ref, qt_ref, k_ref, vt_ref, o_ref,
                      m_ref, acc_ref, qx_ref, s_ref, fix_ref, kaug_ref, km_ref, *, lambda_init):
    h = pl.program_id(1)
    i = pl.program_id(2)
    tq = qt_ref.shape[-1]
    tk = DIFF_K
    per_q = tq // tk
    n_groups = k_ref.shape[0] // tq
    cp = [cp_ref[3 * h + p] for p in range(3)]
    slope2 = cp[0] + cp[1] + cp[2]

    @pl.when(i == 0)
    def _():
        kl = lax.broadcasted_iota(jnp.int32, (tk, LANES), 1)
        kr = lax.broadcasted_iota(jnp.int32, (tk, LANES), 0).astype(F32)
        qr = lax.broadcasted_iota(jnp.int32, (LANES, tq), 0)
        ql = lax.broadcasted_iota(jnp.int32, (LANES, tq), 1)
        di_lo = (ql & 255).astype(F32)
        di_hi = (ql >> 8).astype(F32)
        kaug = jnp.zeros((tk, LANES), F32)
        qaug = jnp.zeros((LANES, tq), F32)
        for p in range(3):
            kaug = jnp.where(kl == p, -cp[p], kaug)
            kaug = jnp.where(kl == 3 + p, -256.0 * cp[p], kaug)
            kaug = jnp.where(kl == 6 + p, kr, kaug)
            qaug = jnp.where(qr == p, di_lo, qaug)
            qaug = jnp.where(qr == 3 + p, di_hi, qaug)
            qaug = jnp.where(qr == 6 + p, cp[p], qaug)
        kaug_ref[...] = kaug.astype(BF16)
        for c in range(2):
            qx_ref[0, c, 2 * HEAD_DIM:, :] = qaug.astype(BF16)
            qx_ref[1, c, 2 * HEAD_DIM:, :] = (-qaug).astype(BF16)
        for u in range(per_q):
            rel = (lax.broadcasted_iota(jnp.int32, (tk, tq), 0)
                   - lax.broadcasted_iota(jnp.int32, (tk, tq), 1)) + u * tk
            fix_ref[u] = (2.0 * slope2) * jnp.maximum(rel, 0).astype(F32)
        lane_k = lax.broadcasted_iota(jnp.int32, (1, LANES), 1)
        sel_r = lax.broadcasted_iota(jnp.int32, (LANES, LANES), 0)
        sel_c = lax.broadcasted_iota(jnp.int32, (LANES, LANES), 1)
        selector = ((sel_r >= sel_c * HEAD_DIM) & (sel_r < (sel_c + 1) * HEAD_DIM)).astype(BF16)
        group_norm = []
        for g in range(n_groups):
            kf = k_ref[g * tq:(g + 1) * tq, :].astype(F32)
            norms2 = jnp.dot((kf * kf).astype(BF16), selector, preferred_element_type=F32)
            top = jnp.max(norms2, axis=0, keepdims=True)
            group_norm.append([jnp.sqrt(jnp.sum(jnp.where(lane_k == c, top, 0.0), axis=1, keepdims=True))
                               * NORM_SLACK for c in range(2)])
        for c in range(2):
            running = jnp.zeros((1, 1), F32)
            row = jnp.zeros((1, LANES), F32)
            for g in range(n_groups):
                running = jnp.maximum(running, group_norm[g][c])
                row = jnp.where(lane_k == g, running, row)
            km_ref[0, c] = row
            running = jnp.zeros((1, 1), F32)
            row = jnp.zeros((1, LANES), F32)
            for g in reversed(range(n_groups)):
                running = jnp.maximum(running, group_norm[g][c])
                row = jnp.where(lane_k == g, running, row)
            km_ref[1, c] = row

    zeros = jnp.zeros((HEAD_DIM, tq), BF16)
    q_maps = (jnp.concatenate([qt_ref[0], zeros], axis=0),
              jnp.concatenate([zeros, qt_ref[1]], axis=0))
    for c in range(2):
        qx_ref[0, c, :2 * HEAD_DIM, :] = q_maps[c]
        qx_ref[1, c, :2 * HEAD_DIM, :] = q_maps[c]
    m_ref[...] = jnp.full_like(m_ref, NEG)
    acc_ref[...] = jnp.zeros_like(acc_ref)
    kaug = kaug_ref[...]

    n_k = n_groups * per_q

    def qk_stage(j, slot):
        jc = jnp.clip(j, 0, n_k - 1)
        start = pl.multiple_of(jc * tk, tk)
        kb = jnp.concatenate([k_ref[pl.ds(start, tk), :], kaug], axis=1)
        after = (jc >= (i + 1) * per_q).astype(jnp.int32)
        for c in range(2):
            s_ref[slot, c] = jnp.dot(kb, qx_ref[after, c], preferred_element_type=F32)

    ones = jnp.ones((SUM_ROWS, tk), BF16)

    def softmax_stage(j, slot, after, fix):
        start = pl.multiple_of(j * tk, tk)
        vb = jnp.concatenate([vt_ref[:, pl.ds(start, tk)], ones], axis=0)
        delta = (i * tq - start).astype(F32)
        off = slope2 * (-delta if after else delta)
        for c in range(2):
            s = s_ref[slot, c]
            if fix is not None:
                s = s - fix_ref[fix]
            m_old = m_ref[c]
            m_new = jnp.maximum(m_old, jnp.max(s, axis=0, keepdims=True) - off)
            alpha = jnp.exp2(m_old - m_new)
            p = jnp.exp2(s - (m_new + off))
            acc_ref[c] = alpha * acc_ref[c] + jnp.dot(vb, p.astype(BF16), preferred_element_type=F32)
            m_ref[c] = m_new

    def fixed_max_stage(j, slot, after):
        start = pl.multiple_of(j * tk, tk)
        vb = jnp.concatenate([vt_ref[:, pl.ds(start, tk)], ones], axis=0)
        delta = (i * tq - start).astype(F32)
        off = slope2 * (-delta if after else delta)
        for c in range(2):
            p = jnp.exp2(s_ref[slot, c] - (m_ref[c] + off))
            acc_ref[c] = acc_ref[c] + jnp.dot(vb, p.astype(BF16), preferred_element_type=F32)

    def pair(j0, next_j0, after, diagonal, fixed_max=False):
        def stage(j, slot, fix):
            if fixed_max:
                fixed_max_stage(j, slot, after)
            else:
                softmax_stage(j, slot, after, fix)
        qk_stage(j0 + 1, 1)
        stage(j0, 0, 0 if diagonal else None)
        qk_stage(next_j0, 0)
        stage(j0 + 1, 1, 1 if diagonal else None)

    def run_groups(first, step, count, after, then_group, fixed_max):
        def body(unroll, base):
            def run(t, carry):
                for u in range(unroll):
                    done = base + t * unroll + u
                    g = first + step * done
                    nxt = jnp.where(done == count - 1, then_group, g + step)
                    pair(g * per_q, nxt * per_q, after, False, fixed_max)
                return carry
            return run
        n_main = count // DIFF_UNROLL
        lax.fori_loop(0, n_main, body(DIFF_UNROLL, 0), 0)
        lax.fori_loop(0, count - n_main * DIFF_UNROLL, body(1, n_main * DIFF_UNROLL), 0)

    qk_stage(i * per_q, 0)
    pair(i * per_q, (i - 1) * per_q, False, True)

    qn = [jnp.sqrt(jnp.sum(jnp.square(qt_ref[c].astype(F32)), axis=0, keepdims=True)) for c in range(2)]

    step_t = lax.broadcasted_iota(jnp.int32, (n_groups, 1), 0)
    step_grid = lax.broadcasted_iota(jnp.int32, (n_groups, LANES), 0)
    lane_grid = lax.broadcasted_iota(jnp.int32, (n_groups, LANES), 1)
    penalty = slope2 * (step_t * tq + 1).astype(F32)

    def groups_needed(side, available):
        group_of_step = (i - 1 - step_grid) if side == 0 else (i + 1 + step_grid)
        pick = lane_grid == group_of_step
        need = None
        excess = jnp.full((1, tq), NEG, F32)
        for c in range(2):
            km = jnp.sum(jnp.where(pick, km_ref[side, c], 0.0), axis=1, keepdims=True)
            bound = qn[c] * km - penalty
            reach = bound >= (m_ref[c] - SKIP_MARGIN)
            need = reach if need is None else (need | reach)
            excess = jnp.maximum(excess, bound[0:1] - m_ref[c])
        cnt = jnp.sum(jnp.where(need & (step_t < available), 1.0, 0.0), axis=0, keepdims=True)
        return jnp.max(cnt).astype(jnp.int32), jnp.max(excess) <= FAST_MARGIN

    n_before, calm_before = groups_needed(0, i)
    n_after, calm_after = groups_needed(1, n_groups - 1 - i)

    def run_side(first, step, count, after, then_group, calm):
        @pl.when(calm)
        def _():
            run_groups(first, step, count, after, then_group, True)

        @pl.when(jnp.logical_not(calm))
        def _():
            run_groups(first, step, count, after, then_group, False)

    run_side(i - 1, -1, n_before, False, i + 1, calm_before)

    @pl.when((n_before == 0) & (n_after > 0))
    def _():
        qk_stage((i + 1) * per_q, 0)

    run_side(i + 1, 1, n_after, True, i + 1, calm_after)

    lam = lam_ref[...]
    lam_full = (jnp.exp(jnp.sum(lam[0:1] * lam[1:2], axis=-1, keepdims=True))
                - jnp.exp(jnp.sum(lam[2:3] * lam[3:4], axis=-1, keepdims=True)) + lambda_init)
    vdim = vt_ref.shape[0]
    o = (acc_ref[0, :vdim] / acc_ref[0, vdim:vdim + 1]
         - lam_full * (acc_ref[1, :vdim] / acc_ref[1, vdim:vdim + 1]))
    ms = jnp.mean(o * o, axis=0, keepdims=True)
    on = o * lax.rsqrt(ms + SUBLN_EPS) * subg_ref[...] * (1.0 - lambda_init)
    o_ref[...] = on.T.astype(BF16)


def _slope_pieces(slopes):
    s2 = slopes * LOG2E
    c1 = s2.astype(BF16).astype(F32)
    c2 = (s2 - c1).astype(BF16).astype(F32)
    c3 = (s2 - c1 - c2).astype(BF16).astype(F32)
    return jnp.stack([c1, c2, c3], axis=1).reshape(-1)


def _diff_attn(qt, k, vt, slopes, lam, subg, lambda_init):
    batch, heads, _, _, seq = qt.shape
    vdim = vt.shape[2]
    tq = DIFF_Q
    assert tq == 2 * 256 and DIFF_K == 256 and seq % tq == 0
    slopes = _slope_pieces(slopes)
    smem = pl.BlockSpec(memory_space=pltpu.SMEM)
    return pl.pallas_call(
        functools.partial(_diff_attn_kernel, lambda_init=lambda_init),
        grid=(batch, heads, seq // tq),
        in_specs=[smem,
                  pl.BlockSpec(lam.shape, lambda b, h, i: (0, 0)),
                  pl.BlockSpec(subg.shape, lambda b, h, i: (0, 0)),
                  pl.BlockSpec((None, None, 2, HEAD_DIM, tq), lambda b, h, i: (b, h, 0, 0, i)),
                  pl.BlockSpec((None, seq, 2 * HEAD_DIM), lambda b, h, i: (b, 0, h)),
                  pl.BlockSpec((None, None, vdim, seq), lambda b, h, i: (b, h, 0, 0))],
        out_specs=pl.BlockSpec((None, tq, vdim), lambda b, h, i: (b, i, h)),
        out_shape=jax.ShapeDtypeStruct((batch, seq, heads * vdim), BF16),
        scratch_shapes=[pltpu.VMEM((2, 1, tq), F32),
                        pltpu.VMEM((2, vdim + SUM_ROWS, tq), F32),
                        pltpu.VMEM((2, 2, 2 * HEAD_DIM + LANES, tq), BF16),
                        pltpu.VMEM((2, 2, DIFF_K, tq), F32),
                        pltpu.VMEM((tq // DIFF_K, DIFF_K, tq), F32),
                        pltpu.VMEM((DIFF_K, LANES), BF16),
                        pltpu.VMEM((2, 2, 1, LANES), F32)],
        compiler_params=_params("arbitrary", "arbitrary", "arbitrary"),
        name="diff_attn",
    )(slopes, lam, subg, qt, k, vt)


def _alibi_slopes(n_heads):
    h = jnp.arange(1, n_heads + 1, dtype=F32)
    return jnp.exp2(-8.0 * h / n_heads)


def _lambda_init(layer_idx):
    return 0.8 - 0.6 * math.exp(-0.3 * layer_idx)


def kernel(x, norm_mix, norm_ffn, swa_w_qkv, swa_w_o, swa_sink, diff_w_qkv, diff_w_o, diff_lambda,
           diff_subln, moe_w_group, moe_b_group, moe_w_router, moe_b_router, moe_w_gate, moe_w_up,
           moe_w_down, final_norm):
    batch, seq, d = x.shape
    n = batch * seq
    depth = norm_mix.shape[0]
    x2 = x.reshape(n, d)
    q_scale = HEAD_DIM ** -0.5 * LOG2E
    for layer in range(depth):
        j = layer // 2
        g_mix = norm_mix[layer][None, :]
        x3 = x2.reshape(batch, seq, d)
        if layer % 2 == 0:
            n_heads = SWA_KV_HEADS * SWA_GROUP
            qw = n_heads * HEAD_DIM
            kw = SWA_KV_HEADS * HEAD_DIM
            w = swa_w_qkv[j]
            wq = (w[:, :qw] * q_scale).astype(BF16)
            wk = w[:, qw:qw + kw].astype(BF16)
            wv = w[:, qw + kw:].astype(BF16)
            qt, k, vt = _qkv_proj(x3, g_mix, wq, wk, wv, "swa_proj")
            attn = _swa_attn(qt, k, vt, _alibi_slopes(n_heads), swa_sink[j].astype(F32))
            w_o = swa_w_o[j].astype(BF16)
        else:
            heads = d // (2 * HEAD_DIM)
            w = diff_w_qkv[j]
            wq = (w[:, :d] * q_scale).astype(BF16)
            wk = w[:, d:2 * d].astype(BF16)
            wv = w[:, 2 * d:].astype(BF16)
            qt, k, vt = _qkv_proj(x3, g_mix, wq, wk, wv, "diff_proj")
            qt = qt.reshape(batch, heads, 2, HEAD_DIM, seq)
            vt = vt.reshape(batch, heads, 2 * HEAD_DIM, seq)
            attn = _diff_attn(qt, k, vt, _alibi_slopes(heads), diff_lambda[j].astype(F32),
                              diff_subln[j].astype(F32)[:, None], _lambda_init(layer))
            attn = attn.reshape(n, d)
            w_o = diff_w_o[j].astype(BF16)
        last = layer == depth - 1
        x2 = _attn_out_moe(attn, w_o, x2, norm_ffn[layer][None, :], moe_w_group[layer], moe_b_group[layer],
                           moe_w_router[layer], moe_b_router[layer], moe_w_gate, moe_w_up, moe_w_down,
                           layer, final_norm[None, :], last)
    return x2.reshape(batch, seq, d)
```

```python
import functools
import math

import jax
import jax.numpy as jnp
from jax import lax
from jax.experimental import pallas as pl
from jax.experimental.pallas import tpu as pltpu

F32 = jnp.float32
BF16 = jnp.bfloat16

HEAD_DIM = 64
WINDOW = 128
SWA_KV_HEADS = 4
SWA_GROUP = 4
N_GROUPS = 4
EXPERTS_PER_GROUP = 8
N_EXPERTS = N_GROUPS * EXPERTS_PER_GROUP
NORM_EPS = 1e-6
SUBLN_EPS = 1e-5
NEG = -1e30
LANES = 128
LOG2E = math.log2(math.e)

PROJ_ROWS = 512
SWA_Q = 128
MOE_ROWS = 512
MOVE_ROWS = 512
ISSUE_UNROLL = 8
DIFF_Q = 512
DIFF_K = 256
DIFF_UNROLL = 4
SKIP_MARGIN = 152.0
NORM_SLACK = 1.0 + 2.0 ** -6
FAST_MARGIN = 64.0
SUM_ROWS = 16
VMEM_LIMIT = 56 * 1024 * 1024


def _params(*sem):
    return pltpu.CompilerParams(dimension_semantics=sem, vmem_limit_bytes=VMEM_LIMIT)


def _rms(x, g, eps):
    ms = jnp.mean(x * x, axis=-1, keepdims=True)
    return x * lax.rsqrt(ms + eps) * g


def _qkv_proj_kernel(x_ref, g_ref, wq_ref, wk_ref, wv_ref, qt_ref, k_ref, vt_ref, wqt_ref, wvt_ref):
    @pl.when((pl.program_id(0) == 0) & (pl.program_id(1) == 0))
    def _():
        wqt_ref[...] = wq_ref[...].T
        wvt_ref[...] = wv_ref[...].T

    xn = _rms(x_ref[...], g_ref[...], NORM_EPS).astype(BF16)
    nt = (((1,), (1,)), ((), ()))
    qt_ref[...] = lax.dot_general(wqt_ref[...], xn, nt, preferred_element_type=F32).astype(BF16)
    k_ref[...] = jnp.dot(xn, wk_ref[...], preferred_element_type=F32).astype(BF16)
    vt_ref[...] = lax.dot_general(wvt_ref[...], xn, nt, preferred_element_type=F32).astype(BF16)


def _qkv_proj(x3, g, wq, wk, wv, name):
    batch, seq, d = x3.shape
    rows = PROJ_ROWS
    qw, kw, vw = wq.shape[1], wk.shape[1], wv.shape[1]

    def full(w):
        return pl.BlockSpec(w.shape, lambda b, i: (0, 0))

    def transposed(width):
        return pl.BlockSpec((None, width, rows), lambda b, i: (b, 0, i))

    return pl.pallas_call(
        _qkv_proj_kernel,
        grid=(batch, seq // rows),
        in_specs=[pl.BlockSpec((None, rows, d), lambda b, i: (b, i, 0)),
                  pl.BlockSpec((1, d), lambda b, i: (0, 0)),
                  full(wq), full(wk), full(wv)],
        out_specs=[transposed(qw), pl.BlockSpec((None, rows, kw), lambda b, i: (b, i, 0)), transposed(vw)],
        out_shape=[jax.ShapeDtypeStruct((batch, qw, seq), BF16),
                   jax.ShapeDtypeStruct((batch, seq, kw), BF16),
                   jax.ShapeDtypeStruct((batch, vw, seq), BF16)],
        scratch_shapes=[pltpu.VMEM((qw, d), BF16), pltpu.VMEM((vw, d), BF16)],
        compiler_params=_params("arbitrary", "arbitrary"),
        name=name,
    )(x3, g, wq, wk, wv)


def _swa_attn_kernel(slope_ref, sink_ref, qt_ref, kp_ref, kc_ref, kn_ref, vp_ref, vc_ref, vn_ref, o_ref,
                     bias_ref, sinkrow_ref):
    first_step = (pl.program_id(0) == 0) & (pl.program_id(1) == 0)
    i = pl.program_id(1)
    last = pl.num_programs(1) - 1
    tq = SWA_Q
    nk = 3 * tq
    nl = SWA_GROUP * tq

    @pl.when(first_step)
    def _():
        row = lax.broadcasted_iota(jnp.int32, (nk, nl), 0)
        lane = lax.broadcasted_iota(jnp.int32, (nk, nl), 1)
        dist = jnp.abs(row - tq - (lane & (tq - 1)))
        distf = dist.astype(F32)
        lane1 = lax.broadcasted_iota(jnp.int32, (1, nl), 1)
        tq_bits = tq.bit_length() - 1
        for j in range(SWA_KV_HEADS):
            slope = jnp.zeros((nk, nl), F32)
            sink = jnp.zeros((1, nl), F32)
            for g in range(SWA_GROUP):
                h = j * SWA_GROUP + g
                slope = jnp.where((lane >> tq_bits) == g, slope_ref[h], slope)
                sink = jnp.where((lane1 >> tq_bits) == g, sink_ref[h], sink)
            sinkrow_ref[j] = sink
            bias = jnp.where(dist <= WINDOW, -slope * distf, NEG)
            bias_ref[0, j] = bias
            bias_ref[1, j] = jnp.where(row < tq, NEG, bias)
            bias_ref[2, j] = jnp.where(row >= 2 * tq, NEG, bias)

    edge = jnp.where(i == 0, 1, jnp.where(i == last, 2, 0))
    kwin = jnp.concatenate([kp_ref[...], kc_ref[...], kn_ref[...]], axis=0)
    kvw = kwin.shape[1]
    for j in range(SWA_KV_HEADS):
        rows = slice(j * HEAD_DIM, (j + 1) * HEAD_DIM)
        qj = jnp.concatenate([qt_ref[(j * SWA_GROUP + g) * HEAD_DIM:(j * SWA_GROUP + g + 1) * HEAD_DIM, :]
                              for g in range(SWA_GROUP)], axis=1)
        parts = []
        if j > 0:
            parts.append(jnp.zeros((j * HEAD_DIM, nl), BF16))
        parts.append(qj)
        if (j + 1) * HEAD_DIM < kvw:
            parts.append(jnp.zeros((kvw - (j + 1) * HEAD_DIM, nl), BF16))
        qpad = jnp.concatenate(parts, axis=0)
        s = jnp.dot(kwin, qpad, preferred_element_type=F32) + bias_ref[edge, j]
        sink = sinkrow_ref[j]
        m = jnp.maximum(jnp.max(s, axis=0, keepdims=True), sink)
        p = jnp.exp2(s - m)
        vwin = jnp.concatenate([vp_ref[rows, :], vc_ref[rows, :], vn_ref[rows, :]], axis=1)
        vwin = jnp.concatenate([vwin, jnp.ones((SUM_ROWS, nk), BF16)], axis=0)
        pv = jnp.dot(vwin, p.astype(BF16), preferred_element_type=F32)
        denom = pv[HEAD_DIM:HEAD_DIM + 1] + jnp.exp2(sink - m)
        ot = pv[:HEAD_DIM] / denom
        for g in range(SWA_GROUP):
            h = j * SWA_GROUP + g
            o_ref[:, h * HEAD_DIM:(h + 1) * HEAD_DIM] = ot[:, g * tq:(g + 1) * tq].T.astype(BF16)


def _swa_attn(qt, k, vt, slopes, sink):
    batch, qw, seq = qt.shape
    kvw = k.shape[2]
    nb = seq // SWA_Q
    smem = pl.BlockSpec(memory_space=pltpu.SMEM)

    def k_spec(shift):
        return pl.BlockSpec((None, SWA_Q, kvw), lambda b, i: (b, jnp.clip(i + shift, 0, nb - 1), 0))

    def v_spec(shift):
        return pl.BlockSpec((None, kvw, SWA_Q), lambda b, i: (b, 0, jnp.clip(i + shift, 0, nb - 1)))

    out = pl.pallas_call(
        _swa_attn_kernel,
        grid=(batch, nb),
        in_specs=[smem, smem,
                  pl.BlockSpec((None, qw, SWA_Q), lambda b, i: (b, 0, i)),
                  k_spec(-1), k_spec(0), k_spec(1), v_spec(-1), v_spec(0), v_spec(1)],
        out_specs=pl.BlockSpec((None, SWA_Q, qw), lambda b, i: (b, i, 0)),
        out_shape=jax.ShapeDtypeStruct((batch, seq, qw), BF16),
        scratch_shapes=[pltpu.VMEM((3, SWA_KV_HEADS, 3 * SWA_Q, SWA_GROUP * SWA_Q), F32),
                        pltpu.VMEM((SWA_KV_HEADS, 1, SWA_GROUP * SWA_Q), F32)],
        compiler_params=_params("arbitrary", "arbitrary"),
        name="swa_attn",
    )(slopes * LOG2E, sink * LOG2E, qt, k, k, k, vt, vt, vt)
    return out.reshape(batch * seq, qw)


def _router_kernel(a_ref, wo_ref, x_ref, g_ref, w_ref, b_ref, xo_ref, meta_ref, dest_ref, plan_ref,
                   carry_ref, er_ref):
    t = pl.program_id(0)
    rows = x_ref.shape[0]

    @pl.when(t == 0)
    def _():
        carry_ref[...] = jnp.zeros_like(carry_ref)

    x = x_ref[...] + jnp.dot(a_ref[...], wo_ref[...], preferred_element_type=F32)
    xo_ref[...] = x
    xn = _rms(x, g_ref[...], NORM_EPS).astype(BF16)
    logits = jnp.dot(xn, w_ref[...], preferred_element_type=F32) + b_ref[...]
    lane = lax.broadcasted_iota(jnp.int32, (rows, LANES), 1)

    is_group = lane < N_GROUPS
    gl = jnp.where(is_group, logits, NEG)
    gmax = jnp.max(gl, axis=-1, keepdims=True)
    gidx = jnp.min(jnp.where(gl == gmax, lane, LANES), axis=-1, keepdims=True)
    zg = jnp.sum(jnp.where(is_group, jnp.exp(gl - gmax), 0.0), axis=-1, keepdims=True)
    pg = 1.0 / zg

    lo = N_GROUPS + EXPERTS_PER_GROUP * gidx
    in_group = (lane >= lo) & (lane < lo + EXPERTS_PER_GROUP)
    el = jnp.where(in_group, logits, NEG)
    m0 = jnp.max(el, axis=-1, keepdims=True)
    i0 = jnp.min(jnp.where(el == m0, lane, LANES), axis=-1, keepdims=True)
    el1 = jnp.where(lane == i0, NEG, el)
    m1 = jnp.max(el1, axis=-1, keepdims=True)
    i1 = jnp.min(jnp.where(el1 == m1, lane, LANES), axis=-1, keepdims=True)
    r = jnp.exp(m1 - m0)
    w0 = pg / (1.0 + r)
    w1 = pg * r / (1.0 + r)
    e0 = i0 - N_GROUPS
    e1 = i1 - N_GROUPS

    sel0 = lane == e0
    sel1 = lane == e1
    onehot = (sel0 | sel1).astype(BF16)
    rr = lax.broadcasted_iota(jnp.int32, (rows, rows), 0)
    cc = lax.broadcasted_iota(jnp.int32, (rows, rows), 1)
    tri = (cc < rr).astype(BF16)
    before = jnp.dot(tri, onehot, preferred_element_type=F32) + carry_ref[...]
    r0 = jnp.sum(jnp.where(sel0, before, 0.0), axis=-1, keepdims=True)
    r1 = jnp.sum(jnp.where(sel1, before, 0.0), axis=-1, keepdims=True)
    carry_ref[...] = carry_ref[...] + jnp.sum(onehot.astype(F32), axis=0, keepdims=True)

    meta = jnp.where(lane == 0, e0.astype(F32), 0.0)
    meta = jnp.where(lane == 1, e1.astype(F32), meta)
    meta = jnp.where(lane == 2, r0, meta)
    meta = jnp.where(lane == 3, r1, meta)
    meta = jnp.where(lane == 4, w0, meta)
    meta = jnp.where(lane == 5, w1, meta)
    meta_ref[...] = meta
    metat = meta.T
    for f in range(4):
        er_ref[t, :, f * rows:(f + 1) * rows] = metat[f:f + 1, :].astype(jnp.int32)

    @pl.when(t == pl.num_programs(0) - 1)
    def _plan():
        counts = jnp.broadcast_to(carry_ref[...], (8, LANES))
        padded = jnp.floor((counts + (MOE_ROWS - 1)) * (1.0 / MOE_ROWS)) * MOE_ROWS
        lane8 = lax.broadcasted_iota(jnp.int32, (8, LANES), 1)
        pend = padded
        shift = 1
        while shift < LANES:
            pend = pend + jnp.where(lane8 >= shift, pltpu.roll(pend, shift, 1), 0.0)
            shift *= 2
        starts_counts = jnp.where(lane8 < N_EXPERTS, pend - padded, pltpu.roll(counts, N_EXPERTS, 1))
        plan_ref[:, 0:LANES] = starts_counts[0:1].astype(jnp.int32)
        sq_r = lax.broadcasted_iota(jnp.int32, (LANES, LANES), 0)
        sq_c = lax.broadcasted_iota(jnp.int32, (LANES, LANES), 1)
        pend_col = jnp.sum(jnp.where(sq_r == sq_c, jnp.broadcast_to(pend[0:1], (LANES, LANES)), 0.0),
                           axis=1, keepdims=True)
        nbp = plan_ref.shape[1] - LANES
        blk_row = lax.broadcasted_iota(jnp.int32, (LANES, nbp), 0)
        blk_start = (lax.broadcasted_iota(jnp.int32, (LANES, nbp), 1) * MOE_ROWS).astype(F32)
        ended = (pend_col <= blk_start) & (blk_row < N_EXPERTS)
        block_e = jnp.minimum(jnp.sum(ended.astype(F32), axis=0, keepdims=True), N_EXPERTS - 1.0)
        n_used = jnp.sum(jnp.where(lane8[0:1] == N_EXPERTS - 1, pend[0:1], 0.0), axis=1,
                         keepdims=True) * (1.0 / MOE_ROWS)
        blk_lane = lax.broadcasted_iota(jnp.int32, (1, nbp), 1)
        plan_ref[:, LANES:] = jnp.where(blk_lane == nbp - 1, n_used, block_e).astype(jnp.int32)

        starts = (pend - padded)[0:1]
        start_of = [jnp.sum(jnp.where(lane8[0:1] == k, starts, 0.0), axis=1, keepdims=True)
                    for k in range(N_EXPERTS)]

        def tile_dest(tt, carry):
            er = er_ref[tt]
            for slot in range(2):
                e = er[:, slot * rows:(slot + 1) * rows]
                rank = er[:, (2 + slot) * rows:(3 + slot) * rows]
                base = jnp.zeros((1, rows), F32)
                for k in range(N_EXPERTS):
                    base = jnp.where(e == k, start_of[k], base)
                dest_ref[tt, :, slot * rows:(slot + 1) * rows] = base.astype(jnp.int32) + rank
            return carry

        lax.fori_loop(0, pl.num_programs(0), tile_dest, 0)


def _plan_width(n):
    n_blocks = (2 * n + N_EXPERTS * MOE_ROWS) // MOE_ROWS
    return LANES + (n_blocks // LANES + 1) * LANES


def _router(attn, w_o, x2, g, w_pad, b_pad):
    n, d = x2.shape
    rows = MOVE_ROWS
    tiles = n // rows
    pw = _plan_width(n)
    return pl.pallas_call(
        _router_kernel,
        grid=(tiles,),
        in_specs=[pl.BlockSpec((rows, attn.shape[1]), lambda t: (t, 0)),
                  pl.BlockSpec(w_o.shape, lambda t: (0, 0)),
                  pl.BlockSpec((rows, d), lambda t: (t, 0)),
                  pl.BlockSpec((1, d), lambda t: (0, 0)),
                  pl.BlockSpec((d, LANES), lambda t: (0, 0)),
                  pl.BlockSpec((1, LANES), lambda t: (0, 0))],
        out_specs=[pl.BlockSpec((rows, d), lambda t: (t, 0)),
                   pl.BlockSpec((rows, LANES), lambda t: (t, 0)),
                   pl.BlockSpec((tiles, 1, 2 * rows), lambda t: (0, 0, 0)),
                   pl.BlockSpec((1, pw), lambda t: (0, 0))],
        out_shape=[jax.ShapeDtypeStruct((n, d), F32),
                   jax.ShapeDtypeStruct((n, LANES), F32),
                   jax.ShapeDtypeStruct((tiles, 1, 2 * rows), jnp.int32),
                   jax.ShapeDtypeStruct((1, pw), jnp.int32)],
        scratch_shapes=[pltpu.VMEM((1, LANES), F32), pltpu.VMEM((tiles, 1, 4 * rows), jnp.int32)],
        compiler_params=_params("arbitrary"),
        name="router",
    )(attn, w_o, x2, g, w_pad, b_pad)


def _row_copy(src, src_row, dst, dst_row, sem):
    return pltpu.make_async_copy(src.at[pl.ds(src_row, 1)], dst.at[pl.ds(dst_row, 1)], sem)


def _dest_rows(dest_ref, r, rows):
    return dest_ref[0, 0, r], dest_ref[0, 0, rows + r]


def _dispatch_kernel(dest_ref, plan_ref, x_ref, xs_hbm, zero_ref, sem, zsem):
    rows = x_ref.shape[0]

    @pl.when(pl.program_id(0) == 0)
    def _():
        zero_ref[...] = jnp.zeros_like(zero_ref)

        def fill_expert(e, total):
            count = plan_ref[N_EXPERTS + e]
            first = plan_ref[e] + count
            pad = (-count) & (MOE_ROWS - 1)

            def fill(r, c):
                _row_copy(zero_ref, 0, xs_hbm, first + r, zsem).start()
                return c

            def fill_many(q, c):
                for u in range(ISSUE_UNROLL):
                    _row_copy(zero_ref, 0, xs_hbm, first + q * ISSUE_UNROLL + u, zsem).start(priority=u % 2)
                return c

            bulk = pad // ISSUE_UNROLL
            lax.fori_loop(0, bulk, fill_many, 0)
            lax.fori_loop(bulk * ISSUE_UNROLL, pad, fill, 0)
            return total + pad

        total = lax.fori_loop(0, N_EXPERTS, fill_expert, 0)

        def block_copy(b):
            start = pl.multiple_of(b * MOE_ROWS, MOE_ROWS)
            return pltpu.make_async_copy(zero_ref, xs_hbm.at[pl.ds(start, MOE_ROWS)], zsem)

        def drain_rows(r, c):
            block_copy(0).wait()
            return c

        def drain(r, c):
            _row_copy(zero_ref, 0, xs_hbm, 0, zsem).wait()
            return c

        whole = total // MOE_ROWS
        lax.fori_loop(0, whole, drain_rows, 0)
        lax.fori_loop(whole * MOE_ROWS, total, drain, 0)

        n_used = plan_ref[plan_ref.shape[0] - 1]
        n_blocks = xs_hbm.shape[0] // MOE_ROWS

        def fill_block(b, c):
            block_copy(b).start()
            return c

        def drain_block(b, c):
            block_copy(b).wait()
            return c

        lax.fori_loop(n_used, n_blocks, fill_block, 0)
        lax.fori_loop(n_used, n_blocks, drain_block, 0)

    def start(r, c):
        d0, d1 = _dest_rows(dest_ref, r, rows)
        _row_copy(x_ref, r, xs_hbm, d0, sem).start()
        _row_copy(x_ref, r, xs_hbm, d1, sem).start(priority=1)
        return c

    lax.fori_loop(0, rows, start, 0, unroll=ISSUE_UNROLL)
    for _ in range(2):
        pltpu.make_async_copy(x_ref, xs_hbm.at[pl.ds(0, rows)], sem).wait()


def _dispatch(x2, dest, plan, cap):
    n, d = x2.shape
    rows = MOVE_ROWS
    any_spec = pl.BlockSpec(memory_space=pl.ANY)
    smem = pl.BlockSpec(memory_space=pltpu.SMEM)
    return pl.pallas_call(
        _dispatch_kernel,
        grid=(n // rows,),
        in_specs=[pl.BlockSpec((1, 1, 2 * rows), lambda t: (t, 0, 0), memory_space=pltpu.SMEM),
                  smem,
                  pl.BlockSpec((rows, d), lambda t: (t, 0))],
        out_specs=any_spec,
        out_shape=jax.ShapeDtypeStruct((cap, d), x2.dtype),
        scratch_shapes=[pltpu.VMEM((MOE_ROWS, d), x2.dtype), pltpu.SemaphoreType.DMA(()),
                        pltpu.SemaphoreType.DMA(())],
        compiler_params=pltpu.CompilerParams(dimension_semantics=("arbitrary",),
                                             has_side_effects=True, vmem_limit_bytes=VMEM_LIMIT),
        name="moe_dispatch",
    )(dest, plan, x2)


def _expert_kernel(plan_ref, xs_ref, g_ref, wg_ref, wu_ref, wd_ref, ys_ref, wgb, wub, wdb):
    b = pl.program_id(0)
    e = plan_ref[LANES + b]
    prev = plan_ref[LANES + jnp.maximum(b - 1, 0)]
    n_used = plan_ref[plan_ref.shape[0] - 1]

    @pl.when((b == 0) | (e != prev))
    def _():
        wgb[...] = wg_ref[...].astype(BF16)
        wub[...] = wu_ref[...].astype(BF16)
        wdb[...] = wd_ref[...].astype(BF16)

    @pl.when(b < n_used)
    def _():
        xn = _rms(xs_ref[...], g_ref[...], NORM_EPS).astype(BF16)
        hg = jnp.dot(xn, wgb[...], preferred_element_type=F32)
        hu = jnp.dot(xn, wub[...], preferred_element_type=F32)
        hid = (hg / (1.0 + jnp.exp(-hg))) * hu
        ys_ref[...] = jnp.dot(hid.astype(BF16), wdb[...], preferred_element_type=F32)

    @pl.when(b >= n_used)
    def _():
        ys_ref[...] = jnp.zeros_like(ys_ref)


def _experts(xs, g, w_gate, w_up, w_down, plan, layer):
    cap, d = xs.shape
    de = w_gate.shape[-1]
    rows = MOE_ROWS

    def expert_block(b, plan):
        return (layer, plan[LANES + b], 0, 0)

    grid_spec = pltpu.PrefetchScalarGridSpec(
        num_scalar_prefetch=1,
        grid=(cap // rows,),
        in_specs=[pl.BlockSpec((rows, d), lambda b, plan: (jnp.where(b < plan[plan.shape[0] - 1], b, 0), 0)),
                  pl.BlockSpec((1, d), lambda b, plan: (0, 0)),
                  pl.BlockSpec((None, None, d, de), expert_block),
                  pl.BlockSpec((None, None, d, de), expert_block),
                  pl.BlockSpec((None, None, de, d), expert_block)],
        out_specs=pl.BlockSpec((rows, d), lambda b, plan: (b, 0)),
        scratch_shapes=[pltpu.VMEM((d, de), BF16), pltpu.VMEM((d, de), BF16), pltpu.VMEM((de, d), BF16)],
    )
    return pl.pallas_call(
        _expert_kernel,
        grid_spec=grid_spec,
        out_shape=jax.ShapeDtypeStruct((cap, d), F32),
        compiler_params=_params("arbitrary"),
        name="moe_experts",
    )(plan, xs, g, w_gate, w_up, w_down)


def _combine_kernel(dest_ref, x_ref, meta_ref, g_ref, ys_hbm, o_ref, ybuf, sem, *, final_norm):
    rows = x_ref.shape[0]

    def start(r, c):
        d0, d1 = _dest_rows(dest_ref, r, rows)
        _row_copy(ys_hbm, d0, ybuf.at[0], r, sem).start()
        _row_copy(ys_hbm, d1, ybuf.at[1], r, sem).start(priority=1)
        return c

    lax.fori_loop(0, rows, start, 0, unroll=ISSUE_UNROLL)
    for k in range(2):
        pltpu.make_async_copy(ys_hbm.at[pl.ds(0, rows)], ybuf.at[k], sem).wait()
    meta = meta_ref[...]
    out = x_ref[...] + (meta[:, 4:5] * ybuf[0] + meta[:, 5:6] * ybuf[1])
    if final_norm:
        out = _rms(out, g_ref[...], NORM_EPS)
    o_ref[...] = out


def _combine(x2, meta, dest, ys, g, final_norm):
    n, d = x2.shape
    rows = MOVE_ROWS
    return pl.pallas_call(
        functools.partial(_combine_kernel, final_norm=final_norm),
        grid=(n // rows,),
        in_specs=[pl.BlockSpec((1, 1, 2 * rows), lambda t: (t, 0, 0), memory_space=pltpu.SMEM),
                  pl.BlockSpec((rows, d), lambda t: (t, 0)),
                  pl.BlockSpec((rows, LANES), lambda t: (t, 0)),
                  pl.BlockSpec((1, d), lambda t: (0, 0)),
                  pl.BlockSpec(memory_space=pl.ANY)],
        out_specs=pl.BlockSpec((rows, d), lambda t: (t, 0)),
        out_shape=jax.ShapeDtypeStruct((n, d), F32),
        scratch_shapes=[pltpu.VMEM((2, rows, d), F32), pltpu.SemaphoreType.DMA(())],
        compiler_params=_params("arbitrary"),
        name="moe_combine",
    )(dest, x2, meta, g, ys)


def _attn_out_moe(attn, w_o, x2, g, w_group, b_group, w_router, b_router, w_gate, w_up, w_down, layer,
                  g_final, final_norm):
    n, d = x2.shape
    pad = LANES - N_GROUPS - N_EXPERTS
    w_pad = jnp.concatenate([w_group, w_router, jnp.zeros((d, pad), F32)], axis=1).astype(BF16)
    b_pad = jnp.concatenate([b_group, b_router, jnp.zeros((pad,), F32)])[None, :]
    x2, meta, dest, plan = _router(attn, w_o, x2, g, w_pad, b_pad)
    plan = plan.reshape(-1)
    cap = 2 * n + N_EXPERTS * MOE_ROWS
    xs = _dispatch(x2, dest, plan, cap)
    ys = _experts(xs, g, w_gate, w_up, w_down, plan, layer)
    return _combine(x2, meta, dest, ys, g_final, final_norm)


def _diff_attn_kernel(cp_ref, lam_ref, subg_ref, qt_ref, k_ref, vt_ref, o_ref,
                      m_ref, acc_ref, qx_ref, s_ref, fix_ref, kaug_ref, km_ref, *, lambda_init):
    h = pl.program_id(1)
    i = pl.program_id(2)
    tq = qt_ref.shape[-1]
    tk = DIFF_K
    per_q = tq // tk
    n_groups = k_ref.shape[0] // tq
    cp = [cp_ref[3 * h + p] for p in range(3)]
    slope2 = cp[0] + cp[1] + cp[2]

    @pl.when(i == 0)
    def _():
        kl = lax.broadcasted_iota(jnp.int32, (tk, LANES), 1)
        kr = lax.broadcasted_iota(jnp.int32, (tk, LANES), 0).astype(F32)
        qr = lax.broadcasted_iota(jnp.int32, (LANES, tq), 0)
        ql = lax.broadcasted_iota(jnp.int32, (LANES, tq), 1)
        di_lo = (ql & 255).astype(F32)
        di_hi = (ql >> 8).astype(F32)
        kaug = jnp.zeros((tk, LANES), F32)
        qaug = jnp.zeros((LANES, tq), F32)
        for p in range(3):
            kaug = jnp.where(kl == p, -cp[p], kaug)
            kaug = jnp.where(kl == 3 + p, -256.0 * cp[p], kaug)
            kaug = jnp.where(kl == 6 + p, kr, kaug)
            qaug = jnp.where(qr == p, di_lo, qaug)
            qaug = jnp.where(qr == 3 + p, di_hi, qaug)
            qaug = jnp.where(qr == 6 + p, cp[p], qaug)
        kaug_ref[...] = kaug.astype(BF16)
        for c in range(2):
            qx_ref[0, c, 2 * HEAD_DIM:, :] = qaug.astype(BF16)
            qx_ref[1, c, 2 * HEAD_DIM:, :] = (-qaug).astype(BF16)
        for u in range(per_q):
            rel = (lax.broadcasted_iota(jnp.int32, (tk, tq), 0)
                   - lax.broadcasted_iota(jnp.int32, (tk, tq), 1)) + u * tk
            fix_ref[u] = (2.0 * slope2) * jnp.maximum(rel, 0).astype(F32)
        lane_k = lax.broadcasted_iota(jnp.int32, (1, LANES), 1)
        sel_r = lax.broadcasted_iota(jnp.int32, (LANES, LANES), 0)
        sel_c = lax.broadcasted_iota(jnp.int32, (LANES, LANES), 1)
        selector = ((sel_r >= sel_c * HEAD_DIM) & (sel_r < (sel_c + 1) * HEAD_DIM)).astype(BF16)
        group_norm = []
        for g in range(n_groups):
            kf = k_ref[g * tq:(g + 1) * tq, :].astype(F32)
            norms2 = jnp.dot((kf * kf).astype(BF16), selector, preferred_element_type=F32)
            top = jnp.max(norms2, axis=0, keepdims=True)
            group_norm.append([jnp.sqrt(jnp.sum(jnp.where(lane_k == c, top, 0.0), axis=1, keepdims=True))
                               * NORM_SLACK for c in range(2)])
        for c in range(2):
            running = jnp.zeros((1, 1), F32)
            row = jnp.zeros((1, LANES), F32)
            for g in range(n_groups):
                running = jnp.maximum(running, group_norm[g][c])
                row = jnp.where(lane_k == g, running, row)
            km_ref[0, c] = row
            running = jnp.zeros((1, 1), F32)
            row = jnp.zeros((1, LANES), F32)
            for g in reversed(range(n_groups)):
                running = jnp.maximum(running, group_norm[g][c])
                row = jnp.where(lane_k == g, running, row)
            km_ref[1, c] = row

    zeros = jnp.zeros((HEAD_DIM, tq), BF16)
    q_maps = (jnp.concatenate([qt_ref[0], zeros], axis=0),
              jnp.concatenate([zeros, qt_ref[1]], axis=0))
    for c in range(2):
        qx_ref[0, c, :2 * HEAD_DIM, :] = q_maps[c]
        qx_ref[1, c, :2 * HEAD_DIM, :] = q_maps[c]
    m_ref[...] = jnp.full_like(m_ref, NEG)
    acc_ref[...] = jnp.zeros_like(acc_ref)
    kaug = kaug_ref[...]

    n_k = n_groups * per_q

    def qk_stage(j, slot):
        jc = jnp.clip(j, 0, n_k - 1)
        start = pl.multiple_of(jc * tk, tk)
        kb = jnp.concatenate([k_ref[pl.ds(start, tk), :], kaug], axis=1)
        after = (jc >= (i + 1) * per_q).astype(jnp.int32)
        for c in range(2):
            s_ref[slot, c] = jnp.dot(kb, qx_ref[after, c], preferred_element_type=F32)

    ones = jnp.ones((SUM_ROWS, tk), BF16)

    def softmax_stage(j, slot, after, fix):
        start = pl.multiple_of(j * tk, tk)
        vb = jnp.concatenate([vt_ref[:, pl.ds(start, tk)], ones], axis=0)
        delta = (i * tq - start).astype(F32)
        off = slope2 * (-delta if after else delta)
        for c in range(2):
            s = s_ref[slot, c]
            if fix is not None:
                s = s - fix_ref[fix]
            m_old = m_ref[c]
            m_new = jnp.maximum(m_old, jnp.max(s, axis=0, keepdims=True) - off)
            alpha = jnp.exp2(m_old - m_new)
            p = jnp.exp2(s - (m_new + off))
            acc_ref[c] = alpha * acc_ref[c] + jnp.dot(vb, p.astype(BF16), preferred_element_type=F32)
            m_ref[c] = m_new

    def fixed_max_stage(j, slot, after):
        start = pl.multiple_of(j * tk, tk)
        vb = jnp.concatenate([vt_ref[:, pl.ds(start, tk)], ones], axis=0)
        delta = (i * tq - start).astype(F32)
        off = slope2 * (-delta if after else delta)
        for c in range(2):
            p = jnp.exp2(s_ref[slot, c] - (m_ref[c] + off))
            acc_ref[c] = acc_ref[c] + jnp.dot(vb, p.astype(BF16), preferred_element_type=F32)

    def pair(j0, next_j0, after, diagonal, fixed_max=False):
        def stage(j, slot, fix):
            if fixed_max:
                fixed_max_stage(j, slot, after)
            else:
                softmax_stage(j, slot, after, fix)
        qk_stage(j0 + 1, 1)
        stage(j0, 0, 0 if diagonal else None)
        qk_stage(next_j0, 0)
        stage(j0 + 1, 1, 1 if diagonal else None)

    def run_groups(first, step, count, after, then_group, fixed_max):
        def body(unroll, base):
            def run(t, carry):
                for u in range(unroll):
                    done = base + t * unroll + u
                    g = first + step * done
                    nxt = jnp.where(done == count - 1, then_group, g + step)
                    pair(g * per_q, nxt * per_q, after, False, fixed_max)
                return carry
            return run
        done = 0
        for unroll in (DIFF_UNROLL, 2, 1):
            trips = (count - done) // unroll
            lax.fori_loop(0, trips, body(unroll, done), 0)
            done = done + trips * unroll

    qk_stage(i * per_q, 0)
    pair(i * per_q, (i - 1) * per_q, False, True)

    qn = [jnp.sqrt(jnp.sum(jnp.square(qt_ref[c].astype(F32)), axis=0, keepdims=True)) for c in range(2)]

    step_t = lax.broadcasted_iota(jnp.int32, (n_groups, 1), 0)
    step_grid = lax.broadcasted_iota(jnp.int32, (n_groups, LANES), 0)
    lane_grid = lax.broadcasted_iota(jnp.int32, (n_groups, LANES), 1)
    penalty = slope2 * (step_t * tq + 1).astype(F32)

    def groups_needed(side, available):
        group_of_step = (i - 1 - step_grid) if side == 0 else (i + 1 + step_grid)
        pick = lane_grid == group_of_step
        need = None
        excess = jnp.full((1, tq), NEG, F32)
        for c in range(2):
            km = jnp.sum(jnp.where(pick, km_ref[side, c], 0.0), axis=1, keepdims=True)
            bound = qn[c] * km - penalty
            reach = bound >= (m_ref[c] - SKIP_MARGIN)
            need = reach if need is None else (need | reach)
            excess = jnp.maximum(excess, bound[0:1] - m_ref[c])
        cnt = jnp.sum(jnp.where(need & (step_t < available), 1.0, 0.0), axis=0, keepdims=True)
        return jnp.max(cnt).astype(jnp.int32), jnp.max(excess) <= FAST_MARGIN

    n_before, calm_before = groups_needed(0, i)
    n_after, calm_after = groups_needed(1, n_groups - 1 - i)

    def run_side(first, step, count, after, then_group, calm):
        @pl.when(calm)
        def _():
            run_groups(first, step, count, after, then_group, True)

        @pl.when(jnp.logical_not(calm))
        def _():
            run_groups(first, step, count, after, then_group, False)

    run_side(i - 1, -1, n_before, False, i + 1, calm_before)

    @pl.when((n_before == 0) & (n_after > 0))
    def _():
        qk_stage((i + 1) * per_q, 0)

    run_side(i + 1, 1, n_after, True, i + 1, calm_after)

    lam = lam_ref[...]
    lam_full = (jnp.exp(jnp.sum(lam[0:1] * lam[1:2], axis=-1, keepdims=True))
                - jnp.exp(jnp.sum(lam[2:3] * lam[3:4], axis=-1, keepdims=True)) + lambda_init)
    vdim = vt_ref.shape[0]
    o = (acc_ref[0, :vdim] / acc_ref[0, vdim:vdim + 1]
         - lam_full * (acc_ref[1, :vdim] / acc_ref[1, vdim:vdim + 1]))
    ms = jnp.mean(o * o, axis=0, keepdims=True)
    on = o * lax.rsqrt(ms + SUBLN_EPS) * subg_ref[...] * (1.0 - lambda_init)
    o_ref[...] = on.T.astype(BF16)


def _slope_pieces(slopes):
    s2 = slopes * LOG2E
    c1 = s2.astype(BF16).astype(F32)
    c2 = (s2 - c1).astype(BF16).astype(F32)
    c3 = (s2 - c1 - c2).astype(BF16).astype(F32)
    return jnp.stack([c1, c2, c3], axis=1).reshape(-1)


def _diff_attn(qt, k, vt, slopes, lam, subg, lambda_init):
    batch, heads, _, _, seq = qt.shape
    vdim = vt.shape[2]
    tq = DIFF_Q
    assert tq == 2 * 256 and DIFF_K == 256 and seq % tq == 0
    slopes = _slope_pieces(slopes)
    smem = pl.BlockSpec(memory_space=pltpu.SMEM)
    return pl.pallas_call(
        functools.partial(_diff_attn_kernel, lambda_init=lambda_init),
        grid=(batch, heads, seq // tq),
        in_specs=[smem,
                  pl.BlockSpec(lam.shape, lambda b, h, i: (0, 0)),
                  pl.BlockSpec(subg.shape, lambda b, h, i: (0, 0)),
                  pl.BlockSpec((None, None, 2, HEAD_DIM, tq), lambda b, h, i: (b, h, 0, 0, i)),
                  pl.BlockSpec((None, seq, 2 * HEAD_DIM), lambda b, h, i: (b, 0, h)),
                  pl.BlockSpec((None, None, vdim, seq), lambda b, h, i: (b, h, 0, 0))],
        out_specs=pl.BlockSpec((None, tq, vdim), lambda b, h, i: (b, i, h)),
        out_shape=jax.ShapeDtypeStruct((batch, seq, heads * vdim), BF16),
        scratch_shapes=[pltpu.VMEM((2, 1, tq), F32),
                        pltpu.VMEM((2, vdim + SUM_ROWS, tq), F32),
                        pltpu.VMEM((2, 2, 2 * HEAD_DIM + LANES, tq), BF16),
                        pltpu.VMEM((2, 2, DIFF_K, tq), F32),
                        pltpu.VMEM((tq // DIFF_K, DIFF_K, tq), F32),
                        pltpu.VMEM((DIFF_K, LANES), BF16),
                        pltpu.VMEM((2, 2, 1, LANES), F32)],
        compiler_params=_params("arbitrary", "arbitrary", "arbitrary"),
        name="diff_attn",
    )(slopes, lam, subg, qt, k, vt)


def _alibi_slopes(n_heads):
    h = jnp.arange(1, n_heads + 1, dtype=F32)
    return jnp.exp2(-8.0 * h / n_heads)


def _lambda_init(layer_idx):
    return 0.8 - 0.6 * math.exp(-0.3 * layer_idx)


def kernel(x, norm_mix, norm_ffn, swa_w_qkv, swa_w_o, swa_sink, diff_w_qkv, diff_w_o, diff_lambda,
           diff_subln, moe_w_group, moe_b_group, moe_w_router, moe_b_router, moe_w_gate, moe_w_up,
           moe_w_down, final_norm):
    batch, seq, d = x.shape
    n = batch * seq
    depth = norm_mix.shape[0]
    x2 = x.reshape(n, d)
    q_scale = HEAD_DIM ** -0.5 * LOG2E
    for layer in range(depth):
        j = layer // 2
        g_mix = norm_mix[layer][None, :]
        x3 = x2.reshape(batch, seq, d)
        if layer % 2 == 0:
            n_heads = SWA_KV_HEADS * SWA_GROUP
            qw = n_heads * HEAD_DIM
            kw = SWA_KV_HEADS * HEAD_DIM
            w = swa_w_qkv[j]
            wq = (w[:, :qw] * q_scale).astype(BF16)
            wk = w[:, qw:qw + kw].astype(BF16)
            wv = w[:, qw + kw:].astype(BF16)
            qt, k, vt = _qkv_proj(x3, g_mix, wq, wk, wv, "swa_proj")
            attn = _swa_attn(qt, k, vt, _alibi_slopes(n_heads), swa_sink[j].astype(F32))
            w_o = swa_w_o[j].astype(BF16)
        else:
            heads = d // (2 * HEAD_DIM)
            w = diff_w_qkv[j]
            wq = (w[:, :d] * q_scale).astype(BF16)
            wk = w[:, d:2 * d].astype(BF16)
            wv = w[:, 2 * d:].astype(BF16)
            qt, k, vt = _qkv_proj(x3, g_mix, wq, wk, wv, "diff_proj")
            qt = qt.reshape(batch, heads, 2, HEAD_DIM, seq)
            vt = vt.reshape(batch, heads, 2 * HEAD_DIM, seq)
            attn = _diff_attn(qt, k, vt, _alibi_slopes(heads), diff_lambda[j].astype(F32),
                              diff_subln[j].astype(F32)[:, None], _lambda_init(layer))
            attn = attn.reshape(n, d)
            w_o = diff_w_o[j].astype(BF16)
        last = layer == depth - 1
        x2 = _attn_out_moe(attn, w_o, x2, norm_ffn[layer][None, :], moe_w_group[layer], moe_b_group[layer],
                           moe_w_router[layer], moe_b_router[layer], moe_w_gate, moe_w_up, moe_w_down,
                           layer, final_norm[None, :], last)
    return x2.reshape(batch, seq, d)
```

```python
import functools
import math

import jax
import jax.numpy as jnp
from jax import lax
from jax.experimental import pallas as pl
from jax.experimental.pallas import tpu as pltpu

F32 = jnp.float32
BF16 = jnp.bfloat16

HEAD_DIM = 64
WINDOW = 128
SWA_KV_HEADS = 4
SWA_GROUP = 4
N_GROUPS = 4
EXPERTS_PER_GROUP = 8
N_EXPERTS = N_GROUPS * EXPERTS_PER_GROUP
NORM_EPS = 1e-6
SUBLN_EPS = 1e-5
NEG = -1e30
LANES = 128
LOG2E = math.log2(math.e)

PROJ_ROWS = 512
SWA_Q = 128
MOE_ROWS = 512
MOVE_ROWS = 512
ISSUE_UNROLL = 8
DIFF_Q = 512
DIFF_K = 256
DIFF_UNROLL = 4
SKIP_MARGIN = 152.0
NORM_SLACK = 1.0 + 2.0 ** -6
FAST_MARGIN = 64.0
SUM_ROWS = 16
VMEM_LIMIT = 56 * 1024 * 1024


def _params(*sem):
    return pltpu.CompilerParams(dimension_semantics=sem, vmem_limit_bytes=VMEM_LIMIT)


def _rms(x, g, eps):
    ms = jnp.mean(x * x, axis=-1, keepdims=True)
    return x * lax.rsqrt(ms + eps) * g


def _qkv_proj_kernel(x_ref, g_ref, wq_ref, wk_ref, wv_ref, qt_ref, k_ref, vt_ref, wqt_ref, wvt_ref):
    @pl.when((pl.program_id(0) == 0) & (pl.program_id(1) == 0))
    def _():
        wqt_ref[...] = wq_ref[...].T
        wvt_ref[...] = wv_ref[...].T

    xn = _rms(x_ref[...], g_ref[...], NORM_EPS).astype(BF16)
    nt = (((1,), (1,)), ((), ()))
    qt_ref[...] = lax.dot_general(wqt_ref[...], xn, nt, preferred_element_type=F32).astype(BF16)
    k_ref[...] = jnp.dot(xn, wk_ref[...], preferred_element_type=F32).astype(BF16)
    vt_ref[...] = lax.dot_general(wvt_ref[...], xn, nt, preferred_element_type=F32).astype(BF16)


def _qkv_proj(x3, g, wq, wk, wv, name):
    batch, seq, d = x3.shape
    rows = PROJ_ROWS
    qw, kw, vw = wq.shape[1], wk.shape[1], wv.shape[1]

    def full(w):
        return pl.BlockSpec(w.shape, lambda b, i: (0, 0))

    def transposed(width):
        return pl.BlockSpec((None, width, rows), lambda b, i: (b, 0, i))

    return pl.pallas_call(
        _qkv_proj_kernel,
        grid=(batch, seq // rows),
        in_specs=[pl.BlockSpec((None, rows, d), lambda b, i: (b, i, 0)),
                  pl.BlockSpec((1, d), lambda b, i: (0, 0)),
                  full(wq), full(wk), full(wv)],
        out_specs=[transposed(qw), pl.BlockSpec((None, rows, kw), lambda b, i: (b, i, 0)), transposed(vw)],
        out_shape=[jax.ShapeDtypeStruct((batch, qw, seq), BF16),
                   jax.ShapeDtypeStruct((batch, seq, kw), BF16),
                   jax.ShapeDtypeStruct((batch, vw, seq), BF16)],
        scratch_shapes=[pltpu.VMEM((qw, d), BF16), pltpu.VMEM((vw, d), BF16)],
        compiler_params=_params("arbitrary", "arbitrary"),
        name=name,
    )(x3, g, wq, wk, wv)


def _swa_attn_kernel(slope_ref, sink_ref, qt_ref, kp_ref, kc_ref, kn_ref, vp_ref, vc_ref, vn_ref, o_ref,
                     bias_ref, sinkrow_ref):
    first_step = (pl.program_id(0) == 0) & (pl.program_id(1) == 0)
    i = pl.program_id(1)
    last = pl.num_programs(1) - 1
    tq = SWA_Q
    nk = 3 * tq
    nl = SWA_GROUP * tq

    @pl.when(first_step)
    def _():
        row = lax.broadcasted_iota(jnp.int32, (nk, nl), 0)
        lane = lax.broadcasted_iota(jnp.int32, (nk, nl), 1)
        dist = jnp.abs(row - tq - (lane & (tq - 1)))
        distf = dist.astype(F32)
        lane1 = lax.broadcasted_iota(jnp.int32, (1, nl), 1)
        tq_bits = tq.bit_length() - 1
        for j in range(SWA_KV_HEADS):
            slope = jnp.zeros((nk, nl), F32)
            sink = jnp.zeros((1, nl), F32)
            for g in range(SWA_GROUP):
                h = j * SWA_GROUP + g
                slope = jnp.where((lane >> tq_bits) == g, slope_ref[h], slope)
                sink = jnp.where((lane1 >> tq_bits) == g, sink_ref[h], sink)
            sinkrow_ref[j] = sink
            bias = jnp.where(dist <= WINDOW, -slope * distf, NEG)
            bias_ref[0, j] = bias
            bias_ref[1, j] = jnp.where(row < tq, NEG, bias)
            bias_ref[2, j] = jnp.where(row >= 2 * tq, NEG, bias)

    edge = jnp.where(i == 0, 1, jnp.where(i == last, 2, 0))
    kwin = jnp.concatenate([kp_ref[...], kc_ref[...], kn_ref[...]], axis=0)
    kvw = kwin.shape[1]
    for j in range(SWA_KV_HEADS):
        rows = slice(j * HEAD_DIM, (j + 1) * HEAD_DIM)
        qj = jnp.concatenate([qt_ref[(j * SWA_GROUP + g) * HEAD_DIM:(j * SWA_GROUP + g + 1) * HEAD_DIM, :]
                              for g in range(SWA_GROUP)], axis=1)
        parts = []
        if j > 0:
            parts.append(jnp.zeros((j * HEAD_DIM, nl), BF16))
        parts.append(qj)
        if (j + 1) * HEAD_DIM < kvw:
            parts.append(jnp.zeros((kvw - (j + 1) * HEAD_DIM, nl), BF16))
        qpad = jnp.concatenate(parts, axis=0)
        s = jnp.dot(kwin, qpad, preferred_element_type=F32) + bias_ref[edge, j]
        sink = sinkrow_ref[j]
        m = jnp.maximum(jnp.max(s, axis=0, keepdims=True), sink)
        p = jnp.exp2(s - m)
        vwin = jnp.concatenate([vp_ref[rows, :], vc_ref[rows, :], vn_ref[rows, :]], axis=1)
        vwin = jnp.concatenate([vwin, jnp.ones((SUM_ROWS, nk), BF16)], axis=0)
        pv = jnp.dot(vwin, p.astype(BF16), preferred_element_type=F32)
        denom = pv[HEAD_DIM:HEAD_DIM + 1] + jnp.exp2(sink - m)
        ot = pv[:HEAD_DIM] / denom
        for g in range(SWA_GROUP):
            h = j * SWA_GROUP + g
            o_ref[:, h * HEAD_DIM:(h + 1) * HEAD_DIM] = ot[:, g * tq:(g + 1) * tq].T.astype(BF16)


def _swa_attn(qt, k, vt, slopes, sink):
    batch, qw, seq = qt.shape
    kvw = k.shape[2]
    nb = seq // SWA_Q
    smem = pl.BlockSpec(memory_space=pltpu.SMEM)

    def k_spec(shift):
        return pl.BlockSpec((None, SWA_Q, kvw), lambda b, i: (b, jnp.clip(i + shift, 0, nb - 1), 0))

    def v_spec(shift):
        return pl.BlockSpec((None, kvw, SWA_Q), lambda b, i: (b, 0, jnp.clip(i + shift, 0, nb - 1)))

    out = pl.pallas_call(
        _swa_attn_kernel,
        grid=(batch, nb),
        in_specs=[smem, smem,
                  pl.BlockSpec((None, qw, SWA_Q), lambda b, i: (b, 0, i)),
                  k_spec(-1), k_spec(0), k_spec(1), v_spec(-1), v_spec(0), v_spec(1)],
        out_specs=pl.BlockSpec((None, SWA_Q, qw), lambda b, i: (b, i, 0)),
        out_shape=jax.ShapeDtypeStruct((batch, seq, qw), BF16),
        scratch_shapes=[pltpu.VMEM((3, SWA_KV_HEADS, 3 * SWA_Q, SWA_GROUP * SWA_Q), F32),
                        pltpu.VMEM((SWA_KV_HEADS, 1, SWA_GROUP * SWA_Q), F32)],
        compiler_params=_params("arbitrary", "arbitrary"),
        name="swa_attn",
    )(slopes * LOG2E, sink * LOG2E, qt, k, k, k, vt, vt, vt)
    return out.reshape(batch * seq, qw)


def _router_kernel(a_ref, wo_ref, x_ref, g_ref, w_ref, b_ref, xo_ref, meta_ref, dest_ref, plan_ref,
                   carry_ref, er_ref):
    t = pl.program_id(0)
    rows = x_ref.shape[0]

    @pl.when(t == 0)
    def _():
        carry_ref[...] = jnp.zeros_like(carry_ref)

    x = x_ref[...] + jnp.dot(a_ref[...], wo_ref[...], preferred_element_type=F32)
    xo_ref[...] = x
    xn = _rms(x, g_ref[...], NORM_EPS).astype(BF16)
    logits = jnp.dot(xn, w_ref[...], preferred_element_type=F32) + b_ref[...]
    lane = lax.broadcasted_iota(jnp.int32, (rows, LANES), 1)

    is_group = lane < N_GROUPS
    gl = jnp.where(is_group, logits, NEG)
    gmax = jnp.max(gl, axis=-1, keepdims=True)
    gidx = jnp.min(jnp.where(gl == gmax, lane, LANES), axis=-1, keepdims=True)
    zg = jnp.sum(jnp.where(is_group, jnp.exp(gl - gmax), 0.0), axis=-1, keepdims=True)
    pg = 1.0 / zg

    lo = N_GROUPS + EXPERTS_PER_GROUP * gidx
    in_group = (lane >= lo) & (lane < lo + EXPERTS_PER_GROUP)
    el = jnp.where(in_group, logits, NEG)
    m0 = jnp.max(el, axis=-1, keepdims=True)
    i0 = jnp.min(jnp.where(el == m0, lane, LANES), axis=-1, keepdims=True)
    el1 = jnp.where(lane == i0, NEG, el)
    m1 = jnp.max(el1, axis=-1, keepdims=True)
    i1 = jnp.min(jnp.where(el1 == m1, lane, LANES), axis=-1, keepdims=True)
    r = jnp.exp(m1 - m0)
    w0 = pg / (1.0 + r)
    w1 = pg * r / (1.0 + r)
    e0 = i0 - N_GROUPS
    e1 = i1 - N_GROUPS

    sel0 = lane == e0
    sel1 = lane == e1
    onehot = (sel0 | sel1).astype(BF16)
    rr = lax.broadcasted_iota(jnp.int32, (rows, rows), 0)
    cc = lax.broadcasted_iota(jnp.int32, (rows, rows), 1)
    tri = (cc < rr).astype(BF16)
    before = jnp.dot(tri, onehot, preferred_element_type=F32) + carry_ref[...]
    r0 = jnp.sum(jnp.where(sel0, before, 0.0), axis=-1, keepdims=True)
    r1 = jnp.sum(jnp.where(sel1, before, 0.0), axis=-1, keepdims=True)
    carry_ref[...] = carry_ref[...] + jnp.sum(onehot.astype(F32), axis=0, keepdims=True)

    meta = jnp.where(lane == 0, e0.astype(F32), 0.0)
    meta = jnp.where(lane == 1, e1.astype(F32), meta)
    meta = jnp.where(lane == 2, r0, meta)
    meta = jnp.where(lane == 3, r1, meta)
    meta = jnp.where(lane == 4, w0, meta)
    meta = jnp.where(lane == 5, w1, meta)
    meta_ref[...] = meta
    metat = meta.T
    for f in range(4):
        er_ref[t, :, f * rows:(f + 1) * rows] = metat[f:f + 1, :].astype(jnp.int32)

    @pl.when(t == pl.num_programs(0) - 1)
    def _plan():
        counts = jnp.broadcast_to(carry_ref[...], (8, LANES))
        padded = jnp.floor((counts + (MOE_ROWS - 1)) * (1.0 / MOE_ROWS)) * MOE_ROWS
        lane8 = lax.broadcasted_iota(jnp.int32, (8, LANES), 1)
        pend = padded
        shift = 1
        while shift < LANES:
            pend = pend + jnp.where(lane8 >= shift, pltpu.roll(pend, shift, 1), 0.0)
            shift *= 2
        starts_counts = jnp.where(lane8 < N_EXPERTS, pend - padded, pltpu.roll(counts, N_EXPERTS, 1))
        plan_ref[:, 0:LANES] = starts_counts[0:1].astype(jnp.int32)
        sq_r = lax.broadcasted_iota(jnp.int32, (LANES, LANES), 0)
        sq_c = lax.broadcasted_iota(jnp.int32, (LANES, LANES), 1)
        pend_col = jnp.sum(jnp.where(sq_r == sq_c, jnp.broadcast_to(pend[0:1], (LANES, LANES)), 0.0),
                           axis=1, keepdims=True)
        nbp = plan_ref.shape[1] - LANES
        blk_row = lax.broadcasted_iota(jnp.int32, (LANES, nbp), 0)
        blk_start = (lax.broadcasted_iota(jnp.int32, (LANES, nbp), 1) * MOE_ROWS).astype(F32)
        ended = (pend_col <= blk_start) & (blk_row < N_EXPERTS)
        block_e = jnp.minimum(jnp.sum(ended.astype(F32), axis=0, keepdims=True), N_EXPERTS - 1.0)
        n_used = jnp.sum(jnp.where(lane8[0:1] == N_EXPERTS - 1, pend[0:1], 0.0), axis=1,
                         keepdims=True) * (1.0 / MOE_ROWS)
        blk_lane = lax.broadcasted_iota(jnp.int32, (1, nbp), 1)
        plan_ref[:, LANES:] = jnp.where(blk_lane == nbp - 1, n_used, block_e).astype(jnp.int32)

        starts = (pend - padded)[0:1]
        start_of = [jnp.sum(jnp.where(lane8[0:1] == k, starts, 0.0), axis=1, keepdims=True)
                    for k in range(N_EXPERTS)]

        def tile_dest(tt, carry):
            er = er_ref[tt]
            for slot in range(2):
                e = er[:, slot * rows:(slot + 1) * rows]
                rank = er[:, (2 + slot) * rows:(3 + slot) * rows]
                base = jnp.zeros((1, rows), F32)
                for k in range(N_EXPERTS):
                    base = jnp.where(e == k, start_of[k], base)
                dest_ref[tt, :, slot * rows:(slot + 1) * rows] = base.astype(jnp.int32) + rank
            return carry

        lax.fori_loop(0, pl.num_programs(0), tile_dest, 0)


def _plan_width(n):
    n_blocks = (2 * n + N_EXPERTS * MOE_ROWS) // MOE_ROWS
    return LANES + (n_blocks // LANES + 1) * LANES


def _router(attn, w_o, x2, g, w_pad, b_pad):
    n, d = x2.shape
    rows = MOVE_ROWS
    tiles = n // rows
    pw = _plan_width(n)
    return pl.pallas_call(
        _router_kernel,
        grid=(tiles,),
        in_specs=[pl.BlockSpec((rows, attn.shape[1]), lambda t: (t, 0)),
                  pl.BlockSpec(w_o.shape, lambda t: (0, 0)),
                  pl.BlockSpec((rows, d), lambda t: (t, 0)),
                  pl.BlockSpec((1, d), lambda t: (0, 0)),
                  pl.BlockSpec((d, LANES), lambda t: (0, 0)),
                  pl.BlockSpec((1, LANES), lambda t: (0, 0))],
        out_specs=[pl.BlockSpec((rows, d), lambda t: (t, 0)),
                   pl.BlockSpec((rows, LANES), lambda t: (t, 0)),
                   pl.BlockSpec((tiles, 1, 2 * rows), lambda t: (0, 0, 0)),
                   pl.BlockSpec((1, pw), lambda t: (0, 0))],
        out_shape=[jax.ShapeDtypeStruct((n, d), F32),
                   jax.ShapeDtypeStruct((n, LANES), F32),
                   jax.ShapeDtypeStruct((tiles, 1, 2 * rows), jnp.int32),
                   jax.ShapeDtypeStruct((1, pw), jnp.int32)],
        scratch_shapes=[pltpu.VMEM((1, LANES), F32), pltpu.VMEM((tiles, 1, 4 * rows), jnp.int32)],
        compiler_params=_params("arbitrary"),
        name="router",
    )(attn, w_o, x2, g, w_pad, b_pad)


def _row_copy(src, src_row, dst, dst_row, sem):
    return pltpu.make_async_copy(src.at[pl.ds(src_row, 1)], dst.at[pl.ds(dst_row, 1)], sem)


def _dest_rows(dest_ref, r, rows):
    return dest_ref[0, 0, r], dest_ref[0, 0, rows + r]


def _dispatch_kernel(dest_ref, plan_ref, x_ref, xs_hbm, zero_ref, sem, zsem):
    rows = x_ref.shape[0]

    @pl.when(pl.program_id(0) == 0)
    def _():
        zero_ref[...] = jnp.zeros_like(zero_ref)

        def fill_expert(e, total):
            count = plan_ref[N_EXPERTS + e]
            first = plan_ref[e] + count
            pad = (-count) & (MOE_ROWS - 1)

            def fill(r, c):
                _row_copy(zero_ref, 0, xs_hbm, first + r, zsem).start()
                return c

            def fill_many(q, c):
                for u in range(ISSUE_UNROLL):
                    _row_copy(zero_ref, 0, xs_hbm, first + q * ISSUE_UNROLL + u, zsem).start(priority=u % 2)
                return c

            bulk = pad // ISSUE_UNROLL
            lax.fori_loop(0, bulk, fill_many, 0)
            lax.fori_loop(bulk * ISSUE_UNROLL, pad, fill, 0)
            return total + pad

        total = lax.fori_loop(0, N_EXPERTS, fill_expert, 0)

        def block_copy(b):
            start = pl.multiple_of(b * MOE_ROWS, MOE_ROWS)
            return pltpu.make_async_copy(zero_ref, xs_hbm.at[pl.ds(start, MOE_ROWS)], zsem)

        def drain_rows(r, c):
            block_copy(0).wait()
            return c

        def drain(r, c):
            _row_copy(zero_ref, 0, xs_hbm, 0, zsem).wait()
            return c

        whole = total // MOE_ROWS
        lax.fori_loop(0, whole, drain_rows, 0)
        lax.fori_loop(whole * MOE_ROWS, total, drain, 0)

        n_used = plan_ref[plan_ref.shape[0] - 1]
        n_blocks = xs_hbm.shape[0] // MOE_ROWS

        def fill_block(b, c):
            block_copy(b).start()
            return c

        def drain_block(b, c):
            block_copy(b).wait()
            return c

        lax.fori_loop(n_used, n_blocks, fill_block, 0)
        lax.fori_loop(n_used, n_blocks, drain_block, 0)

    def start(r, c):
        d0, d1 = _dest_rows(dest_ref, r, rows)
        _row_copy(x_ref, r, xs_hbm, d0, sem).start()
        _row_copy(x_ref, r, xs_hbm, d1, sem).start(priority=1)
        return c

    lax.fori_loop(0, rows, start, 0, unroll=ISSUE_UNROLL)
    for _ in range(2):
        pltpu.make_async_copy(x_ref, xs_hbm.at[pl.ds(0, rows)], sem).wait()


def _dispatch(x2, dest, plan, cap):
    n, d = x2.shape
    rows = MOVE_ROWS
    any_spec = pl.BlockSpec(memory_space=pl.ANY)
    smem = pl.BlockSpec(memory_space=pltpu.SMEM)
    return pl.pallas_call(
        _dispatch_kernel,
        grid=(n // rows,),
        in_specs=[pl.BlockSpec((1, 1, 2 * rows), lambda t: (t, 0, 0), memory_space=pltpu.SMEM),
                  smem,
                  pl.BlockSpec((rows, d), lambda t: (t, 0))],
        out_specs=any_spec,
        out_shape=jax.ShapeDtypeStruct((cap, d), x2.dtype),
        scratch_shapes=[pltpu.VMEM((MOE_ROWS, d), x2.dtype), pltpu.SemaphoreType.DMA(()),
                        pltpu.SemaphoreType.DMA(())],
        compiler_params=pltpu.CompilerParams(dimension_semantics=("arbitrary",),
                                             has_side_effects=True, vmem_limit_bytes=VMEM_LIMIT),
        name="moe_dispatch",
    )(dest, plan, x2)


def _expert_kernel(plan_ref, xs_ref, g_ref, wg_ref, wu_ref, wd_ref, ys_ref, wgb, wub, wdb):
    b = pl.program_id(0)
    e = plan_ref[LANES + b]
    prev = plan_ref[LANES + jnp.maximum(b - 1, 0)]
    n_used = plan_ref[plan_ref.shape[0] - 1]

    @pl.when((b == 0) | (e != prev))
    def _():
        wgb[...] = wg_ref[...].astype(BF16)
        wub[...] = wu_ref[...].astype(BF16)
        wdb[...] = wd_ref[...].astype(BF16)

    @pl.when(b < n_used)
    def _():
        xn = _rms(xs_ref[...], g_ref[...], NORM_EPS).astype(BF16)
        hg = jnp.dot(xn, wgb[...], preferred_element_type=F32)
        hu = jnp.dot(xn, wub[...], preferred_element_type=F32)
        hid = (hg / (1.0 + jnp.exp(-hg))) * hu
        ys_ref[...] = jnp.dot(hid.astype(BF16), wdb[...], preferred_element_type=F32)

    @pl.when(b >= n_used)
    def _():
        ys_ref[...] = jnp.zeros_like(ys_ref)


def _experts(xs, g, w_gate, w_up, w_down, plan, layer):
    cap, d = xs.shape
    de = w_gate.shape[-1]
    rows = MOE_ROWS

    def expert_block(b, plan):
        return (layer, plan[LANES + b], 0, 0)

    grid_spec = pltpu.PrefetchScalarGridSpec(
        num_scalar_prefetch=1,
        grid=(cap // rows,),
        in_specs=[pl.BlockSpec((rows, d), lambda b, plan: (jnp.where(b < plan[plan.shape[0] - 1], b, 0), 0)),
                  pl.BlockSpec((1, d), lambda b, plan: (0, 0)),
                  pl.BlockSpec((None, None, d, de), expert_block),
                  pl.BlockSpec((None, None, d, de), expert_block),
                  pl.BlockSpec((None, None, de, d), expert_block)],
        out_specs=pl.BlockSpec((rows, d), lambda b, plan: (b, 0)),
        scratch_shapes=[pltpu.VMEM((d, de), BF16), pltpu.VMEM((d, de), BF16), pltpu.VMEM((de, d), BF16)],
    )
    return pl.pallas_call(
        _expert_kernel,
        grid_spec=grid_spec,
        out_shape=jax.ShapeDtypeStruct((cap, d), F32),
        compiler_params=_params("arbitrary"),
        name="moe_experts",
    )(plan, xs, g, w_gate, w_up, w_down)


def _combine_kernel(dest_ref, next_ref, x_ref, meta_ref, g_ref, ys_hbm, o_ref, ybuf, sem, *, final_norm):
    rows = x_ref.shape[0]
    t = pl.program_id(0)
    last = pl.num_programs(0) - 1

    def gather(index_ref, slot):
        def start(r, c):
            d0, d1 = _dest_rows(index_ref, r, rows)
            _row_copy(ys_hbm, d0, ybuf.at[slot, 0], r, sem.at[slot]).start()
            _row_copy(ys_hbm, d1, ybuf.at[slot, 1], r, sem.at[slot]).start(priority=1)
            return c
        lax.fori_loop(0, rows, start, 0, unroll=ISSUE_UNROLL)

    slot = t % 2

    @pl.when(t == 0)
    def _():
        gather(dest_ref, 0)

    @pl.when(t < last)
    def _():
        gather(next_ref, 1 - slot)

    for k in range(2):
        pltpu.make_async_copy(ys_hbm.at[pl.ds(0, rows)], ybuf.at[slot, k], sem.at[slot]).wait()
    meta = meta_ref[...]
    out = x_ref[...] + (meta[:, 4:5] * ybuf[slot, 0] + meta[:, 5:6] * ybuf[slot, 1])
    if final_norm:
        out = _rms(out, g_ref[...], NORM_EPS)
    o_ref[...] = out


def _combine(x2, meta, dest, ys, g, final_norm):
    n, d = x2.shape
    rows = MOVE_ROWS
    steps = n // rows
    return pl.pallas_call(
        functools.partial(_combine_kernel, final_norm=final_norm),
        grid=(n // rows,),
        in_specs=[pl.BlockSpec((1, 1, 2 * rows), lambda t: (t, 0, 0), memory_space=pltpu.SMEM),
                  pl.BlockSpec((1, 1, 2 * rows), lambda t: (jnp.minimum(t + 1, steps - 1), 0, 0),
                               memory_space=pltpu.SMEM),
                  pl.BlockSpec((rows, d), lambda t: (t, 0)),
                  pl.BlockSpec((rows, LANES), lambda t: (t, 0)),
                  pl.BlockSpec((1, d), lambda t: (0, 0)),
                  pl.BlockSpec(memory_space=pl.ANY)],
        out_specs=pl.BlockSpec((rows, d), lambda t: (t, 0)),
        out_shape=jax.ShapeDtypeStruct((n, d), F32),
        scratch_shapes=[pltpu.VMEM((2, 2, rows, d), F32), pltpu.SemaphoreType.DMA((2,))],
        compiler_params=_params("arbitrary"),
        name="moe_combine",
    )(dest, dest, x2, meta, g, ys)


def _attn_out_moe(attn, w_o, x2, g, w_group, b_group, w_router, b_router, w_gate, w_up, w_down, layer,
                  g_final, final_norm):
    n, d = x2.shape
    pad = LANES - N_GROUPS - N_EXPERTS
    w_pad = jnp.concatenate([w_group, w_router, jnp.zeros((d, pad), F32)], axis=1).astype(BF16)
    b_pad = jnp.concatenate([b_group, b_router, jnp.zeros((pad,), F32)])[None, :]
    x2, meta, dest, plan = _router(attn, w_o, x2, g, w_pad, b_pad)
    plan = plan.reshape(-1)
    cap = 2 * n + N_EXPERTS * MOE_ROWS
    xs = _dispatch(x2, dest, plan, cap)
    ys = _experts(xs, g, w_gate, w_up, w_down, plan, layer)
    return _combine(x2, meta, dest, ys, g_final, final_norm)


def _diff_attn_kernel(cp_ref, lam_ref, subg_ref, qt_ref, k_ref, vt_ref, o_ref,
                      m_ref, acc_ref, qx_ref, s_ref, fix_ref, kaug_ref, km_ref, *, lambda_init):
    h = pl.program_id(1)
    i = pl.program_id(2)
    tq = qt_ref.shape[-1]
    tk = DIFF_K
    per_q = tq // tk
    n_groups = k_ref.shape[0] // tq
    cp = [cp_ref[3 * h + p] for p in range(3)]
    slope2 = cp[0] + cp[1] + cp[2]

    @pl.when(i == 0)
    def _():
        kl = lax.broadcasted_iota(jnp.int32, (tk, LANES), 1)
        kr = lax.broadcasted_iota(jnp.int32, (tk, LANES), 0).astype(F32)
        qr = lax.broadcasted_iota(jnp.int32, (LANES, tq), 0)
        ql = lax.broadcasted_iota(jnp.int32, (LANES, tq), 1)
        di_lo = (ql & 255).astype(F32)
        di_hi = (ql >> 8).astype(F32)
        kaug = jnp.zeros((tk, LANES), F32)
        qaug = jnp.zeros((LANES, tq), F32)
        for p in range(3):
            kaug = jnp.where(kl == p, -cp[p], kaug)
            kaug = jnp.where(kl == 3 + p, -256.0 * cp[p], kaug)
            kaug = jnp.where(kl == 6 + p, kr, kaug)
            qaug = jnp.where(qr == p, di_lo, qaug)
            qaug = jnp.where(qr == 3 + p, di_hi, qaug)
            qaug = jnp.where(qr == 6 + p, cp[p], qaug)
        kaug_ref[...] = kaug.astype(BF16)
        for c in range(2):
            qx_ref[0, c, 2 * HEAD_DIM:, :] = qaug.astype(BF16)
            qx_ref[1, c, 2 * HEAD_DIM:, :] = (-qaug).astype(BF16)
        for u in range(per_q):
            rel = (lax.broadcasted_iota(jnp.int32, (tk, tq), 0)
                   - lax.broadcasted_iota(jnp.int32, (tk, tq), 1)) + u * tk
            fix_ref[u] = (2.0 * slope2) * jnp.maximum(rel, 0).astype(F32)
        lane_k = lax.broadcasted_iota(jnp.int32, (1, LANES), 1)
        sel_r = lax.broadcasted_iota(jnp.int32, (LANES, LANES), 0)
        sel_c = lax.broadcasted_iota(jnp.int32, (LANES, LANES), 1)
        selector = ((sel_r >= sel_c * HEAD_DIM) & (sel_r < (sel_c + 1) * HEAD_DIM)).astype(BF16)
        group_norm = []
        for g in range(n_groups):
            kf = k_ref[g * tq:(g + 1) * tq, :].astype(F32)
            norms2 = jnp.dot((kf * kf).astype(BF16), selector, preferred_element_type=F32)
            top = jnp.max(norms2, axis=0, keepdims=True)
            group_norm.append([jnp.sqrt(jnp.sum(jnp.where(lane_k == c, top, 0.0), axis=1, keepdims=True))
                               * NORM_SLACK for c in range(2)])
        for c in range(2):
            running = jnp.zeros((1, 1), F32)
            row = jnp.zeros((1, LANES), F32)
            for g in range(n_groups):
                running = jnp.maximum(running, group_norm[g][c])
                row = jnp.where(lane_k == g, running, row)
            km_ref[0, c] = row
            running = jnp.zeros((1, 1), F32)
            row = jnp.zeros((1, LANES), F32)
            for g in reversed(range(n_groups)):
                running = jnp.maximum(running, group_norm[g][c])
                row = jnp.where(lane_k == g, running, row)
            km_ref[1, c] = row

    zeros = jnp.zeros((HEAD_DIM, tq), BF16)
    q_maps = (jnp.concatenate([qt_ref[0], zeros], axis=0),
              jnp.concatenate([zeros, qt_ref[1]], axis=0))
    for c in range(2):
        qx_ref[0, c, :2 * HEAD_DIM, :] = q_maps[c]
        qx_ref[1, c, :2 * HEAD_DIM, :] = q_maps[c]
    m_ref[...] = jnp.full_like(m_ref, NEG)
    acc_ref[...] = jnp.zeros_like(acc_ref)
    kaug = kaug_ref[...]

    n_k = n_groups * per_q

    def qk_stage(j, slot):
        jc = jnp.clip(j, 0, n_k - 1)
        start = pl.multiple_of(jc * tk, tk)
        kb = jnp.concatenate([k_ref[pl.ds(start, tk), :], kaug], axis=1)
        after = (jc >= (i + 1) * per_q).astype(jnp.int32)
        for c in range(2):
            s_ref[slot, c] = jnp.dot(kb, qx_ref[after, c], preferred_element_type=F32)

    ones = jnp.ones((SUM_ROWS, tk), BF16)

    def softmax_stage(j, slot, after, fix):
        start = pl.multiple_of(j * tk, tk)
        vb = jnp.concatenate([vt_ref[:, pl.ds(start, tk)], ones], axis=0)
        delta = (i * tq - start).astype(F32)
        off = slope2 * (-delta if after else delta)
        for c in range(2):
            s = s_ref[slot, c]
            if fix is not None:
                s = s - fix_ref[fix]
            m_old = m_ref[c]
            m_new = jnp.maximum(m_old, jnp.max(s, axis=0, keepdims=True) - off)
            alpha = jnp.exp2(m_old - m_new)
            p = jnp.exp2(s - (m_new + off))
            acc_ref[c] = alpha * acc_ref[c] + jnp.dot(vb, p.astype(BF16), preferred_element_type=F32)
            m_ref[c] = m_new

    def fixed_max_stage(j, slot, after):
        start = pl.multiple_of(j * tk, tk)
        vb = jnp.concatenate([vt_ref[:, pl.ds(start, tk)], ones], axis=0)
        delta = (i * tq - start).astype(F32)
        off = slope2 * (-delta if after else delta)
        for c in range(2):
            p = jnp.exp2(s_ref[slot, c] - (m_ref[c] + off))
            acc_ref[c] = acc_ref[c] + jnp.dot(vb, p.astype(BF16), preferred_element_type=F32)

    def pair(j0, next_j0, after, diagonal, fixed_max=False):
        def stage(j, slot, fix):
            if fixed_max:
                fixed_max_stage(j, slot, after)
            else:
                softmax_stage(j, slot, after, fix)
        qk_stage(j0 + 1, 1)
        stage(j0, 0, 0 if diagonal else None)
        qk_stage(next_j0, 0)
        stage(j0 + 1, 1, 1 if diagonal else None)

    def run_groups(first, step, count, after, then_group, fixed_max):
        def body(unroll, base):
            def run(t, carry):
                for u in range(unroll):
                    done = base + t * unroll + u
                    g = first + step * done
                    nxt = jnp.where(done == count - 1, then_group, g + step)
                    pair(g * per_q, nxt * per_q, after, False, fixed_max)
                return carry
            return run
        done = 0
        for unroll in (DIFF_UNROLL, 2, 1):
            trips = (count - done) // unroll
            lax.fori_loop(0, trips, body(unroll, done), 0)
            done = done + trips * unroll

    qk_stage(i * per_q, 0)
    pair(i * per_q, (i - 1) * per_q, False, True)

    qn = [jnp.sqrt(jnp.sum(jnp.square(qt_ref[c].astype(F32)), axis=0, keepdims=True)) for c in range(2)]

    step_t = lax.broadcasted_iota(jnp.int32, (n_groups, 1), 0)
    step_grid = lax.broadcasted_iota(jnp.int32, (n_groups, LANES), 0)
    lane_grid = lax.broadcasted_iota(jnp.int32, (n_groups, LANES), 1)
    penalty = slope2 * (step_t * tq + 1).astype(F32)

    def groups_needed(side, available):
        group_of_step = (i - 1 - step_grid) if side == 0 else (i + 1 + step_grid)
        pick = lane_grid == group_of_step
        need = None
        excess = jnp.full((1, tq), NEG, F32)
        for c in range(2):
            km = jnp.sum(jnp.where(pick, km_ref[side, c], 0.0), axis=1, keepdims=True)
            bound = qn[c] * km - penalty
            reach = bound >= (m_ref[c] - SKIP_MARGIN)
            need = reach if need is None else (need | reach)
            excess = jnp.maximum(excess, bound[0:1] - m_ref[c])
        cnt = jnp.sum(jnp.where(need & (step_t < available), 1.0, 0.0), axis=0, keepdims=True)
        return jnp.max(cnt).astype(jnp.int32), jnp.max(excess) <= FAST_MARGIN

    n_before, calm_before = groups_needed(0, i)
    n_after, calm_after = groups_needed(1, n_groups - 1 - i)

    def run_side(first, step, count, after, then_group, calm):
        @pl.when(calm)
        def _():
            run_groups(first, step, count, after, then_group, True)

        @pl.when(jnp.logical_not(calm))
        def _():
            run_groups(first, step, count, after, then_group, False)

    run_side(i - 1, -1, n_before, False, i + 1, calm_before)

    @pl.when((n_before == 0) & (n_after > 0))
    def _():
        qk_stage((i + 1) * per_q, 0)

    run_side(i + 1, 1, n_after, True, i + 1, calm_after)

    lam = lam_ref[...]
    lam_full = (jnp.exp(jnp.sum(lam[0:1] * lam[1:2], axis=-1, keepdims=True))
                - jnp.exp(jnp.sum(lam[2:3] * lam[3:4], axis=-1, keepdims=True)) + lambda_init)
    vdim = vt_ref.shape[0]
    o = (acc_ref[0, :vdim] / acc_ref[0, vdim:vdim + 1]
         - lam_full * (acc_ref[1, :vdim] / acc_ref[1, vdim:vdim + 1]))
    ms = jnp.mean(o * o, axis=0, keepdims=True)
    on = o * lax.rsqrt(ms + SUBLN_EPS) * subg_ref[...] * (1.0 - lambda_init)
    o_ref[...] = on.T.astype(BF16)


def _slope_pieces(slopes):
    s2 = slopes * LOG2E
    c1 = s2.astype(BF16).astype(F32)
    c2 = (s2 - c1).astype(BF16).astype(F32)
    c3 = (s2 - c1 - c2).astype(BF16).astype(F32)
    return jnp.stack([c1, c2, c3], axis=1).reshape(-1)


def _diff_attn(qt, k, vt, slopes, lam, subg, lambda_init):
    batch, heads, _, _, seq = qt.shape
    vdim = vt.shape[2]
    tq = DIFF_Q
    assert tq == 2 * 256 and DIFF_K == 256 and seq % tq == 0
    slopes = _slope_pieces(slopes)
    smem = pl.BlockSpec(memory_space=pltpu.SMEM)
    return pl.pallas_call(
        functools.partial(_diff_attn_kernel, lambda_init=lambda_init),
        grid=(batch, heads, seq // tq),
        in_specs=[smem,
                  pl.BlockSpec(lam.shape, lambda b, h, i: (0, 0)),
                  pl.BlockSpec(subg.shape, lambda b, h, i: (0, 0)),
                  pl.BlockSpec((None, None, 2, HEAD_DIM, tq), lambda b, h, i: (b, h, 0, 0, i)),
                  pl.BlockSpec((None, seq, 2 * HEAD_DIM), lambda b, h, i: (b, 0, h)),
                  pl.BlockSpec((None, None, vdim, seq), lambda b, h, i: (b, h, 0, 0))],
        out_specs=pl.BlockSpec((None, tq, vdim), lambda b, h, i: (b, i, h)),
        out_shape=jax.ShapeDtypeStruct((batch, seq, heads * vdim), BF16),
        scratch_shapes=[pltpu.VMEM((2, 1, tq), F32),
                        pltpu.VMEM((2, vdim + SUM_ROWS, tq), F32),
                        pltpu.VMEM((2, 2, 2 * HEAD_DIM + LANES, tq), BF16),
                        pltpu.VMEM((2, 2, DIFF_K, tq), F32),
                        pltpu.VMEM((tq // DIFF_K, DIFF_K, tq), F32),
                        pltpu.VMEM((DIFF_K, LANES), BF16),
                        pltpu.VMEM((2, 2, 1, LANES), F32)],
        compiler_params=_params("arbitrary", "arbitrary", "arbitrary"),
        name="diff_attn",
    )(slopes, lam, subg, qt, k, vt)


def _alibi_slopes(n_heads):
    h = jnp.arange(1, n_heads + 1, dtype=F32)
    return jnp.exp2(-8.0 * h / n_heads)


def _lambda_init(layer_idx):
    return 0.8 - 0.6 * math.exp(-0.3 * layer_idx)


def kernel(x, norm_mix, norm_ffn, swa_w_qkv, swa_w_o, swa_sink, diff_w_qkv, diff_w_o, diff_lambda,
           diff_subln, moe_w_group, moe_b_group, moe_w_router, moe_b_router, moe_w_gate, moe_w_up,
           moe_w_down, final_norm):
    batch, seq, d = x.shape
    n = batch * seq
    depth = norm_mix.shape[0]
    x2 = x.reshape(n, d)
    q_scale = HEAD_DIM ** -0.5 * LOG2E
    for layer in range(depth):
        j = layer // 2
        g_mix = norm_mix[layer][None, :]
        x3 = x2.reshape(batch, seq, d)
        if layer % 2 == 0:
            n_heads = SWA_KV_HEADS * SWA_GROUP
            qw = n_heads * HEAD_DIM
            kw = SWA_KV_HEADS * HEAD_DIM
            w = swa_w_qkv[j]
            wq = (w[:, :qw] * q_scale).astype(BF16)
            wk = w[:, qw:qw + kw].astype(BF16)
            wv = w[:, qw + kw:].astype(BF16)
            qt, k, vt = _qkv_proj(x3, g_mix, wq, wk, wv, "swa_proj")
            attn = _swa_attn(qt, k, vt, _alibi_slopes(n_heads), swa_sink[j].astype(F32))
            w_o = swa_w_o[j].astype(BF16)
        else:
            heads = d // (2 * HEAD_DIM)
            w = diff_w_qkv[j]
            wq = (w[:, :d] * q_scale).astype(BF16)
            wk = w[:, d:2 * d].astype(BF16)
            wv = w[:, 2 * d:].astype(BF16)
            qt, k, vt = _qkv_proj(x3, g_mix, wq, wk, wv, "diff_proj")
            qt = qt.reshape(batch, heads, 2, HEAD_DIM, seq)
            vt = vt.reshape(batch, heads, 2 * HEAD_DIM, seq)
            attn = _diff_attn(qt, k, vt, _alibi_slopes(heads), diff_lambda[j].astype(F32),
                              diff_subln[j].astype(F32)[:, None], _lambda_init(layer))
            attn = attn.reshape(n, d)
            w_o = diff_w_o[j].astype(BF16)
        last = layer == depth - 1
        x2 = _attn_out_moe(attn, w_o, x2, norm_ffn[layer][None, :], moe_w_group[layer], moe_b_group[layer],
                           moe_w_router[layer], moe_b_router[layer], moe_w_gate, moe_w_up, moe_w_down,
                           layer, final_norm[None, :], last)
    return x2.reshape(batch, seq, d)
```
